```python
import jax
import jax.numpy as jnp
from jax import lax
import numpy as np

D_MODEL = 1024
BATCH = 8
SEQ = 8192
DEPTH = 1

GRID_W = 64
CTX_LEN = 256
A_HEAD_DIM = 64
A_HEADS = D_MODEL // A_HEAD_DIM
A_WIDTH = A_HEADS * A_HEAD_DIM
A_DECAY_LORA = 64
A_ICLR_LORA = 64
A_CONV = 3
R_QK_DIM = 256
R_HEADS = D_MODEL // R_QK_DIM
R_V_DIM = 2 * R_QK_DIM
R_QK_WIDTH = R_HEADS * R_QK_DIM
R_V_WIDTH = R_HEADS * R_V_DIM
R_CHUNK = 128
ROPE_BASE = 10000.0
NORM_EPS = 1e-6
A_GN_EPS = 64e-5
R_GN_EPS = 1e-5
L2_EPS = 1e-12
IN_SIZES = (3 * A_WIDTH, A_WIDTH, 2 * A_DECAY_LORA, 2 * A_ICLR_LORA,
            R_QK_WIDTH, R_QK_WIDTH, R_V_WIDTH, R_V_WIDTH, D_MODEL, D_MODEL)
N_IN = sum(IN_SIZES)

kernel_name = 'hybrid_rwkv7_retention_prefix_block'


def rms_norm(x, w):
    xf = x.astype(jnp.float32)
    y = xf * lax.rsqrt(jnp.mean(xf * xf, axis=-1, keepdims=True) + NORM_EPS)
    return (y * w).astype(x.dtype)


def head_norm(x, w, b, heads, eps):
    B, L, C = x.shape
    xf = x.astype(jnp.float32).reshape(B, L, heads, C // heads)
    mu = jnp.mean(xf, axis=-1, keepdims=True)
    var = jnp.mean(jnp.square(xf - mu), axis=-1, keepdims=True)
    y = ((xf - mu) * lax.rsqrt(var + eps)).reshape(B, L, C)
    return (y * w + b).astype(x.dtype)


def split_cols(p):
    offs, o = [], 0
    for s in IN_SIZES[:-1]:
        o += s
        offs.append(o)
    return jnp.split(p, offs, axis=-1)


def conv_centred(x, w):
    L = x.shape[1]
    pad = A_CONV // 2
    xp = jnp.pad(x, ((0, 0), (pad, A_CONV - 1 - pad), (0, 0)))
    y = xp[:, 0:L] * w[0]
    for j in range(1, A_CONV):
        y = y + xp[:, j:j + L] * w[j]
    return y


def rope1d(x, pos):
    d = x.shape[-1]
    half = d // 2
    freqs = ROPE_BASE ** (-jnp.arange(half, dtype=jnp.float32) / half)
    ang = pos.astype(jnp.float32)[:, None] * freqs[None, :]
    cos = jnp.cos(ang)[None, :, None, :]
    sin = jnp.sin(ang)[None, :, None, :]
    xf = x.astype(jnp.float32)
    x1, x2 = xf[..., :half], xf[..., half:]
    return jnp.concatenate([x1 * cos - x2 * sin, x1 * sin + x2 * cos], axis=-1)


def rope2d(x, rows, cols):
    half = x.shape[-1] // 2
    return jnp.concatenate([rope1d(x[..., :half], rows), rope1d(x[..., half:], cols)], axis=-1)


def rwkv7_scan(r, w, k, v, kk, a, S0, reverse):
    tm = lambda t: jnp.moveaxis(t.astype(jnp.float32), 1, 0)
    with_out = r is not None
    xs = (tm(w), tm(k), tm(v), tm(kk), tm(kk * a)) + ((tm(r),) if with_out else ())

    def step(S, inp):
        w_t, k_t, v_t, kk_t, b_t = inp[:5]
        sa = jnp.einsum('bhvk,bhk->bhv', S, kk_t)
        S = (S * w_t[:, :, None, :] - sa[..., None] * b_t[:, :, None, :]
             + v_t[..., None] * k_t[:, :, None, :])
        o = jnp.einsum('bhvk,bhk->bhv', S, inp[5]) if with_out else None
        return S, o

    S, o = lax.scan(step, S0, xs, reverse=reverse)
    return S, (jnp.moveaxis(o, 0, 1) if with_out else None)


def rwkv_branch(rkv, g, lo_w, lo_a, S0, with_out,
                conv_w, w_up, w0, a_up, a0, k_k, k_a, r_k, ln_w, ln_b, w_o):
    B, L, _ = rkv.shape
    heads = lambda t: t.reshape(B, L, A_HEADS, A_HEAD_DIM)
    r, k, v = jnp.split(conv_centred(rkv, conv_w), 3, axis=-1)
    kk = heads((k * k_k).astype(jnp.float32))
    kk = kk * lax.rsqrt(jnp.sum(kk * kk, axis=-1, keepdims=True) + L2_EPS)
    lo_w = lo_w.astype(jnp.float32)
    lo_a = lo_a.astype(jnp.float32)
    rh = heads(r) if with_out else None
    vh = heads(v)
    states, outs, bons = [], [], []
    for d, (lw, la) in enumerate(zip(jnp.split(lo_w, 2, axis=-1), jnp.split(lo_a, 2, axis=-1))):
        z = w0[d] + jnp.tanh(lw) @ w_up[d]
        w = jnp.exp(-jnp.exp(-jax.nn.softplus(-z) - 0.5))
        a = jax.nn.sigmoid(a0[d] + la @ a_up[d])
        kd = heads(k * (1.0 + (a - 1.0) * k_a))
        S, o = rwkv7_scan(rh, heads(w), kd, vh, kk, heads(a), S0[d], reverse=(d == 1))
        states.append(S)
        if with_out:
            outs.append(o)
            bons.append(jnp.sum(rh * kd * r_k, axis=-1, keepdims=True) * vh)
    if not with_out:
        return None, states
    o = head_norm((outs[0] + outs[1]).reshape(B, L, A_WIDTH), ln_w, ln_b, A_HEADS, A_GN_EPS)
    o = o + (bons[0] + bons[1]).reshape(B, L, A_WIDTH).astype(o.dtype)
    return ((o * jax.nn.silu(g)) @ w_o).astype(g.dtype), states


def retention_chunked(q, k, v, lg, S0):
    B, H, L, _ = k.shape
    dv = v.shape[-1]
    n = L // R_CHUNK
    chunks = lambda t: jnp.moveaxis(t.reshape(B, H, n, R_CHUNK, t.shape[-1]), 2, 0)
    idx = jnp.arange(R_CHUNK, dtype=jnp.float32)
    diff = idx[:, None] - idx[None, :]
    intra = jnp.exp(jnp.where(diff >= 0, lg[:, None, None] * diff, -jnp.inf))
    q_dec = jnp.exp(lg[:, None] * (idx + 1.0))[None, :, :, None]
    k_dec = jnp.exp(lg[:, None] * (R_CHUNK - 1.0 - idx))[None, :, :, None]
    c_dec = jnp.exp(lg * R_CHUNK)[None, :, None, None]
    with_out = q is not None
    xs = (chunks(k), chunks(v)) + ((chunks(q),) if with_out else ())

    def step(S, inp):
        kc, vc = inp[0], inp[1]
        S_new = S * c_dec + jnp.einsum('bhck,bhcv->bhkv', kc * k_dec, vc)
        if not with_out:
            return S_new, None
        qc = inp[2]
        scores = jnp.einsum('bhik,bhjk->bhij', qc, kc) * intra
        o = (jnp.einsum('bhij,bhjv->bhiv', scores, vc)
             + jnp.einsum('bhik,bhkv->bhiv', qc * q_dec, S))
        return S_new, o

    S, o = lax.scan(step, S0, xs)
    if with_out:
        o = jnp.moveaxis(o, 0, 2).reshape(B, H, L, dv)
    return o, S


def retention_branch(q, k, v, g, pos, S0, with_out, r_decay, ln_w, ln_b, w_o):
    B, L, _ = k.shape

    def heads(t, dh, rotate):
        t = t.reshape(B, L, R_HEADS, dh)
        if rotate and pos is not None:
            t = rope2d(t, pos[0], pos[1])
        return jnp.moveaxis(t.astype(jnp.float32), 2, 1)

    lg = -jnp.exp(r_decay.astype(jnp.float32))
    kh = heads(k, R_QK_DIM, True) * (R_QK_DIM ** -0.5)
    vh = heads(v, R_V_DIM, False)
    qh = heads(q, R_QK_DIM, True) if with_out else None
    flip = lambda t: None if t is None else t[:, :, ::-1]
    o_f, S_f = retention_chunked(qh, kh, vh, lg[0], S0[0])
    o_b, S_b = retention_chunked(flip(qh), flip(kh), flip(vh), lg[1], S0[1])
    if not with_out:
        return None, [S_f, S_b]
    o = jnp.moveaxis(o_f + flip(o_b), 1, 2).reshape(B, L, R_V_WIDTH)
    o = head_norm(o, ln_w, ln_b, R_HEADS, R_GN_EPS)
    return ((o * jax.nn.silu(g)) @ w_o).astype(g.dtype), [S_f, S_b]


def setup_inputs(seed: int = 0) -> dict:
    key = jax.random.key(seed)
    ks = jax.random.split(key, 26)
    f32 = jnp.float32
    nrm = lambda k, shape, s: jax.random.normal(k, shape, f32) * s
    D = D_MODEL
    ramp = -6.0 + 5.0 * jnp.linspace(0.0, 1.0, A_WIDTH, dtype=f32) ** 0.9
    base_decay = jnp.log(-jnp.log(1.0 - 2.0 ** (-5.0 - jnp.arange(R_HEADS, dtype=f32))))
    conv_base = jnp.array([0.25, 0.75, 0.25], f32)[:, None]
    return {
        'x': nrm(ks[0], (BATCH, SEQ, D), 1.0),
        'c': nrm(ks[1], (BATCH, D), 1.0),
        'ctx': nrm(ks[2], (BATCH, CTX_LEN, D), 1.0),
        'c_ctx': nrm(ks[3], (D,), 1.0),
        'norm_w': 1.0 + nrm(ks[4], (DEPTH, D), 0.02),
        'ada_w': nrm(ks[5], (DEPTH, D, 3 * D), D ** -0.5),
        'ada_b': nrm(ks[6], (DEPTH, 3 * D), 0.02),
        'w_in': nrm(ks[7], (DEPTH, D, N_IN), D ** -0.5),
        'a_conv': conv_base + nrm(ks[8], (DEPTH, A_CONV, 3 * A_WIDTH), 0.1),
        'a_w_up': nrm(ks[9], (DEPTH, 2, A_DECAY_LORA, A_WIDTH), 0.1),
        'a_w0': ramp + nrm(ks[10], (DEPTH, 2, A_WIDTH), 0.1),
        'a_a_up': nrm(ks[11], (DEPTH, 2, A_ICLR_LORA, A_WIDTH), 0.5 * A_ICLR_LORA ** -0.5),
        'a_a0': nrm(ks[12], (DEPTH, 2, A_WIDTH), 0.1),
        'a_k_k': 0.85 + nrm(ks[13], (DEPTH, A_WIDTH), 0.05),
        'a_k_a': 1.0 + nrm(ks[14], (DEPTH, A_WIDTH), 0.05),
        'a_r_k': nrm(ks[15], (DEPTH, A_HEADS, A_HEAD_DIM), 0.1),
        'a_ln_w': 1.0 + nrm(ks[16], (DEPTH, A_WIDTH), 0.02),
        'a_ln_b': nrm(ks[17], (DEPTH, A_WIDTH), 0.02),
        'a_w_out': nrm(ks[18], (DEPTH, A_WIDTH, D), A_WIDTH ** -0.5),
        'r_decay': base_decay + nrm(ks[19], (DEPTH, 2, R_HEADS), 0.05),
        'r_ln_w': 1.0 + nrm(ks[20], (DEPTH, R_V_WIDTH), 0.02),
        'r_ln_b': nrm(ks[21], (DEPTH, R_V_WIDTH), 0.02),
        'r_w_out': nrm(ks[22], (DEPTH, R_V_WIDTH, D), R_V_WIDTH ** -0.5),
        'w_out': nrm(ks[23], (DEPTH, D, D), D ** -0.5),
        'final_norm_w': 1.0 + nrm(ks[24], (D,), 0.02),
    }


def reference(x, c, ctx, c_ctx, norm_w, ada_w, ada_b, w_in, a_conv, a_w_up, a_w0, a_a_up, a_a0,
              a_k_k, a_k_a, a_r_k, a_ln_w, a_ln_b, a_w_out, r_decay, r_ln_w, r_ln_b, r_w_out,
              w_out, final_norm_w):
    B, L, _ = x.shape
    ROWS = L // GRID_W
    rows = jnp.repeat(jnp.arange(ROWS), GRID_W)
    cols = jnp.tile(jnp.arange(GRID_W), ROWS)
    Bc = ctx.shape[0]
    Sa0 = [jnp.zeros((Bc, A_HEADS, A_HEAD_DIM, A_HEAD_DIM), jnp.float32)] * 2
    Sr0 = [jnp.zeros((Bc, R_HEADS, R_QK_DIM, R_V_DIM), jnp.float32)] * 2
    for l in range(DEPTH):
        need_ctx = l < DEPTH - 1
        shift, scale, gate = jnp.split(jax.nn.silu(c) @ ada_w[l] + ada_b[l], 3, axis=-1)
        shift_c, scale_c, gate_c = jnp.split(jax.nn.silu(c_ctx) @ ada_w[l] + ada_b[l], 3, axis=-1)
        u = rms_norm(x, norm_w[l]) * (1.0 + scale[:, None, :]) + shift[:, None, :]
        uc = rms_norm(ctx, norm_w[l]) * (1.0 + scale_c) + shift_c
        rkv, ga, lw, la, q, k, v, gr, ma, mb = split_cols(u @ w_in[l])
        rkv_c, ga_c, lw_c, la_c, q_c, k_c, v_c, gr_c, ma_c, mb_c = split_cols(uc @ w_in[l])
        a_args = (a_conv[l], a_w_up[l], a_w0[l], a_a_up[l], a_a0[l], a_k_k[l], a_k_a[l], a_r_k[l],
                  a_ln_w[l], a_ln_b[l], a_w_out[l])
        r_args = (r_decay[l], r_ln_w[l], r_ln_b[l], r_w_out[l])
        ya_c, Sa_c = rwkv_branch(rkv_c, ga_c, lw_c, la_c, Sa0, need_ctx, *a_args)
        yr_c, Sr_c = retention_branch(q_c, k_c, v_c, gr_c, None, Sr0, need_ctx, *r_args)
        ya, _ = rwkv_branch(rkv, ga, lw, la, Sa_c, True, *a_args)
        yr, _ = retention_branch(q, k, v, gr, (rows, cols), Sr_c, True, *r_args)
        merged = jax.nn.sigmoid(ma) * ya + jax.nn.sigmoid(mb) * yr
        x = x + gate[:, None, :] * (merged @ w_out[l])
        if need_ctx:
            merged_c = jax.nn.sigmoid(ma_c) * ya_c + jax.nn.sigmoid(mb_c) * yr_c
            ctx = ctx + gate_c * (merged_c @ w_out[l])
    return rms_norm(x, final_norm_w)
```

```python
import functools
import math

import jax
import jax.numpy as jnp
from jax import lax
from jax.experimental import pallas as pl
from jax.experimental.pallas import tpu as pltpu

F32 = jnp.float32
BF16 = jnp.bfloat16

D_MODEL = 1024
A_HEAD = 64
A_HEADS = D_MODEL // A_HEAD
A_LORA = 64
QUAD = 4 * A_HEAD
N_QUADS = D_MODEL // QUAD
A_CHUNK = 64
R_HEADS = 4
R_QK = 256
R_V = 512
R_CHUNK = 128
GRID_W = 64
ROPE_BASE = 10000.0
NORM_EPS = 1e-6
A_GN_EPS = 64e-5
R_GN_EPS = 1e-5
L2_EPS = 1e-12
DECAY_SCALE = math.exp(-0.5)

OFF_RKV, OFF_GA, OFF_Q, OFF_K, OFF_V, OFF_GR, OFF_MA, OFF_MB, OFF_LO = (
    0, 3072, 4096, 5120, 6144, 8192, 10240, 11264, 12288)
N_PROJ = 12544
V7X_VMEM_LIMIT = 56 * 1024 * 1024


def _cparams(sem):
    return pltpu.CompilerParams(dimension_semantics=sem, vmem_limit_bytes=V7X_VMEM_LIMIT)


def _dot(a, b):
    return jnp.dot(a, b, preferred_element_type=F32)


def _dot_nt(a, b):
    return lax.dot_general(a, b, (((1,), (1,)), ((), ())), preferred_element_type=F32)


def _dot_tn(a, b):
    return lax.dot_general(a, b, (((0,), (0,)), ((), ())), preferred_element_type=F32)


def _split(x):
    hi = x.astype(BF16)
    lo = (x - hi.astype(F32)).astype(BF16)
    return hi, lo


def _dot_exact_rhs(x, m):
    hi, lo = _split(x)
    return _dot(hi, m) + _dot(lo, m)


def _dot_exact_lhs(m, x):
    hi, lo = _split(x)
    return _dot(m, hi) + _dot(m, lo)


def _dot3(a, b):
    ah, al = _split(a)
    bh, bl = _split(b)
    return _dot(ah, bh) + _dot(ah, bl) + _dot(al, bh)


def _sigmoid(x):
    return 1.0 / (1.0 + jnp.exp(-x))


def _silu(x):
    return x * _sigmoid(x)


def _adaln_kernel(c_ref, w_ref, b_ref, o_ref):
    cs = _silu(c_ref[...])
    o_ref[...] = _dot3(cs, w_ref[...]) + b_ref[...]


def _adaln(cc, ada_w, ada_b):
    n = ada_w.shape[1]
    tn = 512
    return pl.pallas_call(
        _adaln_kernel,
        grid=(n // tn,),
        in_specs=[pl.BlockSpec(cc.shape, lambda j: (0, 0)),
                  pl.BlockSpec((ada_w.shape[0], tn), lambda j: (0, j)),
                  pl.BlockSpec((1, tn), lambda j: (0, j))],
        out_specs=pl.BlockSpec((cc.shape[0], tn), lambda j: (0, j)),
        out_shape=jax.ShapeDtypeStruct((cc.shape[0], n), F32),
        compiler_params=_cparams(("arbitrary",)),
        name="adaln",
    )(cc, ada_w, ada_b.reshape(1, n))


def _inproj_kernel(x_ref, sc_ref, sh_ref, nw_ref, w_ref, o_ref, u_ref):
    @pl.when(pl.program_id(2) == 0)
    def _():
        xf = x_ref[0]
        y = xf * lax.rsqrt(jnp.mean(xf * xf, axis=-1, keepdims=True) + NORM_EPS) * nw_ref[...]
        u_ref[...] = (y * sc_ref[0] + sh_ref[0]).astype(BF16)

    o_ref[0] = _dot(u_ref[...], w_ref[...]).astype(o_ref.dtype)


def _inproj(x, scale1p, shift, norm_w, w_bf16, tm):
    b, l, d = x.shape
    n = w_bf16.shape[1]
    tn = 1792
    return pl.pallas_call(
        _inproj_kernel,
        grid=(b, l // tm, n // tn),
        in_specs=[pl.BlockSpec((1, tm, d), lambda bi, i, j: (bi, i, 0)),
                  pl.BlockSpec((1, 1, d), lambda bi, i, j: (bi, 0, 0)),
                  pl.BlockSpec((1, 1, d), lambda bi, i, j: (bi, 0, 0)),
                  pl.BlockSpec((1, d), lambda bi, i, j: (0, 0)),
                  pl.BlockSpec((d, tn), lambda bi, i, j: (0, j))],
        out_specs=pl.BlockSpec((1, tm, tn), lambda bi, i, j: (bi, i, j)),
        out_shape=jax.ShapeDtypeStruct((b, l, n), BF16),
        scratch_shapes=[pltpu.VMEM((tm, d), BF16)],
        compiler_params=_cparams(("arbitrary", "arbitrary", "arbitrary")),
        name="inproj",
    )(x, scale1p, shift, norm_w.reshape(1, d), w_bf16)


def _head_masks(rows, dtype):
    lane_head = lax.broadcasted_iota(jnp.int32, (rows, QUAD), 1) // A_HEAD
    return [(lane_head == h).astype(dtype) for h in range(4)]


def _bd4(x_bf16, masks):
    return jnp.concatenate([x_bf16 * m for m in masks], axis=0)


def _fold4(z, masks_f32):
    acc = z[0:A_HEAD] * masks_f32[0]
    for h in range(1, 4):
        acc = acc + z[h * A_HEAD:(h + 1) * A_HEAD] * masks_f32[h]
    return acc


def _rwkv_par_kernel(r_ref, k_ref, v_ref, rp_ref, kp_ref, vp_ref, rn_ref, kn_ref, vn_ref, lo_ref,
                     cr_ref, ck_ref, cv_ref, kkw_ref, kaw_ref, rkw_ref, w0_ref, a0_ref, wup_ref, aup_ref,
                     rtf_ref, oif_ref, phf_ref, dlf_ref, rtb_ref, oib_ref, phb_ref, dlb_ref, bon_ref,
                     *, tt, n_tiles):
    i = pl.program_id(1)
    row = lax.broadcasted_iota(jnp.int32, (tt, QUAD), 0)
    has_prev = (i > 0).astype(F32)
    has_next = (i < n_tiles - 1).astype(F32)

    def conv(x_ref, p_ref, n_ref, c_ref):
        x = x_ref[0].astype(F32)
        prev_row = p_ref[0][15:16, :].astype(F32) * has_prev
        next_row = n_ref[0][0:1, :].astype(F32) * has_next
        x_prev = jnp.where(row == 0, prev_row, pltpu.roll(x, 1, axis=0))
        x_next = jnp.where(row == tt - 1, next_row, pltpu.roll(x, tt - 1, axis=0))
        cw = c_ref[...]
        return x_prev * cw[0:1] + x * cw[1:2] + x_next * cw[2:3]

    r = conv(r_ref, rp_ref, rn_ref, cr_ref)
    k = conv(k_ref, kp_ref, kn_ref, ck_ref)
    v = conv(v_ref, vp_ref, vn_ref, cv_ref)

    li = lax.broadcasted_iota(jnp.int32, (QUAD, QUAD), 0) // A_HEAD
    lj = lax.broadcasted_iota(jnp.int32, (QUAD, QUAD), 1) // A_HEAD
    msum = (li == lj).astype(BF16)

    kk = k * kkw_ref[...]
    kk = kk * lax.rsqrt(_dot_exact_rhs(kk * kk, msum) + L2_EPS)

    ti = lax.broadcasted_iota(jnp.int32, (tt, tt), 0)
    tj = lax.broadcasted_iota(jnp.int32, (tt, tt), 1)
    same_chunk = (ti // A_CHUNK) == (tj // A_CHUNK)
    ones_blk = same_chunk.astype(BF16)

    lo = lo_ref[0].astype(F32)
    lact = jnp.tanh(lo[:, 0:2 * A_LORA])
    la = lo[:, 2 * A_LORA:4 * A_LORA]

    masks_b = _head_masks(A_CHUNK, BF16)
    masks_f = _head_masks(A_CHUNK, F32)
    ct = lax.broadcasted_iota(jnp.int32, (A_CHUNK, QUAD), 0)
    cs = lax.broadcasted_iota(jnp.int32, (A_CHUNK, QUAD), 1) % A_HEAD
    eye_q = (ct == cs).astype(F32)
    same16 = ((ct // 16) == (cs // 16)).astype(F32)
    merge32 = (((ct // 32) == (cs // 32)) & ((ct // 16) != (cs // 16))).astype(F32)
    merge64 = ((ct // 32) != (cs // 32)).astype(F32)

    bonus = jnp.zeros((tt, QUAD), F32)
    outs = ((rtf_ref, oif_ref, phf_ref, dlf_ref), (rtb_ref, oib_ref, phb_ref, dlb_ref))
    for d in range(2):
        if d == 0:
            tri = (same_chunk & (tj <= ti)).astype(BF16)
            strict = (cs < ct).astype(F32)
            incl = (cs <= ct).astype(F32)
        else:
            tri = (same_chunk & (tj >= ti)).astype(BF16)
            strict = (cs > ct).astype(F32)
            incl = (cs >= ct).astype(F32)
        z = w0_ref[d:d + 1, :] + _dot3(lact, wup_ref[d])
        logw = -DECAY_SCALE * _sigmoid(z)
        av = _sigmoid(a0_ref[d:d + 1, :] + _dot3(la, aup_ref[d]))
        kd = k * (1.0 + (av - 1.0) * kaw_ref[...])
        bonus = bonus + _dot_exact_rhs(r * kd * rkw_ref[...], msum) * v

        cum = _dot_exact_lhs(tri, logw)
        ctot = _dot_exact_lhs(ones_blk, logw)
        e_inv = jnp.exp(-cum)
        e_g = jnp.exp(ctot - cum)
        kh_all = kk * jnp.exp(cum - logw)
        rh_all = r * jnp.exp(cum)
        kka = kk * av
        bt_all = kka * e_inv
        kt_all = kd * e_inv
        bg_all = kka * e_g
        kg_all = kd * e_g
        gam_all = jnp.exp(ctot)

        rt_ref, oi_ref, ph_ref, dl_ref = outs[d]
        for c in range(tt // A_CHUNK):
            sl = slice(c * A_CHUNK, (c + 1) * A_CHUNK)
            kh, rh, bt, kt = kh_all[sl], rh_all[sl], bt_all[sl], kt_all[sl]
            bg, kg, vc, gam = bg_all[sl], kg_all[sl], v[sl], gam_all[sl]

            lhs1 = jnp.concatenate([kh, rh], axis=0).astype(BF16)
            rhs1 = jnp.concatenate([_bd4(bt.astype(BF16), masks_b),
                                    _bd4(kt.astype(BF16), masks_b)], axis=0)
            a_all = _dot_nt(lhs1, rhs1)
            a_ab = a_all[0:A_CHUNK, 0:QUAD] * strict
            a_ak = a_all[0:A_CHUNK, QUAD:2 * QUAD] * strict
            a_rb = a_all[A_CHUNK:, 0:QUAD] * incl
            a_rk = a_all[A_CHUNK:, QUAD:2 * QUAD] * incl

            def qmm(a, w):
                return _dot(a.astype(BF16), _bd4(w.astype(BF16), masks_b))

            p = -(a_ab * same16)
            t = eye_q + p
            p = qmm(p, p)
            for _ in range(2):
                both = qmm(jnp.concatenate([p, t], axis=0), p)
                p, t = both[0:A_CHUNK], t + both[A_CHUNK:]
            t = t + qmm(t, p)
            t = t - qmm(qmm(t, a_ab * merge32), t)
            t = t - qmm(qmm(t, a_ab * merge64), t)

            v_bd = _bd4(vc.astype(BF16), masks_b)
            xa = _dot(jnp.concatenate([a_ak, a_rk], axis=0).astype(BF16), v_bd)
            x, ark_v = xa[0:A_CHUNK], xa[A_CHUNK:]
            wu = _dot(t.astype(BF16), jnp.concatenate([_bd4(kh.astype(BF16), masks_b),
                                                      _bd4(x.astype(BF16), masks_b)], axis=1))
            w, u = wu[:, 0:QUAD], wu[:, QUAD:]
            wu_b = wu.astype(BF16)
            arb = _dot(a_rb.astype(BF16), jnp.concatenate([_bd4(wu_b[:, 0:QUAD], masks_b),
                                                          _bd4(wu_b[:, QUAD:], masks_b)], axis=1))
            rt = rh - arb[:, 0:QUAD]
            oi = ark_v - arb[:, QUAD:]

            zz = _dot_tn(bg.astype(BF16), wu_b)
            z2 = _dot_tn(kg.astype(BF16), vc.astype(BF16))
            phi = eye_q * gam - _fold4(zz[:, 0:QUAD], masks_f)
            dlt = _fold4(z2, masks_f) - _fold4(zz[:, QUAD:], masks_f)

            rt_ref[0, sl, :] = rt.astype(rt_ref.dtype)
            oi_ref[0, sl, :] = oi.astype(oi_ref.dtype)
            ph_ref[0, sl, :] = phi.astype(ph_ref.dtype)
            dl_ref[0, sl, :] = dlt.astype(dl_ref.dtype)
    bon_ref[0] = bonus.astype(bon_ref.dtype)


def _rwkv_par(p, a_conv, k_k, k_a, r_k, w0, a0, wup_pad, aup_pad, tt):
    b, l, _ = p.shape
    n_tiles = l // tt
    h16 = tt // 16
    last16 = l // 16 - 1

    def tok(col0):
        return pl.BlockSpec((1, tt, QUAD), lambda bi, i, q: (bi, i, col0 // QUAD + q))

    def prev(col0):
        return pl.BlockSpec((1, 16, QUAD), lambda bi, i, q: (bi, jnp.maximum(i * h16 - 1, 0), col0 // QUAD + q))

    def nxt(col0):
        return pl.BlockSpec((1, 16, QUAD),
                            lambda bi, i, q: (bi, jnp.minimum((i + 1) * h16, last16), col0 // QUAD + q))

    def vec(rows, col0=0):
        return pl.BlockSpec((rows, QUAD), lambda bi, i, q: (0, col0 // QUAD + q))

    out_tok = pl.BlockSpec((1, tt, QUAD), lambda bi, i, q: (bi, i, q))
    out_sd = jax.ShapeDtypeStruct((b, l, D_MODEL), BF16)
    kern = functools.partial(_rwkv_par_kernel, tt=tt, n_tiles=n_tiles)
    return pl.pallas_call(
        kern,
        grid=(b, n_tiles, N_QUADS),
        in_specs=[tok(0), tok(1024), tok(2048), prev(0), prev(1024), prev(2048),
                  nxt(0), nxt(1024), nxt(2048),
                  pl.BlockSpec((1, tt, 4 * A_LORA), lambda bi, i, q: (bi, i, OFF_LO // (4 * A_LORA))),
                  vec(3, 0), vec(3, 1024), vec(3, 2048),
                  vec(1), vec(1), vec(1), vec(2), vec(2),
                  pl.BlockSpec((2, 2 * A_LORA, QUAD), lambda bi, i, q: (0, 0, q)),
                  pl.BlockSpec((2, 2 * A_LORA, QUAD), lambda bi, i, q: (0, 0, q))],
        out_specs=[out_tok] * 9,
        out_shape=[out_sd] * 9,
        compiler_params=_cparams(("arbitrary", "arbitrary", "arbitrary")),
        name="rwkv_par",
    )(p, p, p, p, p, p, p, p, p, p, a_conv, a_conv, a_conv,
      k_k.reshape(1, D_MODEL), k_a.reshape(1, D_MODEL), r_k.reshape(1, D_MODEL), w0, a0, wup_pad, aup_pad)


def _rwkv_seq_kernel(rtf_ref, oif_ref, phf_ref, dlf_ref, rtb_ref, oib_ref, phb_ref, dlb_ref,
                     h0f_ref, h0b_ref, of_ref, ob_ref, hff_ref, hfb_ref, hf_s, hb_s, *, tt):
    i = pl.program_id(2)

    @pl.when(i == 0)
    def _():
        hf_s[...] = h0f_ref[0]
        hb_s[...] = h0b_ref[0]

    masks_b = _head_masks(A_CHUNK, BF16)
    nc = tt // A_CHUNK

    def step(h, rt_ref, ph_ref, dl_ref, oi_ref, o_ref, c):
        sl = slice(c * A_CHUNK, (c + 1) * A_CHUNK)
        h_bd = _bd4(h.astype(BF16), masks_b)
        both = _dot(jnp.concatenate([rt_ref[0, sl, :], ph_ref[0, sl, :]], axis=0), h_bd)
        o_ref[0, sl, :] = (both[0:A_CHUNK] + oi_ref[0, sl, :].astype(F32)).astype(o_ref.dtype)
        return both[A_CHUNK:] + dl_ref[0, sl, :].astype(F32)

    hf = hf_s[...]
    hb = hb_s[...]
    for c in range(nc):
        hf = step(hf, rtf_ref, phf_ref, dlf_ref, oif_ref, of_ref, c)
        hb = step(hb, rtb_ref, phb_ref, dlb_ref, oib_ref, ob_ref, nc - 1 - c)
    hf_s[...] = hf
    hb_s[...] = hb
    hff_ref[0] = hf
    hfb_ref[0] = hb


def _rwkv_seq(par, h0f, h0b, tt):
    rtf, oif, phf, dlf, rtb, oib, phb, dlb = par
    b, l, _ = rtf.shape
    n_tiles = l // tt
    fwd = pl.BlockSpec((1, tt, QUAD), lambda bi, q, i: (bi, i, q))
    bwd = pl.BlockSpec((1, tt, QUAD), lambda bi, q, i: (bi, n_tiles - 1 - i, q))
    st = pl.BlockSpec((1, A_HEAD, QUAD), lambda bi, q, i: (bi, 0, q))
    o_sd = jax.ShapeDtypeStruct((b, l, D_MODEL), BF16)
    h_sd = jax.ShapeDtypeStruct((b, A_HEAD, D_MODEL), F32)
    return pl.pallas_call(
        functools.partial(_rwkv_seq_kernel, tt=tt),
        grid=(b, N_QUADS, n_tiles),
        in_specs=[fwd] * 4 + [bwd] * 4 + [st, st],
        out_specs=[fwd, bwd, st, st],
        out_shape=[o_sd, o_sd, h_sd, h_sd],
        scratch_shapes=[pltpu.VMEM((A_HEAD, QUAD), F32), pltpu.VMEM((A_HEAD, QUAD), F32)],
        compiler_params=_cparams(("arbitrary", "arbitrary", "arbitrary")),
        name="rwkv_seq",
    )(rtf, oif, phf, dlf, rtb, oib, phb, dlb, h0f, h0b)


def _ret_kernel(lg_ref, q_ref, k_ref, v_ref, cos_ref, sin_ref, s0_ref, *rest,
                tt, reverse, with_intra, n_heads):
    if with_intra:
        prev_ref, o_ref, sf_ref, s_s = rest
    else:
        o_ref, sf_ref, s_s = rest
    bh = pl.program_id(0)
    i = pl.program_id(1)
    h = bh % n_heads

    @pl.when(i == 0)
    def _():
        s_s[...] = s0_ref[0]

    lg_f = lg_ref[0, h]
    lg_b = lg_ref[1, h]
    lg = lg_b if reverse else lg_f

    def rope(x_ref):
        x = x_ref[0].astype(F32)
        parts = []
        for j in range(R_QK // 128):
            xs = x[:, j * 128:(j + 1) * 128]
            parts.append(pltpu.roll(xs, 64, axis=1))
        xr = jnp.concatenate(parts, axis=1)
        return x * cos_ref[...] + xr * sin_ref[...]

    q = rope(q_ref)
    k = rope(k_ref) * (R_QK ** -0.5)

    idx = lax.broadcasted_iota(jnp.int32, (R_CHUNK, R_QK), 0).astype(F32)
    if reverse:
        q_dec = jnp.exp(lg * (R_CHUNK - idx))
        k_dec = jnp.exp(lg * idx)
    else:
        q_dec = jnp.exp(lg * (idx + 1.0))
        k_dec = jnp.exp(lg * (R_CHUNK - 1.0 - idx))
    c_dec = jnp.exp(lg * R_CHUNK)
    if with_intra:
        di = lax.broadcasted_iota(jnp.int32, (R_CHUNK, R_CHUNK), 0)
        dj = lax.broadcasted_iota(jnp.int32, (R_CHUNK, R_CHUNK), 1)
        diff = (di - dj).astype(F32)
        dmask = (jnp.where(diff >= 0, jnp.exp(lg_f * jnp.maximum(diff, 0.0)), 0.0)
                 + jnp.where(diff <= 0, jnp.exp(lg_b * jnp.maximum(-diff, 0.0)), 0.0))

    nc = tt // R_CHUNK
    s = s_s[...]
    for cc in range(nc):
        c = nc - 1 - cc if reverse else cc
        sl = slice(c * R_CHUNK, (c + 1) * R_CHUNK)
        qc, kc = q[sl], k[sl]
        vc = v_ref[0, sl, :]
        o = _dot((qc * q_dec).astype(BF16), s.astype(BF16))
        if with_intra:
            scores = _dot_nt(qc.astype(BF16), kc.astype(BF16)) * dmask
            o = o + _dot(scores.astype(BF16), vc) + prev_ref[0, sl, :].astype(F32)
        o_ref[0, sl, :] = o.astype(o_ref.dtype)
        s = s * c_dec + _dot_tn((kc * k_dec).astype(BF16), vc)
    s_s[...] = s
    sf_ref[0] = s


def _ret_pass(p, lg, cos_t, sin_t, s0, prev, tt, reverse):
    with_intra = prev is not None
    b, l, _ = p.shape
    n_tiles = l // tt
    nh = R_HEADS

    def tile(i):
        return n_tiles - 1 - i if reverse else i

    def col(width, off):
        return pl.BlockSpec((1, tt, width), lambda bh, i, lg_: (bh // nh, tile(i), off // width + bh % nh))

    tab = pl.BlockSpec((tt, R_QK), lambda bh, i, lg_: (tile(i), 0))
    st = pl.BlockSpec((1, R_QK, R_V), lambda bh, i, lg_: (bh, 0, 0))
    o_spec = pl.BlockSpec((1, tt, R_V), lambda bh, i, lg_: (bh // nh, tile(i), bh % nh))
    kern = functools.partial(_ret_kernel, tt=tt, reverse=reverse, with_intra=with_intra, n_heads=nh)
    grid_spec = pltpu.PrefetchScalarGridSpec(
        num_scalar_prefetch=1,
        grid=(b * nh, n_tiles),
        in_specs=[col(R_QK, OFF_Q), col(R_QK, OFF_K), col(R_V, OFF_V), tab, tab, st]
        + ([o_spec] if with_intra else []),
        out_specs=[o_spec, st],
        scratch_shapes=[pltpu.VMEM((R_QK, R_V), F32)],
    )
    return pl.pallas_call(
        kern,
        grid_spec=grid_spec,
        out_shape=[jax.ShapeDtypeStruct((b, l, nh * R_V), BF16),
                   jax.ShapeDtypeStruct((b * nh, R_QK, R_V), F32)],
        compiler_params=_cparams(("arbitrary", "arbitrary")),
        name="ret_bwd" if reverse else "ret_fwd",
    )(lg, p, p, p, cos_t, sin_t, s0, *([prev] if with_intra else []))


def _final_kernel(x_ref, ga_ref, gr_ref, ma_ref, mb_ref, of_ref, ob_ref, bon_ref, or_ref,
                  gate_ref, alw_ref, alb_ref, rlw_ref, rlb_ref, fnw_ref, awo_ref, rwo_ref, wo_ref, o_ref):
    li = lax.broadcasted_iota(jnp.int32, (D_MODEL, D_MODEL), 0) // A_HEAD
    lj = lax.broadcasted_iota(jnp.int32, (D_MODEL, D_MODEL), 1) // A_HEAD
    msum = (li == lj).astype(BF16)
    inv = 1.0 / A_HEAD

    oa = of_ref[0].astype(F32) + ob_ref[0].astype(F32)
    mu = _dot_exact_rhs(oa, msum) * inv
    dv = oa - mu
    var = _dot_exact_rhs(dv * dv, msum) * inv
    ya = dv * lax.rsqrt(var + A_GN_EPS) * alw_ref[...] + alb_ref[...] + bon_ref[0].astype(F32)
    ya = _dot((ya * _silu(ga_ref[0].astype(F32))).astype(BF16), awo_ref[...])

    orr = or_ref[0].astype(F32)
    parts = []
    for h in range(R_HEADS):
        oh = orr[:, h * R_V:(h + 1) * R_V]
        m = jnp.mean(oh, axis=-1, keepdims=True)
        dh = oh - m
        vh = jnp.mean(dh * dh, axis=-1, keepdims=True)
        parts.append(dh * lax.rsqrt(vh + R_GN_EPS))
    yr = jnp.concatenate(parts, axis=1) * rlw_ref[...] + rlb_ref[...]
    yr = _dot((yr * _silu(gr_ref[0].astype(F32))).astype(BF16), rwo_ref[...])

    merged = _sigmoid(ma_ref[0].astype(F32)) * ya + _sigmoid(mb_ref[0].astype(F32)) * yr
    y = _dot(merged.astype(BF16), wo_ref[...])
    xo = x_ref[0] + gate_ref[0] * y
    o_ref[0] = xo * lax.rsqrt(jnp.mean(xo * xo, axis=-1, keepdims=True) + NORM_EPS) * fnw_ref[...]


def _final(x, p, of, ob, bonus, o_ret, gate, a_ln_w, a_ln_b, r_ln_w, r_ln_b, final_w, awo, rwo, wo, tm):
    b, l, d = x.shape

    def tok(width, off=0):
        return pl.BlockSpec((1, tm, width), lambda bi, i: (bi, i, off // width))

    def vec(width):
        return pl.BlockSpec((1, width), lambda bi, i: (0, 0))

    def mat(r, c):
        return pl.BlockSpec((r, c), lambda bi, i: (0, 0))

    rv = R_HEADS * R_V
    return pl.pallas_call(
        _final_kernel,
        grid=(b, l // tm),
        in_specs=[tok(d), tok(d, OFF_GA), tok(rv, OFF_GR), tok(d, OFF_MA), tok(d, OFF_MB),
                  tok(d), tok(d), tok(d), tok(rv),
                  pl.BlockSpec((1, 1, d), lambda bi, i: (bi, 0, 0)),
                  vec(d), vec(d), vec(rv), vec(rv), vec(d), mat(d, d), mat(rv, d), mat(d, d)],
        out_specs=tok(d),
        out_shape=jax.ShapeDtypeStruct((b, l, d), F32),
        compiler_params=_cparams(("arbitrary", "arbitrary")),
        name="final",
    )(x, p, p, p, p, of, ob, bonus, o_ret, gate,
      a_ln_w.reshape(1, d), a_ln_b.reshape(1, d), r_ln_w.reshape(1, rv), r_ln_b.reshape(1, rv),
      final_w.reshape(1, d), awo, rwo, wo)


def _rope_tables(l):
    t = jnp.arange(l)
    rows = (t // GRID_W).astype(F32)
    cols = (t % GRID_W).astype(F32)
    half = R_QK // 4
    freqs = ROPE_BASE ** (-jnp.arange(half, dtype=F32) / half)
    ar = rows[:, None] * freqs[None, :]
    ac = cols[:, None] * freqs[None, :]
    cos_t = jnp.concatenate([jnp.cos(ar), jnp.cos(ar), jnp.cos(ac), jnp.cos(ac)], axis=1)
    sin_t = jnp.concatenate([-jnp.sin(ar), jnp.sin(ar), -jnp.sin(ac), jnp.sin(ac)], axis=1)
    return cos_t, sin_t


def _reorder_w_in(w):
    return jnp.concatenate([w[:, 0:4096], w[:, 4352:], w[:, 4096:4352]], axis=1)


def _pick_tile(l, pref):
    t = min(l, pref)
    while l % t:
        t //= 2
    return t


def kernel(x, c, ctx, c_ctx, norm_w, ada_w, ada_b, w_in, a_conv, a_w_up, a_w0, a_a_up, a_a0, a_k_k, a_k_a,
           a_r_k, a_ln_w, a_ln_b, a_w_out, r_decay, r_ln_w, r_ln_b, r_w_out, w_out, final_norm_w):
    b, l, d = x.shape
    lc = ctx.shape[1]
    assert d == D_MODEL and l % R_CHUNK == 0 and lc % R_CHUNK == 0
    lyr = 0

    cc = jnp.zeros((16, d), F32).at[:b].set(c).at[b].set(c_ctx)
    mod = _adaln(cc, ada_w[lyr], ada_b[lyr])
    shift, scale, gate = mod[:b, :d], mod[:b, d:2 * d], mod[:b, 2 * d:]
    shift_c = jnp.broadcast_to(mod[b, :d], (b, d))
    scale_c = jnp.broadcast_to(mod[b, d:2 * d], (b, d))

    w_bf = _reorder_w_in(w_in[lyr]).astype(BF16)
    p_x = _inproj(x, (1.0 + scale)[:, None, :], shift[:, None, :], norm_w[lyr], w_bf, _pick_tile(l, 1024))
    p_c = _inproj(ctx, (1.0 + scale_c)[:, None, :], shift_c[:, None, :], norm_w[lyr], w_bf, _pick_tile(lc, 1024))

    zpad = jnp.zeros((A_LORA, d), F32)
    wup_pad = jnp.stack([jnp.concatenate([a_w_up[lyr, 0], zpad], 0), jnp.concatenate([zpad, a_w_up[lyr, 1]], 0)])
    aup_pad = jnp.stack([jnp.concatenate([a_a_up[lyr, 0], zpad], 0), jnp.concatenate([zpad, a_a_up[lyr, 1]], 0)])
    a_args = (a_conv[lyr], a_k_k[lyr], a_k_a[lyr], a_r_k[lyr].reshape(-1), a_w0[lyr], a_a0[lyr], wup_pad, aup_pad)
    h_zero = jnp.zeros((b, A_HEAD, d), F32)
    par_c = _rwkv_par(p_c, *a_args, tt=_pick_tile(lc, 256))
    _, _, hcf, hcb = _rwkv_seq(par_c[:8], h_zero, h_zero, _pick_tile(lc, 512))
    par_x = _rwkv_par(p_x, *a_args, tt=_pick_tile(l, 256))
    o_af, o_ab, _, _ = _rwkv_seq(par_x[:8], hcf, hcb, _pick_tile(l, 512))
    bonus = par_x[8]

    lg = -jnp.exp(r_decay[lyr].astype(F32))
    cos_x, sin_x = _rope_tables(l)
    cos_c = jnp.ones((lc, R_QK), F32)
    sin_c = jnp.zeros((lc, R_QK), F32)
    s_zero = jnp.zeros((b * R_HEADS, R_QK, R_V), F32)
    tc = _pick_tile(lc, 1024)
    _, sc_b = _ret_pass(p_c, lg, cos_c, sin_c, s_zero, None, tc, True)
    _, sc_f = _ret_pass(p_c, lg, cos_c, sin_c, s_zero, None, tc, False)
    tx = _pick_tile(l, 1024)
    o_rb, _ = _ret_pass(p_x, lg, cos_x, sin_x, sc_b, None, tx, True)
    o_r, _ = _ret_pass(p_x, lg, cos_x, sin_x, sc_f, o_rb, tx, False)

    return _final(x, p_x, o_af, o_ab, bonus, o_r, gate[:, None, :], a_ln_w[lyr], a_ln_b[lyr],
                  r_ln_w[lyr], r_ln_b[lyr], final_norm_w, a_w_out[lyr].astype(BF16),
                  r_w_out[lyr].astype(BF16), w_out[lyr].astype(BF16), _pick_tile(l, 256))
```

```python
import functools
import math

import jax
import jax.numpy as jnp
from jax import lax
from jax.experimental import pallas as pl
from jax.experimental.pallas import tpu as pltpu

F32 = jnp.float32
BF16 = jnp.bfloat16

D_MODEL = 1024
A_HEAD = 64
A_HEADS = D_MODEL // A_HEAD
A_LORA = 64
QUAD = 4 * A_HEAD
PAIR = 2 * A_HEAD
N_QUADS = D_MODEL // QUAD
A_CHUNK = 64
R_HEADS = 4
R_QK = 256
R_V = 512
R_CHUNK = 128
GRID_W = 64
ROPE_BASE = 10000.0
NORM_EPS = 1e-6
A_GN_EPS = 64e-5
R_GN_EPS = 1e-5
L2_EPS = 1e-12
DECAY_SCALE = math.exp(-0.5)

OFF_RKV, OFF_GA, OFF_Q, OFF_K, OFF_V, OFF_GR, OFF_MA, OFF_MB, OFF_LO = (
    0, 3072, 4096, 5120, 6144, 8192, 10240, 11264, 12288)
N_PROJ = 12544
V7X_VMEM_LIMIT = 56 * 1024 * 1024


def _cparams(sem):
    return pltpu.CompilerParams(dimension_semantics=sem, vmem_limit_bytes=V7X_VMEM_LIMIT)


def _dot(a, b):
    return jnp.dot(a, b, preferred_element_type=F32)


def _dot_nt(a, b):
    return lax.dot_general(a, b, (((1,), (1,)), ((), ())), preferred_element_type=F32)


def _dot_tn(a, b):
    return lax.dot_general(a, b, (((0,), (0,)), ((), ())), preferred_element_type=F32)


def _split(x):
    hi = x.astype(BF16)
    lo = (x - hi.astype(F32)).astype(BF16)
    return hi, lo


def _dot_exact_rhs(x, m):
    hi, lo = _split(x)
    return _dot(hi, m) + _dot(lo, m)


def _dot_exact_lhs(m, x):
    hi, lo = _split(x)
    return _dot(m, hi) + _dot(m, lo)


def _dot3(a, b):
    ah, al = _split(a)
    bh, bl = _split(b)
    return _dot(ah, bh) + _dot(ah, bl) + _dot(al, bh)


def _sigmoid(x):
    return 1.0 / (1.0 + jnp.exp(-x))


def _silu(x):
    return x * _sigmoid(x)


def _adaln_kernel(c_ref, w_ref, b_ref, o_ref):
    cs = _silu(c_ref[...])
    o_ref[...] = _dot3(cs, w_ref[...]) + b_ref[...]


def _adaln(cc, ada_w, ada_b):
    n = ada_w.shape[1]
    tn = 512
    return pl.pallas_call(
        _adaln_kernel,
        grid=(n // tn,),
        in_specs=[pl.BlockSpec(cc.shape, lambda j: (0, 0)),
                  pl.BlockSpec((ada_w.shape[0], tn), lambda j: (0, j)),
                  pl.BlockSpec((1, tn), lambda j: (0, j))],
        out_specs=pl.BlockSpec((cc.shape[0], tn), lambda j: (0, j)),
        out_shape=jax.ShapeDtypeStruct((cc.shape[0], n), F32),
        compiler_params=_cparams(("arbitrary",)),
        name="adaln",
    )(cc, ada_w, ada_b.reshape(1, n))


def _inproj_kernel(x_ref, sc_ref, sh_ref, nw_ref, w_ref, o_ref, u_ref):
    @pl.when(pl.program_id(2) == 0)
    def _():
        xf = x_ref[0]
        y = xf * lax.rsqrt(jnp.mean(xf * xf, axis=-1, keepdims=True) + NORM_EPS) * nw_ref[...]
        u_ref[...] = (y * sc_ref[0] + sh_ref[0]).astype(BF16)

    o_ref[0] = _dot(u_ref[...], w_ref[...]).astype(o_ref.dtype)


def _inproj(x, scale1p, shift, norm_w, w_bf16, tm):
    b, l, d = x.shape
    n = w_bf16.shape[1]
    tn = 1792
    return pl.pallas_call(
        _inproj_kernel,
        grid=(b, l // tm, n // tn),
        in_specs=[pl.BlockSpec((1, tm, d), lambda bi, i, j: (bi, i, 0)),
                  pl.BlockSpec((1, 1, d), lambda bi, i, j: (bi, 0, 0)),
                  pl.BlockSpec((1, 1, d), lambda bi, i, j: (bi, 0, 0)),
                  pl.BlockSpec((1, d), lambda bi, i, j: (0, 0)),
                  pl.BlockSpec((d, tn), lambda bi, i, j: (0, j))],
        out_specs=pl.BlockSpec((1, tm, tn), lambda bi, i, j: (bi, i, j)),
        out_shape=jax.ShapeDtypeStruct((b, l, n), BF16),
        scratch_shapes=[pltpu.VMEM((tm, d), BF16)],
        compiler_params=_cparams(("arbitrary", "arbitrary", "arbitrary")),
        name="inproj",
    )(x, scale1p, shift, norm_w.reshape(1, d), w_bf16)


def _head_masks(rows, dtype):
    lane_head = lax.broadcasted_iota(jnp.int32, (rows, PAIR), 1) // A_HEAD
    return [(lane_head == h).astype(dtype) for h in range(2)]


def _bd2(x_bf16, masks):
    return jnp.concatenate([x_bf16 * m for m in masks], axis=0)


def _fold2(z, masks_f32):
    return z[0:A_HEAD] * masks_f32[0] + z[A_HEAD:2 * A_HEAD] * masks_f32[1]


def _rwkv_par_kernel(r_ref, k_ref, v_ref, rp_ref, kp_ref, vp_ref, rn_ref, kn_ref, vn_ref, lo_ref,
                     cr_ref, ck_ref, cv_ref, kkw_ref, kaw_ref, rkw_ref, w0_ref, a0_ref, wup_ref, aup_ref,
                     rtf_ref, oif_ref, phf_ref, dlf_ref, rtb_ref, oib_ref, phb_ref, dlb_ref, bon_ref,
                     *, tt, n_tiles):
    i = pl.program_id(1)
    row = lax.broadcasted_iota(jnp.int32, (tt, QUAD), 0)
    has_prev = (i > 0).astype(F32)
    has_next = (i < n_tiles - 1).astype(F32)

    def conv(x_ref, p_ref, n_ref, c_ref):
        x = x_ref[0].astype(F32)
        prev_row = p_ref[0][15:16, :].astype(F32) * has_prev
        next_row = n_ref[0][0:1, :].astype(F32) * has_next
        x_prev = jnp.where(row == 0, prev_row, pltpu.roll(x, 1, axis=0))
        x_next = jnp.where(row == tt - 1, next_row, pltpu.roll(x, tt - 1, axis=0))
        cw = c_ref[...]
        return x_prev * cw[0:1] + x * cw[1:2] + x_next * cw[2:3]

    r = conv(r_ref, rp_ref, rn_ref, cr_ref)
    k = conv(k_ref, kp_ref, kn_ref, ck_ref)
    v = conv(v_ref, vp_ref, vn_ref, cv_ref)

    li = lax.broadcasted_iota(jnp.int32, (QUAD, QUAD), 0) // A_HEAD
    lj = lax.broadcasted_iota(jnp.int32, (QUAD, QUAD), 1) // A_HEAD
    msum = (li == lj).astype(BF16)

    kk = k * kkw_ref[...]
    kk = kk * lax.rsqrt(_dot_exact_rhs(kk * kk, msum) + L2_EPS)

    ti = lax.broadcasted_iota(jnp.int32, (tt, tt), 0)
    tj = lax.broadcasted_iota(jnp.int32, (tt, tt), 1)
    same_chunk = (ti // A_CHUNK) == (tj // A_CHUNK)
    ones_blk = same_chunk.astype(BF16)

    lo = lo_ref[0].astype(F32)
    lact = jnp.tanh(lo[:, 0:2 * A_LORA])
    la = lo[:, 2 * A_LORA:4 * A_LORA]

    masks_b = _head_masks(A_CHUNK, BF16)
    masks_f = _head_masks(A_CHUNK, F32)
    ct = lax.broadcasted_iota(jnp.int32, (A_CHUNK, PAIR), 0)
    cs = lax.broadcasted_iota(jnp.int32, (A_CHUNK, PAIR), 1) % A_HEAD
    eye_q = (ct == cs).astype(F32)
    same16 = ((ct // 16) == (cs // 16)).astype(F32)
    merge32 = (((ct // 32) == (cs // 32)) & ((ct // 16) != (cs // 16))).astype(F32)
    merge64 = ((ct // 32) != (cs // 32)).astype(F32)

    bonus = jnp.zeros((tt, QUAD), F32)
    outs = ((rtf_ref, oif_ref, phf_ref, dlf_ref), (rtb_ref, oib_ref, phb_ref, dlb_ref))
    prep = []
    for d in range(2):
        if d == 0:
            tri = (same_chunk & (tj <= ti)).astype(BF16)
            strict = (cs < ct).astype(F32)
            incl = (cs <= ct).astype(F32)
        else:
            tri = (same_chunk & (tj >= ti)).astype(BF16)
            strict = (cs > ct).astype(F32)
            incl = (cs >= ct).astype(F32)
        z = w0_ref[d:d + 1, :] + _dot3(lact, wup_ref[d])
        logw = -DECAY_SCALE * _sigmoid(z)
        av = _sigmoid(a0_ref[d:d + 1, :] + _dot3(la, aup_ref[d]))
        kd = k * (1.0 + (av - 1.0) * kaw_ref[...])
        bonus = bonus + _dot_exact_rhs(r * kd * rkw_ref[...], msum) * v

        cum = _dot_exact_lhs(tri, logw)
        ctot = _dot_exact_lhs(ones_blk, logw)
        e_inv = jnp.exp(-cum)
        e_g = jnp.exp(ctot - cum)
        kka = kk * av
        prep.append(dict(kh=kk * jnp.exp(cum - logw), rh=r * jnp.exp(cum), bt=kka * e_inv, kt=kd * e_inv,
                         bg=kka * e_g, kg=kd * e_g, gam=jnp.exp(ctot), strict=strict, incl=incl))

    units = [(d, c, pr) for d in range(2) for c in range(tt // A_CHUNK) for pr in range(QUAD // PAIR)]

    def cut(x, c, pr):
        return x[c * A_CHUNK:(c + 1) * A_CHUNK, pr * PAIR:(pr + 1) * PAIR]

    def bd(x):
        return _bd2(x.astype(BF16), masks_b)

    def pmm(a, w):
        return _dot(a.astype(BF16), bd(w))

    def stage_a(d, c, pr):
        pd = prep[d]
        kh, rh = cut(pd["kh"], c, pr), cut(pd["rh"], c, pr)
        lhs1 = jnp.concatenate([kh, rh], axis=0).astype(BF16)
        rhs1 = jnp.concatenate([bd(cut(pd["bt"], c, pr)), bd(cut(pd["kt"], c, pr))], axis=0)
        a_all = _dot_nt(lhs1, rhs1)
        return dict(kh=kh, rh=rh,
                    a_ab=a_all[0:A_CHUNK, 0:PAIR] * pd["strict"],
                    a_ak=a_all[0:A_CHUNK, PAIR:] * pd["strict"],
                    a_rb=a_all[A_CHUNK:, 0:PAIR] * pd["incl"],
                    a_rk=a_all[A_CHUNK:, PAIR:] * pd["incl"])

    st = [stage_a(*u_) for u_ in units]

    ps = [-(s_["a_ab"] * same16) for s_ in st]
    ts = [eye_q + p for p in ps]
    ps = [pmm(p, p) for p in ps]
    for _ in range(2):
        both = [pmm(jnp.concatenate([p, t], axis=0), p) for p, t in zip(ps, ts)]
        ps = [b_[0:A_CHUNK] for b_ in both]
        ts = [t + b_[A_CHUNK:] for t, b_ in zip(ts, both)]
    ts = [t + pmm(t, p) for t, p in zip(ts, ps)]
    for blk in (merge32, merge64):
        gs = [pmm(t, s_["a_ab"] * blk) for t, s_ in zip(ts, st)]
        ts = [t - pmm(g, t) for t, g in zip(ts, gs)]

    vs = [cut(v, c, pr) for d, c, pr in units]
    xas = [pmm(jnp.concatenate([s_["a_ak"], s_["a_rk"]], axis=0), vc) for s_, vc in zip(st, vs)]
    wus = [_dot(t.astype(BF16), jnp.concatenate([bd(s_["kh"]), bd(xa[0:A_CHUNK])], axis=1))
           for t, s_, xa in zip(ts, st, xas)]
    wubs = [wu.astype(BF16) for wu in wus]
    arbs = [_dot(s_["a_rb"].astype(BF16), jnp.concatenate([_bd2(wb[:, 0:PAIR], masks_b),
                                                          _bd2(wb[:, PAIR:], masks_b)], axis=1))
            for s_, wb in zip(st, wubs)]
    zzs = [_dot_tn(cut(prep[d]["bg"], c, pr).astype(BF16), wb) for (d, c, pr), wb in zip(units, wubs)]
    z2s = [_dot_tn(cut(prep[d]["kg"], c, pr).astype(BF16), vc.astype(BF16)) for (d, c, pr), vc in zip(units, vs)]

    for (d, c, pr), s_, xa, arb, zz, z2 in zip(units, st, xas, arbs, zzs, z2s):
        rt_ref, oi_ref, ph_ref, dl_ref = outs[d]
        gam = cut(prep[d]["gam"], c, pr)
        rw = slice(c * A_CHUNK, (c + 1) * A_CHUNK)
        ln = slice(pr * PAIR, (pr + 1) * PAIR)
        rt_ref[0, rw, ln] = (s_["rh"] - arb[:, 0:PAIR]).astype(rt_ref.dtype)
        oi_ref[0, rw, ln] = (xa[A_CHUNK:] - arb[:, PAIR:]).astype(oi_ref.dtype)
        ph_ref[0, rw, ln] = (eye_q * gam - _fold2(zz[:, 0:PAIR], masks_f)).astype(ph_ref.dtype)
        dl_ref[0, rw, ln] = (_fold2(z2, masks_f) - _fold2(zz[:, PAIR:], masks_f)).astype(dl_ref.dtype)
    bon_ref[0] = bonus.astype(bon_ref.dtype)


def _rwkv_par(p, a_conv, k_k, k_a, r_k, w0, a0, wup_pad, aup_pad, tt):
    b, l, _ = p.shape
    n_tiles = l // tt
    h16 = tt // 16
    last16 = l // 16 - 1

    def tok(col0):
        return pl.BlockSpec((1, tt, QUAD), lambda bi, i, q: (bi, i, col0 // QUAD + q))

    def prev(col0):
        return pl.BlockSpec((1, 16, QUAD), lambda bi, i, q: (bi, jnp.maximum(i * h16 - 1, 0), col0 // QUAD + q))

    def nxt(col0):
        return pl.BlockSpec((1, 16, QUAD),
                            lambda bi, i, q: (bi, jnp.minimum((i + 1) * h16, last16), col0 // QUAD + q))

    def vec(rows, col0=0):
        return pl.BlockSpec((rows, QUAD), lambda bi, i, q: (0, col0 // QUAD + q))

    out_tok = pl.BlockSpec((1, tt, QUAD), lambda bi, i, q: (bi, i, q))
    out_sd = jax.ShapeDtypeStruct((b, l, D_MODEL), BF16)
    kern = functools.partial(_rwkv_par_kernel, tt=tt, n_tiles=n_tiles)
    return pl.pallas_call(
        kern,
        grid=(b, n_tiles, N_QUADS),
        in_specs=[tok(0), tok(1024), tok(2048), prev(0), prev(1024), prev(2048),
                  nxt(0), nxt(1024), nxt(2048),
                  pl.BlockSpec((1, tt, 4 * A_LORA), lambda bi, i, q: (bi, i, OFF_LO // (4 * A_LORA))),
                  vec(3, 0), vec(3, 1024), vec(3, 2048),
                  vec(1), vec(1), vec(1), vec(2), vec(2),
                  pl.BlockSpec((2, 2 * A_LORA, QUAD), lambda bi, i, q: (0, 0, q)),
                  pl.BlockSpec((2, 2 * A_LORA, QUAD), lambda bi, i, q: (0, 0, q))],
        out_specs=[out_tok] * 9,
        out_shape=[out_sd] * 9,
        compiler_params=_cparams(("arbitrary", "arbitrary", "arbitrary")),
        name="rwkv_par",
    )(p, p, p, p, p, p, p, p, p, p, a_conv, a_conv, a_conv,
      k_k.reshape(1, D_MODEL), k_a.reshape(1, D_MODEL), r_k.reshape(1, D_MODEL), w0, a0, wup_pad, aup_pad)


def _rwkv_seq_kernel(rtf_ref, oif_ref, phf_ref, dlf_ref, rtb_ref, oib_ref, phb_ref, dlb_ref,
                     h0f_ref, h0b_ref, of_ref, ob_ref, hff_ref, hfb_ref, hf_s, hb_s, *, tt):
    i = pl.program_id(1)

    @pl.when(i == 0)
    def _():
        hf_s[...] = h0f_ref[0]
        hb_s[...] = h0b_ref[0]

    masks_b = _head_masks(A_CHUNK, BF16)
    nc = tt // A_CHUNK
    fwd_refs = (rtf_ref, phf_ref, dlf_ref, oif_ref, of_ref)
    bwd_refs = (rtb_ref, phb_ref, dlb_ref, oib_ref, ob_ref)

    n_pairs = D_MODEL // PAIR

    def pair(j):
        return slice(j * PAIR, (j + 1) * PAIR)

    hs = [hf_s[:, pair(j)] for j in range(n_pairs)] + [hb_s[:, pair(j)] for j in range(n_pairs)]
    for c in range(nc):
        chains = [(fwd_refs, c, j) for j in range(n_pairs)] + [(bwd_refs, nc - 1 - c, j) for j in range(n_pairs)]
        boths = []
        for h, (refs, cc, j) in zip(hs, chains):
            sl = slice(cc * A_CHUNK, (cc + 1) * A_CHUNK)
            lhs = jnp.concatenate([refs[0][0, sl, pair(j)], refs[1][0, sl, pair(j)]], axis=0)
            boths.append(_dot(lhs, _bd2(h.astype(BF16), masks_b)))
        new_hs = []
        for both, (refs, cc, j) in zip(boths, chains):
            sl = slice(cc * A_CHUNK, (cc + 1) * A_CHUNK)
            refs[4][0, sl, pair(j)] = (both[0:A_CHUNK] + refs[3][0, sl, pair(j)].astype(F32)).astype(refs[4].dtype)
            new_hs.append(both[A_CHUNK:] + refs[2][0, sl, pair(j)].astype(F32))
        hs = new_hs
    for j in range(n_pairs):
        hf_s[:, pair(j)] = hs[j]
        hb_s[:, pair(j)] = hs[n_pairs + j]
        hff_ref[0, :, pair(j)] = hs[j]
        hfb_ref[0, :, pair(j)] = hs[n_pairs + j]


def _rwkv_seq(par, h0f, h0b, tt):
    rtf, oif, phf, dlf, rtb, oib, phb, dlb = par
    b, l, _ = rtf.shape
    n_tiles = l // tt
    fwd = pl.BlockSpec((1, tt, D_MODEL), lambda bi, i: (bi, i, 0))
    bwd = pl.BlockSpec((1, tt, D_MODEL), lambda bi, i: (bi, n_tiles - 1 - i, 0))
    st = pl.BlockSpec((1, A_HEAD, D_MODEL), lambda bi, i: (bi, 0, 0))
    o_sd = jax.ShapeDtypeStruct((b, l, D_MODEL), BF16)
    h_sd = jax.ShapeDtypeStruct((b, A_HEAD, D_MODEL), F32)
    return pl.pallas_call(
        functools.partial(_rwkv_seq_kernel, tt=tt),
        grid=(b, n_tiles),
        in_specs=[fwd] * 4 + [bwd] * 4 + [st, st],
        out_specs=[fwd, bwd, st, st],
        out_shape=[o_sd, o_sd, h_sd, h_sd],
        scratch_shapes=[pltpu.VMEM((A_HEAD, D_MODEL), F32), pltpu.VMEM((A_HEAD, D_MODEL), F32)],
        compiler_params=_cparams(("arbitrary", "arbitrary")),
        name="rwkv_seq",
    )(rtf, oif, phf, dlf, rtb, oib, phb, dlb, h0f, h0b)


def _ret_kernel(lg_ref, q_ref, k_ref, v_ref, cos_ref, sin_ref, s0_ref, *rest,
                tt, reverse, with_intra, n_heads):
    if with_intra:
        prev_ref, o_ref, sf_ref, s_s = rest
    else:
        o_ref, sf_ref, s_s = rest
    bh = pl.program_id(0)
    i = pl.program_id(1)
    h = bh % n_heads

    @pl.when(i == 0)
    def _():
        s_s[...] = s0_ref[0]

    lg_f = lg_ref[0, h]
    lg_b = lg_ref[1, h]
    lg = lg_b if reverse else lg_f

    def rope(x_ref):
        x = x_ref[0].astype(F32)
        parts = []
        for j in range(R_QK // 128):
            xs = x[:, j * 128:(j + 1) * 128]
            parts.append(pltpu.roll(xs, 64, axis=1))
        xr = jnp.concatenate(parts, axis=1)
        return x * cos_ref[...] + xr * sin_ref[...]

    q = rope(q_ref)
    k = rope(k_ref) * (R_QK ** -0.5)

    idx = lax.broadcasted_iota(jnp.int32, (R_CHUNK, R_QK), 0).astype(F32)
    if reverse:
        q_dec = jnp.exp(lg * (R_CHUNK - idx))
        k_dec = jnp.exp(lg * idx)
    else:
        q_dec = jnp.exp(lg * (idx + 1.0))
        k_dec = jnp.exp(lg * (R_CHUNK - 1.0 - idx))
    c_dec = jnp.exp(lg * R_CHUNK)
    if with_intra:
        di = lax.broadcasted_iota(jnp.int32, (R_CHUNK, R_CHUNK), 0)
        dj = lax.broadcasted_iota(jnp.int32, (R_CHUNK, R_CHUNK), 1)
        diff = (di - dj).astype(F32)
        dmask = (jnp.where(diff >= 0, jnp.exp(lg_f * jnp.maximum(diff, 0.0)), 0.0)
                 + jnp.where(diff <= 0, jnp.exp(lg_b * jnp.maximum(-diff, 0.0)), 0.0))

    nc = tt // R_CHUNK
    s = s_s[...]
    for cc in range(nc):
        c = nc - 1 - cc if reverse else cc
        sl = slice(c * R_CHUNK, (c + 1) * R_CHUNK)
        qc, kc = q[sl], k[sl]
        vc = v_ref[0, sl, :]
        o = _dot((qc * q_dec).astype(BF16), s.astype(BF16))
        if with_intra:
            scores = _dot_nt(qc.astype(BF16), kc.astype(BF16)) * dmask
            o = o + _dot(scores.astype(BF16), vc) + prev_ref[0, sl, :].astype(F32)
        o_ref[0, sl, :] = o.astype(o_ref.dtype)
        s = s * c_dec + _dot_tn((kc * k_dec).astype(BF16), vc)
    s_s[...] = s
    sf_ref[0] = s


def _ret_pass(p, lg, cos_t, sin_t, s0, prev, tt, reverse):
    with_intra = prev is not None
    b, l, _ = p.shape
    n_tiles = l // tt
    nh = R_HEADS

    def tile(i):
        return n_tiles - 1 - i if reverse else i

    def col(width, off):
        return pl.BlockSpec((1, tt, width), lambda bh, i, lg_: (bh // nh, tile(i), off // width + bh % nh))

    tab = pl.BlockSpec((tt, R_QK), lambda bh, i, lg_: (tile(i), 0))
    st = pl.BlockSpec((1, R_QK, R_V), lambda bh, i, lg_: (bh, 0, 0))
    o_spec = pl.BlockSpec((1, tt, R_V), lambda bh, i, lg_: (bh // nh, tile(i), bh % nh))
    kern = functools.partial(_ret_kernel, tt=tt, reverse=reverse, with_intra=with_intra, n_heads=nh)
    grid_spec = pltpu.PrefetchScalarGridSpec(
        num_scalar_prefetch=1,
        grid=(b * nh, n_tiles),
        in_specs=[col(R_QK, OFF_Q), col(R_QK, OFF_K), col(R_V, OFF_V), tab, tab, st]
        + ([o_spec] if with_intra else []),
        out_specs=[o_spec, st],
        scratch_shapes=[pltpu.VMEM((R_QK, R_V), F32)],
    )
    return pl.pallas_call(
        kern,
        grid_spec=grid_spec,
        out_shape=[jax.ShapeDtypeStruct((b, l, nh * R_V), BF16),
                   jax.ShapeDtypeStruct((b * nh, R_QK, R_V), F32)],
        compiler_params=_cparams(("arbitrary", "arbitrary")),
        name="ret_bwd" if reverse else "ret_fwd",
    )(lg, p, p, p, cos_t, sin_t, s0, *([prev] if with_intra else []))


def _final_kernel(x_ref, ga_ref, gr_ref, ma_ref, mb_ref, of_ref, ob_ref, bon_ref, or_ref,
                  gate_ref, alw_ref, alb_ref, rlw_ref, rlb_ref, fnw_ref, awo_ref, rwo_ref, wo_ref, o_ref):
    li = lax.broadcasted_iota(jnp.int32, (QUAD, QUAD), 0) // A_HEAD
    lj = lax.broadcasted_iota(jnp.int32, (QUAD, QUAD), 1) // A_HEAD
    msum = (li == lj).astype(BF16)
    inv = 1.0 / A_HEAD

    def head_sum(t):
        return jnp.concatenate([_dot_exact_rhs(t[:, q * QUAD:(q + 1) * QUAD], msum) for q in range(N_QUADS)], axis=1)

    oa = of_ref[0].astype(F32) + ob_ref[0].astype(F32)
    mu = head_sum(oa) * inv
    dv = oa - mu
    var = head_sum(dv * dv) * inv
    ya = dv * lax.rsqrt(var + A_GN_EPS) * alw_ref[...] + alb_ref[...] + bon_ref[0].astype(F32)
    ya = _dot((ya * _silu(ga_ref[0].astype(F32))).astype(BF16), awo_ref[...])

    orr = or_ref[0].astype(F32)
    parts = []
    for h in range(R_HEADS):
        oh = orr[:, h * R_V:(h + 1) * R_V]
        m = jnp.mean(oh, axis=-1, keepdims=True)
        dh = oh - m
        vh = jnp.mean(dh * dh, axis=-1, keepdims=True)
        parts.append(dh * lax.rsqrt(vh + R_GN_EPS))
    yr = jnp.concatenate(parts, axis=1) * rlw_ref[...] + rlb_ref[...]
    yr = _dot((yr * _silu(gr_ref[0].astype(F32))).astype(BF16), rwo_ref[...])

    merged = _sigmoid(ma_ref[0].astype(F32)) * ya + _sigmoid(mb_ref[0].astype(F32)) * yr
    y = _dot(merged.astype(BF16), wo_ref[...])
    xo = x_ref[0] + gate_ref[0] * y
    o_ref[0] = xo * lax.rsqrt(jnp.mean(xo * xo, axis=-1, keepdims=True) + NORM_EPS) * fnw_ref[...]


def _final(x, p, of, ob, bonus, o_ret, gate, a_ln_w, a_ln_b, r_ln_w, r_ln_b, final_w, awo, rwo, wo, tm):
    b, l, d = x.shape

    def tok(width, off=0):
        return pl.BlockSpec((1, tm, width), lambda bi, i: (bi, i, off // width))

    def vec(width):
        return pl.BlockSpec((1, width), lambda bi, i: (0, 0))

    def mat(r, c):
        return pl.BlockSpec((r, c), lambda bi, i: (0, 0))

    rv = R_HEADS * R_V
    return pl.pallas_call(
        _final_kernel,
        grid=(b, l // tm),
        in_specs=[tok(d), tok(d, OFF_GA), tok(rv, OFF_GR), tok(d, OFF_MA), tok(d, OFF_MB),
                  tok(d), tok(d), tok(d), tok(rv),
                  pl.BlockSpec((1, 1, d), lambda bi, i: (bi, 0, 0)),
                  vec(d), vec(d), vec(rv), vec(rv), vec(d), mat(d, d), mat(rv, d), mat(d, d)],
        out_specs=tok(d),
        out_shape=jax.ShapeDtypeStruct((b, l, d), F32),
        compiler_params=_cparams(("arbitrary", "arbitrary")),
        name="final",
    )(x, p, p, p, p, of, ob, bonus, o_ret, gate,
      a_ln_w.reshape(1, d), a_ln_b.reshape(1, d), r_ln_w.reshape(1, rv), r_ln_b.reshape(1, rv),
      final_w.reshape(1, d), awo, rwo, wo)


def _rope_tables(l):
    t = jnp.arange(l)
    rows = (t // GRID_W).astype(F32)
    cols = (t % GRID_W).astype(F32)
    half = R_QK // 4
    freqs = ROPE_BASE ** (-jnp.arange(half, dtype=F32) / half)
    ar = rows[:, None] * freqs[None, :]
    ac = cols[:, None] * freqs[None, :]
    cos_t = jnp.concatenate([jnp.cos(ar), jnp.cos(ar), jnp.cos(ac), jnp.cos(ac)], axis=1)
    sin_t = jnp.concatenate([-jnp.sin(ar), jnp.sin(ar), -jnp.sin(ac), jnp.sin(ac)], axis=1)
    return cos_t, sin_t


def _reorder_w_in(w):
    return jnp.concatenate([w[:, 0:4096], w[:, 4352:], w[:, 4096:4352]], axis=1)


def _pick_tile(l, pref):
    t = min(l, pref)
    while l % t:
        t //= 2
    return t


def kernel(x, c, ctx, c_ctx, norm_w, ada_w, ada_b, w_in, a_conv, a_w_up, a_w0, a_a_up, a_a0, a_k_k, a_k_a,
           a_r_k, a_ln_w, a_ln_b, a_w_out, r_decay, r_ln_w, r_ln_b, r_w_out, w_out, final_norm_w):
    b, l, d = x.shape
    lc = ctx.shape[1]
    assert d == D_MODEL and l % R_CHUNK == 0 and lc % R_CHUNK == 0
    lyr = 0

    cc = jnp.zeros((16, d), F32).at[:b].set(c).at[b].set(c_ctx)
    mod = _adaln(cc, ada_w[lyr], ada_b[lyr])
    shift, scale, gate = mod[:b, :d], mod[:b, d:2 * d], mod[:b, 2 * d:]
    shift_c = jnp.broadcast_to(mod[b, :d], (b, d))
    scale_c = jnp.broadcast_to(mod[b, d:2 * d], (b, d))

    w_bf = _reorder_w_in(w_in[lyr]).astype(BF16)
    p_x = _inproj(x, (1.0 + scale)[:, None, :], shift[:, None, :], norm_w[lyr], w_bf, _pick_tile(l, 1024))
    p_c = _inproj(ctx, (1.0 + scale_c)[:, None, :], shift_c[:, None, :], norm_w[lyr], w_bf, _pick_tile(lc, 1024))

    zpad = jnp.zeros((A_LORA, d), F32)
    wup_pad = jnp.stack([jnp.concatenate([a_w_up[lyr, 0], zpad], 0), jnp.concatenate([zpad, a_w_up[lyr, 1]], 0)])
    aup_pad = jnp.stack([jnp.concatenate([a_a_up[lyr, 0], zpad], 0), jnp.concatenate([zpad, a_a_up[lyr, 1]], 0)])
    a_args = (a_conv[lyr], a_k_k[lyr], a_k_a[lyr], a_r_k[lyr].reshape(-1), a_w0[lyr], a_a0[lyr], wup_pad, aup_pad)
    h_zero = jnp.zeros((b, A_HEAD, d), F32)
    par_c = _rwkv_par(p_c, *a_args, tt=_pick_tile(lc, 256))
    _, _, hcf, hcb = _rwkv_seq(par_c[:8], h_zero, h_zero, _pick_tile(lc, 512))
    par_x = _rwkv_par(p_x, *a_args, tt=_pick_tile(l, 256))
    o_af, o_ab, _, _ = _rwkv_seq(par_x[:8], hcf, hcb, _pick_tile(l, 512))
    bonus = par_x[8]

    lg = -jnp.exp(r_decay[lyr].astype(F32))
    cos_x, sin_x = _rope_tables(l)
    cos_c = jnp.ones((lc, R_QK), F32)
    sin_c = jnp.zeros((lc, R_QK), F32)
    s_zero = jnp.zeros((b * R_HEADS, R_QK, R_V), F32)
    tc = _pick_tile(lc, 1024)
    _, sc_b = _ret_pass(p_c, lg, cos_c, sin_c, s_zero, None, tc, True)
    _, sc_f = _ret_pass(p_c, lg, cos_c, sin_c, s_zero, None, tc, False)
    tx = _pick_tile(l, 1024)
    o_rb, _ = _ret_pass(p_x, lg, cos_x, sin_x, sc_b, None, tx, True)
    o_r, _ = _ret_pass(p_x, lg, cos_x, sin_x, sc_f, o_rb, tx, False)

    return _final(x, p_x, o_af, o_ab, bonus, o_r, gate[:, None, :], a_ln_w[lyr], a_ln_b[lyr],
                  r_ln_w[lyr], r_ln_b[lyr], final_norm_w, a_w_out[lyr].astype(BF16),
                  r_w_out[lyr].astype(BF16), w_out[lyr].astype(BF16), _pick_tile(l, 256))
```

```python
import functools
import math

import jax
import jax.numpy as jnp
from jax import lax
from jax.experimental import pallas as pl
from jax.experimental.pallas import tpu as pltpu

F32 = jnp.float32
BF16 = jnp.bfloat16

D_MODEL = 1024
A_HEAD = 64
A_HEADS = D_MODEL // A_HEAD
A_LORA = 64
QUAD = 4 * A_HEAD
PAIR = 2 * A_HEAD
N_QUADS = D_MODEL // QUAD
A_CHUNK = 64
R_HEADS = 4
R_QK = 256
R_V = 512
R_BLOCK = 256
GRID_W = 64
ROPE_BASE = 10000.0
NORM_EPS = 1e-6
A_GN_EPS = 64e-5
R_GN_EPS = 1e-5
L2_EPS = 1e-12
DECAY_SCALE = math.exp(-0.5)

OFF_RKV, OFF_GA, OFF_Q, OFF_K, OFF_V, OFF_GR, OFF_MA, OFF_MB, OFF_LO = (
    0, 3072, 4096, 5120, 6144, 8192, 10240, 11264, 12288)
V7X_VMEM_LIMIT = 56 * 1024 * 1024


def _cparams(sem):
    return pltpu.CompilerParams(dimension_semantics=sem, vmem_limit_bytes=V7X_VMEM_LIMIT)


def _dot(a, b):
    return jnp.dot(a, b, preferred_element_type=F32)


def _dot_nt(a, b):
    return lax.dot_general(a, b, (((1,), (1,)), ((), ())), preferred_element_type=F32)


def _dot_tn(a, b):
    return lax.dot_general(a, b, (((0,), (0,)), ((), ())), preferred_element_type=F32)


def _split(x):
    hi = x.astype(BF16)
    lo = (x - hi.astype(F32)).astype(BF16)
    return hi, lo


def _dot_exact_lhs(m, x):
    hi, lo = _split(x)
    return _dot(m, hi) + _dot(m, lo)


def _dot3(a, b):
    ah, al = _split(a)
    bh, bl = _split(b)
    return _dot(ah, bh) + _dot(ah, bl) + _dot(al, bh)


def _sigmoid(x):
    return 0.5 * jnp.tanh(0.5 * x) + 0.5


def _silu(x):
    return x * _sigmoid(x)


def _adaln_kernel(c_ref, w_ref, b_ref, o_ref):
    cs = _silu(c_ref[...])
    o_ref[...] = _dot3(cs, w_ref[...]) + b_ref[...]


def _adaln(cc, ada_w, ada_b):
    n = ada_w.shape[1]
    tn = 512
    return pl.pallas_call(
        _adaln_kernel,
        grid=(n // tn,),
        in_specs=[pl.BlockSpec(cc.shape, lambda j: (0, 0)),
                  pl.BlockSpec((ada_w.shape[0], tn), lambda j: (0, j)),
                  pl.BlockSpec((1, tn), lambda j: (0, j))],
        out_specs=pl.BlockSpec((cc.shape[0], tn), lambda j: (0, j)),
        out_shape=jax.ShapeDtypeStruct((cc.shape[0], n), F32),
        compiler_params=_cparams(("arbitrary",)),
        name="adaln",
    )(cc, ada_w, ada_b.reshape(1, n))


def _inproj_kernel(x_ref, sc_ref, sh_ref, nw_ref, w_ref, o_ref, u_ref):
    @pl.when(pl.program_id(2) == 0)
    def _():
        xf = x_ref[0]
        y = xf * lax.rsqrt(jnp.mean(xf * xf, axis=-1, keepdims=True) + NORM_EPS) * nw_ref[...]
        u_ref[...] = (y * sc_ref[0] + sh_ref[0]).astype(BF16)

    o_ref[0] = _dot(u_ref[...], w_ref[...]).astype(o_ref.dtype)


def _inproj(x, scale1p, shift, norm_w, w_bf16, tm):
    b, l, d = x.shape
    n = w_bf16.shape[1]
    tn = 1792
    return pl.pallas_call(
        _inproj_kernel,
        grid=(b, l // tm, n // tn),
        in_specs=[pl.BlockSpec((1, tm, d), lambda bi, i, j: (bi, i, 0)),
                  pl.BlockSpec((1, 1, d), lambda bi, i, j: (bi, 0, 0)),
                  pl.BlockSpec((1, 1, d), lambda bi, i, j: (bi, 0, 0)),
                  pl.BlockSpec((1, d), lambda bi, i, j: (0, 0)),
                  pl.BlockSpec((d, tn), lambda bi, i, j: (0, j))],
        out_specs=pl.BlockSpec((1, tm, tn), lambda bi, i, j: (bi, i, j)),
        out_shape=jax.ShapeDtypeStruct((b, l, n), BF16),
        scratch_shapes=[pltpu.VMEM((tm, d), BF16)],
        compiler_params=_cparams(("arbitrary", "arbitrary", "arbitrary")),
        name="inproj",
    )(x, scale1p, shift, norm_w.reshape(1, d), w_bf16)


def _head_masks(rows, dtype):
    lane_head = lax.broadcasted_iota(jnp.int32, (rows, PAIR), 1) // A_HEAD
    return [(lane_head == h).astype(dtype) for h in range(2)]


def _bd2(x_bf16, masks):
    return jnp.concatenate([x_bf16 * m for m in masks], axis=0)


def _fold2(z, masks_f32):
    return z[0:A_HEAD] * masks_f32[0] + z[A_HEAD:2 * A_HEAD] * masks_f32[1]


def _rwkv_par_kernel(r_ref, k_ref, v_ref, rp_ref, kp_ref, vp_ref, rn_ref, kn_ref, vn_ref, lo_ref,
                     cr_ref, ck_ref, cv_ref, kkw_ref, kaw_ref, rkw_ref, w0_ref, a0_ref, wup_ref, aup_ref,
                     rtf_ref, oif_ref, phf_ref, dlf_ref, rtb_ref, oib_ref, phb_ref, dlb_ref, bon_ref,
                     *, tt, n_tiles):
    i = pl.program_id(1)
    row = lax.broadcasted_iota(jnp.int32, (tt, QUAD), 0)
    has_prev = (i > 0).astype(F32)
    has_next = (i < n_tiles - 1).astype(F32)

    def conv(x_ref, p_ref, n_ref, c_ref):
        x = x_ref[0].astype(F32)
        prev_row = p_ref[0][15:16, :].astype(F32) * has_prev
        next_row = n_ref[0][0:1, :].astype(F32) * has_next
        x_prev = jnp.where(row == 0, prev_row, pltpu.roll(x, 1, axis=0))
        x_next = jnp.where(row == tt - 1, next_row, pltpu.roll(x, tt - 1, axis=0))
        cw = c_ref[...]
        return x_prev * cw[0:1] + x * cw[1:2] + x_next * cw[2:3]

    r = conv(r_ref, rp_ref, rn_ref, cr_ref)
    k = conv(k_ref, kp_ref, kn_ref, ck_ref)
    v = conv(v_ref, vp_ref, vn_ref, cv_ref)

    li = lax.broadcasted_iota(jnp.int32, (QUAD, QUAD), 0) // A_HEAD
    lj = lax.broadcasted_iota(jnp.int32, (QUAD, QUAD), 1) // A_HEAD
    msum = (li == lj).astype(BF16)

    kk = k * kkw_ref[...]
    kk = kk * lax.rsqrt(_dot((kk * kk).astype(BF16), msum) + L2_EPS)

    ti = lax.broadcasted_iota(jnp.int32, (tt, tt), 0)
    tj = lax.broadcasted_iota(jnp.int32, (tt, tt), 1)
    same_chunk = (ti // A_CHUNK) == (tj // A_CHUNK)

    lo = lo_ref[0].astype(F32)
    lact = jnp.tanh(lo[:, 0:2 * A_LORA])
    la = lo[:, 2 * A_LORA:4 * A_LORA]

    masks_b = _head_masks(A_CHUNK, BF16)
    masks_f = _head_masks(A_CHUNK, F32)
    ct = lax.broadcasted_iota(jnp.int32, (A_CHUNK, PAIR), 0)
    cs = lax.broadcasted_iota(jnp.int32, (A_CHUNK, PAIR), 1) % A_HEAD
    eye_q = (ct == cs).astype(F32)
    same16 = ((ct // 16) == (cs // 16)).astype(F32)
    merge32 = (((ct // 32) == (cs // 32)) & ((ct // 16) != (cs // 16))).astype(F32)
    merge64 = ((ct // 32) != (cs // 32)).astype(F32)

    bonus = jnp.zeros((tt, QUAD), F32)
    outs = ((rtf_ref, oif_ref, phf_ref, dlf_ref), (rtb_ref, oib_ref, phb_ref, dlb_ref))
    prep = []
    for d in range(2):
        if d == 0:
            tri = (same_chunk & (tj <= ti)).astype(BF16)
            strict = (cs < ct).astype(F32)
            incl = (cs <= ct).astype(F32)
        else:
            tri = (same_chunk & (tj >= ti)).astype(BF16)
            strict = (cs > ct).astype(F32)
            incl = (cs >= ct).astype(F32)
        z = w0_ref[d:d + 1, :] + _dot3(lact, wup_ref[d])
        logw = -DECAY_SCALE * _sigmoid(z)
        av = _sigmoid(a0_ref[d:d + 1, :] + _dot3(la, aup_ref[d]))
        kd = k * (1.0 + (av - 1.0) * kaw_ref[...])
        bonus = bonus + _dot((r * kd * rkw_ref[...]).astype(BF16), msum) * v

        cum = _dot_exact_lhs(tri, logw)
        last = 0 if d == 1 else A_CHUNK - 1
        ctot = jnp.concatenate([jnp.broadcast_to(cum[c * A_CHUNK + last:c * A_CHUNK + last + 1, :], (A_CHUNK, QUAD))
                                for c in range(tt // A_CHUNK)], axis=0)
        e_inv = jnp.exp(-cum)
        e_g = jnp.exp(ctot - cum)
        kka = kk * av
        prep.append(dict(kh=kk * jnp.exp(cum - logw), rh=r * jnp.exp(cum), bt=kka * e_inv, kt=kd * e_inv,
                         bg=kka * e_g, kg=kd * e_g, gam=jnp.exp(ctot), strict=strict, incl=incl))

    units = [(d, c, pr) for d in range(2) for c in range(tt // A_CHUNK) for pr in range(QUAD // PAIR)]

    def cut(x, c, pr):
        return x[c * A_CHUNK:(c + 1) * A_CHUNK, pr * PAIR:(pr + 1) * PAIR]

    def bd(x):
        return _bd2(x.astype(BF16), masks_b)

    def pmm(a, w):
        return _dot(a.astype(BF16), bd(w))

    def stage_a(d, c, pr):
        pd = prep[d]
        kh, rh = cut(pd["kh"], c, pr), cut(pd["rh"], c, pr)
        lhs1 = jnp.concatenate([kh, rh], axis=0).astype(BF16)
        rhs1 = jnp.concatenate([bd(cut(pd["bt"], c, pr)), bd(cut(pd["kt"], c, pr))], axis=0)
        a_all = _dot_nt(lhs1, rhs1)
        return dict(kh=kh, rh=rh,
                    a_ab=a_all[0:A_CHUNK, 0:PAIR] * pd["strict"],
                    a_ak=a_all[0:A_CHUNK, PAIR:] * pd["strict"],
                    a_rb=a_all[A_CHUNK:, 0:PAIR] * pd["incl"],
                    a_rk=a_all[A_CHUNK:, PAIR:] * pd["incl"])

    st = [stage_a(*u_) for u_ in units]

    ps = [-(s_["a_ab"] * same16) for s_ in st]
    ts = [eye_q + p for p in ps]
    ps = [pmm(p, p) for p in ps]
    for _ in range(2):
        both = [pmm(jnp.concatenate([p, t], axis=0), p) for p, t in zip(ps, ts)]
        ps = [b_[0:A_CHUNK] for b_ in both]
        ts = [t + b_[A_CHUNK:] for t, b_ in zip(ts, both)]
    ts = [t + pmm(t, p) for t, p in zip(ts, ps)]
    for blk in (merge32, merge64):
        gs = [pmm(t, s_["a_ab"] * blk) for t, s_ in zip(ts, st)]
        ts = [t - pmm(g, t) for t, g in zip(ts, gs)]

    vs = [cut(v, c, pr) for d, c, pr in units]
    xas = [pmm(jnp.concatenate([s_["a_ak"], s_["a_rk"]], axis=0), vc) for s_, vc in zip(st, vs)]
    wus = [_dot(t.astype(BF16), jnp.concatenate([bd(s_["kh"]), bd(xa[0:A_CHUNK])], axis=1))
           for t, s_, xa in zip(ts, st, xas)]
    wubs = [wu.astype(BF16) for wu in wus]
    arbs = [_dot(s_["a_rb"].astype(BF16), jnp.concatenate([_bd2(wb[:, 0:PAIR], masks_b),
                                                          _bd2(wb[:, PAIR:], masks_b)], axis=1))
            for s_, wb in zip(st, wubs)]
    zzs = [_dot_tn(cut(prep[d]["bg"], c, pr).astype(BF16), wb) for (d, c, pr), wb in zip(units, wubs)]
    z2s = [_dot_tn(cut(prep[d]["kg"], c, pr).astype(BF16), vc.astype(BF16)) for (d, c, pr), vc in zip(units, vs)]

    for (d, c, pr), s_, xa, arb, zz, z2 in zip(units, st, xas, arbs, zzs, z2s):
        rt_ref, oi_ref, ph_ref, dl_ref = outs[d]
        gam = cut(prep[d]["gam"], c, pr)
        rw = slice(c * A_CHUNK, (c + 1) * A_CHUNK)
        ln = slice(pr * PAIR, (pr + 1) * PAIR)
        rt_ref[0, rw, ln] = (s_["rh"] - arb[:, 0:PAIR]).astype(rt_ref.dtype)
        oi_ref[0, rw, ln] = (xa[A_CHUNK:] - arb[:, PAIR:]).astype(oi_ref.dtype)
        ph_ref[0, rw, ln] = (eye_q * gam - _fold2(zz[:, 0:PAIR], masks_f)).astype(ph_ref.dtype)
        dl_ref[0, rw, ln] = (_fold2(z2, masks_f) - _fold2(zz[:, PAIR:], masks_f)).astype(dl_ref.dtype)
    bon_ref[0] = bonus.astype(bon_ref.dtype)


def _rwkv_par(p, a_conv, k_k, k_a, r_k, w0, a0, wup_pad, aup_pad, tt):
    b, l, _ = p.shape
    n_tiles = l // tt
    h16 = tt // 16
    last16 = l // 16 - 1

    def tok(col0):
        return pl.BlockSpec((1, tt, QUAD), lambda bi, i, q: (bi, i, col0 // QUAD + q))

    def prev(col0):
        return pl.BlockSpec((1, 16, QUAD), lambda bi, i, q: (bi, jnp.maximum(i * h16 - 1, 0), col0 // QUAD + q))

    def nxt(col0):
        return pl.BlockSpec((1, 16, QUAD),
                            lambda bi, i, q: (bi, jnp.minimum((i + 1) * h16, last16), col0 // QUAD + q))

    def vec(rows, col0=0):
        return pl.BlockSpec((rows, QUAD), lambda bi, i, q: (0, col0 // QUAD + q))

    out_tok = pl.BlockSpec((1, tt, QUAD), lambda bi, i, q: (bi, i, q))
    out_sd = jax.ShapeDtypeStruct((b, l, D_MODEL), BF16)
    kern = functools.partial(_rwkv_par_kernel, tt=tt, n_tiles=n_tiles)
    return pl.pallas_call(
        kern,
        grid=(b, n_tiles, N_QUADS),
        in_specs=[tok(0), tok(1024), tok(2048), prev(0), prev(1024), prev(2048),
                  nxt(0), nxt(1024), nxt(2048),
                  pl.BlockSpec((1, tt, 4 * A_LORA), lambda bi, i, q: (bi, i, OFF_LO // (4 * A_LORA))),
                  vec(3, 0), vec(3, 1024), vec(3, 2048),
                  vec(1), vec(1), vec(1), vec(2), vec(2),
                  pl.BlockSpec((2, 2 * A_LORA, QUAD), lambda bi, i, q: (0, 0, q)),
                  pl.BlockSpec((2, 2 * A_LORA, QUAD), lambda bi, i, q: (0, 0, q))],
        out_specs=[out_tok] * 9,
        out_shape=[out_sd] * 9,
        compiler_params=_cparams(("arbitrary", "arbitrary", "arbitrary")),
        name="rwkv_par",
    )(p, p, p, p, p, p, p, p, p, p, a_conv, a_conv, a_conv,
      k_k.reshape(1, D_MODEL), k_a.reshape(1, D_MODEL), r_k.reshape(1, D_MODEL), w0, a0, wup_pad, aup_pad)


def _rwkv_seq_kernel(rtf_ref, oif_ref, phf_ref, dlf_ref, rtb_ref, oib_ref, phb_ref, dlb_ref,
                     h0f_ref, h0b_ref, of_ref, ob_ref, hff_ref, hfb_ref, hf_s, hb_s, *, tt):
    i = pl.program_id(1)

    @pl.when(i == 0)
    def _():
        hf_s[...] = h0f_ref[0]
        hb_s[...] = h0b_ref[0]

    masks_b = _head_masks(A_CHUNK, BF16)
    nc = tt // A_CHUNK
    fwd_refs = (rtf_ref, phf_ref, dlf_ref, oif_ref, of_ref)
    bwd_refs = (rtb_ref, phb_ref, dlb_ref, oib_ref, ob_ref)

    n_pairs = D_MODEL // PAIR

    def pair(j):
        return slice(j * PAIR, (j + 1) * PAIR)

    hs = [hf_s[:, pair(j)] for j in range(n_pairs)] + [hb_s[:, pair(j)] for j in range(n_pairs)]
    for c in range(nc):
        chains = [(fwd_refs, c, j) for j in range(n_pairs)] + [(bwd_refs, nc - 1 - c, j) for j in range(n_pairs)]
        boths = []
        for h, (refs, cc, j) in zip(hs, chains):
            sl = slice(cc * A_CHUNK, (cc + 1) * A_CHUNK)
            lhs = jnp.concatenate([refs[0][0, sl, pair(j)], refs[1][0, sl, pair(j)]], axis=0)
            boths.append(_dot(lhs, _bd2(h.astype(BF16), masks_b)))
        new_hs = []
        for both, (refs, cc, j) in zip(boths, chains):
            sl = slice(cc * A_CHUNK, (cc + 1) * A_CHUNK)
            refs[4][0, sl, pair(j)] = (both[0:A_CHUNK] + refs[3][0, sl, pair(j)].astype(F32)).astype(refs[4].dtype)
            new_hs.append(both[A_CHUNK:] + refs[2][0, sl, pair(j)].astype(F32))
        hs = new_hs
    for j in range(n_pairs):
        hf_s[:, pair(j)] = hs[j]
        hb_s[:, pair(j)] = hs[n_pairs + j]
        hff_ref[0, :, pair(j)] = hs[j]
        hfb_ref[0, :, pair(j)] = hs[n_pairs + j]


def _rwkv_seq(par, h0f, h0b, tt):
    rtf, oif, phf, dlf, rtb, oib, phb, dlb = par
    b, l, _ = rtf.shape
    n_tiles = l // tt
    fwd = pl.BlockSpec((1, tt, D_MODEL), lambda bi, i: (bi, i, 0))
    bwd = pl.BlockSpec((1, tt, D_MODEL), lambda bi, i: (bi, n_tiles - 1 - i, 0))
    st = pl.BlockSpec((1, A_HEAD, D_MODEL), lambda bi, i: (bi, 0, 0))
    o_sd = jax.ShapeDtypeStruct((b, l, D_MODEL), BF16)
    h_sd = jax.ShapeDtypeStruct((b, A_HEAD, D_MODEL), F32)
    return pl.pallas_call(
        functools.partial(_rwkv_seq_kernel, tt=tt),
        grid=(b, n_tiles),
        in_specs=[fwd] * 4 + [bwd] * 4 + [st, st],
        out_specs=[fwd, bwd, st, st],
        out_shape=[o_sd, o_sd, h_sd, h_sd],
        scratch_shapes=[pltpu.VMEM((A_HEAD, D_MODEL), F32), pltpu.VMEM((A_HEAD, D_MODEL), F32)],
        compiler_params=_cparams(("arbitrary", "arbitrary")),
        name="rwkv_seq",
    )(rtf, oif, phf, dlf, rtb, oib, phb, dlb, h0f, h0b)


def _ret_kernel(lg_ref, q_ref, k_ref, v_ref, rc_ref, rs_ref, cc_ref, cs_ref, s0_ref, *rest,
                tt, reverse, with_intra, n_heads, n_tiles):
    if with_intra:
        prev_ref, o_ref, sf_ref, s_s = rest
    else:
        o_ref, sf_ref, s_s = rest
    bh = pl.program_id(0)
    i = pl.program_id(1)
    h = bh % n_heads

    @pl.when(i == 0)
    def _():
        s_s[...] = s0_ref[0]

    lg_f = lg_ref[0, h]
    lg_b = lg_ref[1, h]
    lg = lg_b if reverse else lg_f

    n_rows = tt // GRID_W
    r0 = (n_tiles - 1 - i if reverse else i) * n_rows

    def by_row(ref):
        return jnp.concatenate([jnp.broadcast_to(ref[pl.ds(r0 + j, 1), :], (GRID_W, 128)) for j in range(n_rows)], axis=0)

    def by_col(ref):
        return jnp.concatenate([ref[...]] * n_rows, axis=0)

    cos_r, sin_r, cos_c, sin_c = by_row(rc_ref), by_row(rs_ref), by_col(cc_ref), by_col(cs_ref)

    def rope(x_ref):
        x = x_ref[0].astype(F32)
        xa, xb = x[:, 0:128], x[:, 128:256]
        return jnp.concatenate([xa * cos_r + pltpu.roll(xa, 64, axis=1) * sin_r,
                                xb * cos_c + pltpu.roll(xb, 64, axis=1) * sin_c], axis=1)

    q = rope(q_ref)
    k = rope(k_ref) * (R_QK ** -0.5)

    idx = lax.broadcasted_iota(jnp.int32, (R_BLOCK, R_QK), 0).astype(F32)
    if reverse:
        q_dec = jnp.exp(lg * (R_BLOCK - idx))
        k_dec = jnp.exp(lg * idx)
    else:
        q_dec = jnp.exp(lg * (idx + 1.0))
        k_dec = jnp.exp(lg * (R_BLOCK - 1.0 - idx))
    c_dec = jnp.exp(lg * R_BLOCK)

    nc = tt // R_BLOCK
    order = [nc - 1 - cc if reverse else cc for cc in range(nc)]

    def blk(c):
        return slice(c * R_BLOCK, (c + 1) * R_BLOCK)

    if with_intra:
        di = lax.broadcasted_iota(jnp.int32, (R_BLOCK, R_BLOCK), 0)
        dj = lax.broadcasted_iota(jnp.int32, (R_BLOCK, R_BLOCK), 1)
        diff = (di - dj).astype(F32)
        dmask = (jnp.where(diff >= 0, jnp.exp(lg_f * jnp.maximum(diff, 0.0)), 0.0)
                 + jnp.where(diff <= 0, jnp.exp(lg_b * jnp.maximum(-diff, 0.0)), 0.0))
        scores = {c: _dot_nt(q[blk(c)].astype(BF16), k[blk(c)].astype(BF16)) for c in order}
    kvs = {c: _dot_tn((k[blk(c)] * k_dec).astype(BF16), v_ref[0, blk(c), :]) for c in order}

    s = s_s[...]
    inter = {}
    for c in order:
        inter[c] = _dot((q[blk(c)] * q_dec).astype(BF16), s.astype(BF16))
        s = s * c_dec + kvs[c]
    s_s[...] = s
    sf_ref[0] = s
    for c in order:
        o = inter[c]
        if with_intra:
            o = o + _dot((scores[c] * dmask).astype(BF16), v_ref[0, blk(c), :]) + prev_ref[0, blk(c), :].astype(F32)
        o_ref[0, blk(c), :] = o.astype(o_ref.dtype)


def _ret_pass(p, lg, tabs, s0, prev, tt, reverse):
    with_intra = prev is not None
    b, l, _ = p.shape
    n_tiles = l // tt
    nh = R_HEADS

    def tile(i):
        return n_tiles - 1 - i if reverse else i

    def col(width, off):
        return pl.BlockSpec((1, tt, width), lambda bh, i, lg_: (bh // nh, tile(i), off // width + bh % nh))

    def whole(a):
        return pl.BlockSpec(a.shape, lambda bh, i, lg_: (0, 0))

    st = pl.BlockSpec((1, R_QK, R_V), lambda bh, i, lg_: (bh, 0, 0))
    o_spec = pl.BlockSpec((1, tt, R_V), lambda bh, i, lg_: (bh // nh, tile(i), bh % nh))
    kern = functools.partial(_ret_kernel, tt=tt, reverse=reverse, with_intra=with_intra, n_heads=nh,
                             n_tiles=n_tiles)
    grid_spec = pltpu.PrefetchScalarGridSpec(
        num_scalar_prefetch=1,
        grid=(b * nh, n_tiles),
        in_specs=[col(R_QK, OFF_Q), col(R_QK, OFF_K), col(R_V, OFF_V)] + [whole(t) for t in tabs] + [st]
        + ([o_spec] if with_intra else []),
        out_specs=[o_spec, st],
        scratch_shapes=[pltpu.VMEM((R_QK, R_V), F32)],
    )
    return pl.pallas_call(
        kern,
        grid_spec=grid_spec,
        out_shape=[jax.ShapeDtypeStruct((b, l, nh * R_V), BF16),
                   jax.ShapeDtypeStruct((b * nh, R_QK, R_V), F32)],
        compiler_params=_cparams(("arbitrary", "arbitrary")),
        name="ret_bwd" if reverse else "ret_fwd",
    )(lg, p, p, p, *tabs, s0, *([prev] if with_intra else []))


def _final_kernel(x_ref, ga_ref, gr_ref, ma_ref, mb_ref, of_ref, ob_ref, bon_ref, or_ref,
                  gate_ref, alw_ref, alb_ref, rlw_ref, rlb_ref, fnw_ref, awo_ref, rwo_ref, wo_ref, o_ref):
    li = lax.broadcasted_iota(jnp.int32, (QUAD, QUAD), 0) // A_HEAD
    lj = lax.broadcasted_iota(jnp.int32, (QUAD, QUAD), 1) // A_HEAD
    msum = (li == lj).astype(BF16)
    inv = 1.0 / A_HEAD

    def head_sum(t):
        return jnp.concatenate([_dot(t[:, q * QUAD:(q + 1) * QUAD].astype(BF16), msum) for q in range(N_QUADS)], axis=1)

    oa = of_ref[0].astype(F32) + ob_ref[0].astype(F32)
    mu = head_sum(oa) * inv
    dv = oa - mu
    var = head_sum(dv * dv) * inv
    ya = dv * lax.rsqrt(var + A_GN_EPS) * alw_ref[...] + alb_ref[...] + bon_ref[0].astype(F32)
    ya = _dot((ya * _silu(ga_ref[0].astype(F32))).astype(BF16), awo_ref[...])

    orr = or_ref[0].astype(F32)
    parts = []
    for h in range(R_HEADS):
        oh = orr[:, h * R_V:(h + 1) * R_V]
        m = jnp.mean(oh, axis=-1, keepdims=True)
        dh = oh - m
        vh = jnp.mean(dh * dh, axis=-1, keepdims=True)
        parts.append(dh * lax.rsqrt(vh + R_GN_EPS))
    yr = jnp.concatenate(parts, axis=1) * rlw_ref[...] + rlb_ref[...]
    yr = _dot((yr * _silu(gr_ref[0].astype(F32))).astype(BF16), rwo_ref[...])

    merged = _sigmoid(ma_ref[0].astype(F32)) * ya + _sigmoid(mb_ref[0].astype(F32)) * yr
    y = _dot(merged.astype(BF16), wo_ref[...])
    xo = x_ref[0] + gate_ref[0] * y
    o_ref[0] = xo * lax.rsqrt(jnp.mean(xo * xo, axis=-1, keepdims=True) + NORM_EPS) * fnw_ref[...]


def _final(x, p, of, ob, bonus, o_ret, gate, a_ln_w, a_ln_b, r_ln_w, r_ln_b, final_w, awo, rwo, wo, tm):
    b, l, d = x.shape

    def tok(width, off=0):
        return pl.BlockSpec((1, tm, width), lambda bi, i: (bi, i, off // width))

    def vec(width):
        return pl.BlockSpec((1, width), lambda bi, i: (0, 0))

    def mat(r, c):
        return pl.BlockSpec((r, c), lambda bi, i: (0, 0))

    rv = R_HEADS * R_V
    return pl.pallas_call(
        _final_kernel,
        grid=(b, l // tm),
        in_specs=[tok(d), tok(d, OFF_GA), tok(rv, OFF_GR), tok(d, OFF_MA), tok(d, OFF_MB),
                  tok(d), tok(d), tok(d), tok(rv),
                  pl.BlockSpec((1, 1, d), lambda bi, i: (bi, 0, 0)),
                  vec(d), vec(d), vec(rv), vec(rv), vec(d), mat(d, d), mat(rv, d), mat(d, d)],
        out_specs=tok(d),
        out_shape=jax.ShapeDtypeStruct((b, l, d), F32),
        compiler_params=_cparams(("arbitrary", "arbitrary")),
        name="final",
    )(x, p, p, p, p, of, ob, bonus, o_ret, gate,
      a_ln_w.reshape(1, d), a_ln_b.reshape(1, d), r_ln_w.reshape(1, rv), r_ln_b.reshape(1, rv),
      final_w.reshape(1, d), awo, rwo, wo)


def _rope_tables(l, rotate):
    n_rows = l // GRID_W
    if not rotate:
        return (jnp.ones((n_rows, 128), F32), jnp.zeros((n_rows, 128), F32),
                jnp.ones((GRID_W, 128), F32), jnp.zeros((GRID_W, 128), F32))
    half = R_QK // 4
    freqs = ROPE_BASE ** (-jnp.arange(half, dtype=F32) / half)

    def tab(n):
        ang = jnp.arange(n, dtype=F32)[:, None] * freqs[None, :]
        return (jnp.concatenate([jnp.cos(ang), jnp.cos(ang)], axis=1),
                jnp.concatenate([-jnp.sin(ang), jnp.sin(ang)], axis=1))

    return tab(n_rows) + tab(GRID_W)


def _reorder_w_in(w):
    return jnp.concatenate([w[:, 0:4096], w[:, 4352:], w[:, 4096:4352]], axis=1)


def _pick_tile(l, pref):
    t = min(l, pref)
    while l % t:
        t //= 2
    return t


def kernel(x, c, ctx, c_ctx, norm_w, ada_w, ada_b, w_in, a_conv, a_w_up, a_w0, a_a_up, a_a0, a_k_k, a_k_a,
           a_r_k, a_ln_w, a_ln_b, a_w_out, r_decay, r_ln_w, r_ln_b, r_w_out, w_out, final_norm_w):
    b, l, d = x.shape
    lc = ctx.shape[1]
    assert d == D_MODEL and l % R_BLOCK == 0 and lc % R_BLOCK == 0
    lyr = 0

    cc = jnp.zeros((16, d), F32).at[:b].set(c).at[b].set(c_ctx)
    mod = _adaln(cc, ada_w[lyr], ada_b[lyr])
    shift, scale, gate = mod[:b, :d], mod[:b, d:2 * d], mod[:b, 2 * d:]
    shift_c = jnp.broadcast_to(mod[b, :d], (b, d))
    scale_c = jnp.broadcast_to(mod[b, d:2 * d], (b, d))

    w_bf = _reorder_w_in(w_in[lyr]).astype(BF16)
    p_x = _inproj(x, (1.0 + scale)[:, None, :], shift[:, None, :], norm_w[lyr], w_bf, _pick_tile(l, 1024))
    p_c = _inproj(ctx, (1.0 + scale_c)[:, None, :], shift_c[:, None, :], norm_w[lyr], w_bf, _pick_tile(lc, 1024))

    zpad = jnp.zeros((A_LORA, d), F32)
    wup_pad = jnp.stack([jnp.concatenate([a_w_up[lyr, 0], zpad], 0), jnp.concatenate([zpad, a_w_up[lyr, 1]], 0)])
    aup_pad = jnp.stack([jnp.concatenate([a_a_up[lyr, 0], zpad], 0), jnp.concatenate([zpad, a_a_up[lyr, 1]], 0)])
    a_args = (a_conv[lyr], a_k_k[lyr], a_k_a[lyr], a_r_k[lyr].reshape(-1), a_w0[lyr], a_a0[lyr], wup_pad, aup_pad)
    h_zero = jnp.zeros((b, A_HEAD, d), F32)
    par_c = _rwkv_par(p_c, *a_args, tt=_pick_tile(lc, 256))
    _, _, hcf, hcb = _rwkv_seq(par_c[:8], h_zero, h_zero, _pick_tile(lc, 512))
    par_x = _rwkv_par(p_x, *a_args, tt=_pick_tile(l, 256))
    o_af, o_ab, _, _ = _rwkv_seq(par_x[:8], hcf, hcb, _pick_tile(l, 512))
    bonus = par_x[8]

    lg = -jnp.exp(r_decay[lyr].astype(F32))
    tabs_x = _rope_tables(l, True)
    tabs_c = _rope_tables(lc, False)
    s_zero = jnp.zeros((b * R_HEADS, R_QK, R_V), F32)
    tc = _pick_tile(lc, 1024)
    _, sc_b = _ret_pass(p_c, lg, tabs_c, s_zero, None, tc, True)
    _, sc_f = _ret_pass(p_c, lg, tabs_c, s_zero, None, tc, False)
    tx = _pick_tile(l, 1024)
    o_rb, _ = _ret_pass(p_x, lg, tabs_x, sc_b, None, tx, True)
    o_r, _ = _ret_pass(p_x, lg, tabs_x, sc_f, o_rb, tx, False)

    return _final(x, p_x, o_af, o_ab, bonus, o_r, gate[:, None, :], a_ln_w[lyr], a_ln_b[lyr],
                  r_ln_w[lyr], r_ln_b[lyr], final_norm_w, a_w_out[lyr].astype(BF16),
                  r_w_out[lyr].astype(BF16), w_out[lyr].astype(BF16), _pick_tile(l, 256))
```

```python
import functools
import math

import jax
import jax.numpy as jnp
from jax import lax
from jax.experimental import pallas as pl
from jax.experimental.pallas import tpu as pltpu

F32 = jnp.float32
BF16 = jnp.bfloat16

D_MODEL = 1024
A_HEAD = 64
A_HEADS = D_MODEL // A_HEAD
A_LORA = 64
QUAD = 4 * A_HEAD
PAIR = 2 * A_HEAD
N_QUADS = D_MODEL // QUAD
A_CHUNK = 64
SUB = 8
CUM_BLOCK = 256
R_HEADS = 4
R_QK = 256
R_V = 512
R_BLOCK = 256
GRID_W = 64
ROPE_BASE = 10000.0
NORM_EPS = 1e-6
A_GN_EPS = 64e-5
R_GN_EPS = 1e-5
L2_EPS = 1e-12
DECAY_SCALE = math.exp(-0.5)

OFF_RKV, OFF_GA, OFF_Q, OFF_K, OFF_V, OFF_GR, OFF_MA, OFF_MB, OFF_LO = (
    0, 3072, 4096, 5120, 6144, 8192, 10240, 11264, 12288)
V7X_VMEM_LIMIT = 56 * 1024 * 1024


def _cparams(sem):
    return pltpu.CompilerParams(dimension_semantics=sem, vmem_limit_bytes=V7X_VMEM_LIMIT)


def _dot(a, b):
    return jnp.dot(a, b, preferred_element_type=F32)


def _dot_nt(a, b):
    return lax.dot_general(a, b, (((1,), (1,)), ((), ())), preferred_element_type=F32)


def _dot_tn(a, b):
    return lax.dot_general(a, b, (((0,), (0,)), ((), ())), preferred_element_type=F32)


def _split(x):
    hi = x.astype(BF16)
    lo = (x - hi.astype(F32)).astype(BF16)
    return hi, lo


def _dot_exact_lhs(m, x):
    hi, lo = _split(x)
    return _dot(m, hi) + _dot(m, lo)


def _dot3(a, b):
    ah, al = _split(a)
    bh, bl = _split(b)
    return _dot(ah, bh) + _dot(ah, bl) + _dot(al, bh)


def _sigmoid(x):
    return 0.5 * jnp.tanh(0.5 * x) + 0.5


def _silu(x):
    return x * _sigmoid(x)


def _adaln_kernel(c_ref, w_ref, b_ref, o_ref):
    cs = _silu(c_ref[...])
    o_ref[...] = _dot3(cs, w_ref[...]) + b_ref[...]


def _adaln(cc, ada_w, ada_b):
    n = ada_w.shape[1]
    tn = 512
    return pl.pallas_call(
        _adaln_kernel,
        grid=(n // tn,),
        in_specs=[pl.BlockSpec(cc.shape, lambda j: (0, 0)),
                  pl.BlockSpec((ada_w.shape[0], tn), lambda j: (0, j)),
                  pl.BlockSpec((1, tn), lambda j: (0, j))],
        out_specs=pl.BlockSpec((cc.shape[0], tn), lambda j: (0, j)),
        out_shape=jax.ShapeDtypeStruct((cc.shape[0], n), F32),
        compiler_params=_cparams(("arbitrary",)),
        name="adaln",
    )(cc, ada_w, ada_b.reshape(1, n))


def _inproj_kernel(x_ref, sc_ref, sh_ref, nw_ref, w_ref, o_ref, u_ref):
    @pl.when(pl.program_id(2) == 0)
    def _():
        xf = x_ref[0]
        y = xf * lax.rsqrt(jnp.mean(xf * xf, axis=-1, keepdims=True) + NORM_EPS) * nw_ref[...]
        u_ref[...] = (y * sc_ref[0] + sh_ref[0]).astype(BF16)

    o_ref[0] = _dot(u_ref[...], w_ref[...]).astype(o_ref.dtype)


def _inproj(x, scale1p, shift, norm_w, w_bf16, tm):
    b, l, d = x.shape
    n = w_bf16.shape[1]
    tn = 1792
    return pl.pallas_call(
        _inproj_kernel,
        grid=(b, l // tm, n // tn),
        in_specs=[pl.BlockSpec((1, tm, d), lambda bi, i, j: (bi, i, 0)),
                  pl.BlockSpec((1, 1, d), lambda bi, i, j: (bi, 0, 0)),
                  pl.BlockSpec((1, 1, d), lambda bi, i, j: (bi, 0, 0)),
                  pl.BlockSpec((1, d), lambda bi, i, j: (0, 0)),
                  pl.BlockSpec((d, tn), lambda bi, i, j: (0, j))],
        out_specs=pl.BlockSpec((1, tm, tn), lambda bi, i, j: (bi, i, j)),
        out_shape=jax.ShapeDtypeStruct((b, l, n), BF16),
        scratch_shapes=[pltpu.VMEM((tm, d), BF16)],
        compiler_params=_cparams(("arbitrary", "arbitrary", "arbitrary")),
        name="inproj",
    )(x, scale1p, shift, norm_w.reshape(1, d), w_bf16)


def _head_masks(rows, dtype):
    lane_head = lax.broadcasted_iota(jnp.int32, (rows, PAIR), 1) // A_HEAD
    return [(lane_head == h).astype(dtype) for h in range(2)]


def _bd2(x_bf16, masks):
    return jnp.concatenate([x_bf16 * m for m in masks], axis=0)


def _fold2(z, masks_f32):
    return z[0:A_HEAD] * masks_f32[0] + z[A_HEAD:2 * A_HEAD] * masks_f32[1]


def _rwkv_par_kernel(r_ref, k_ref, v_ref, rp_ref, kp_ref, vp_ref, rn_ref, kn_ref, vn_ref, lo_ref,
                     cr_ref, ck_ref, cv_ref, kkw_ref, kaw_ref, rkw_ref, w0_ref, a0_ref, wup_ref, aup_ref,
                     rtf_ref, oif_ref, phf_ref, dlf_ref, rtb_ref, oib_ref, phb_ref, dlb_ref, bon_ref,
                     *, tt, n_tiles):
    i = pl.program_id(1)
    row = lax.broadcasted_iota(jnp.int32, (tt, QUAD), 0)
    has_prev = (i > 0).astype(F32)
    has_next = (i < n_tiles - 1).astype(F32)

    def conv(x_ref, p_ref, n_ref, c_ref):
        x = x_ref[0].astype(F32)
        prev_row = p_ref[0][15:16, :].astype(F32) * has_prev
        next_row = n_ref[0][0:1, :].astype(F32) * has_next
        x_prev = jnp.where(row == 0, prev_row, pltpu.roll(x, 1, axis=0))
        x_next = jnp.where(row == tt - 1, next_row, pltpu.roll(x, tt - 1, axis=0))
        cw = c_ref[...]
        return x_prev * cw[0:1] + x * cw[1:2] + x_next * cw[2:3]

    r = conv(r_ref, rp_ref, rn_ref, cr_ref)
    k = conv(k_ref, kp_ref, kn_ref, ck_ref)
    v = conv(v_ref, vp_ref, vn_ref, cv_ref)

    li = lax.broadcasted_iota(jnp.int32, (QUAD, QUAD), 0) // A_HEAD
    lj = lax.broadcasted_iota(jnp.int32, (QUAD, QUAD), 1) // A_HEAD
    msum = (li == lj).astype(BF16)

    kk = k * kkw_ref[...]
    kk = kk * lax.rsqrt(_dot((kk * kk).astype(BF16), msum) + L2_EPS)

    cb = min(tt, CUM_BLOCK)
    ti = lax.broadcasted_iota(jnp.int32, (cb, cb), 0)
    tj = lax.broadcasted_iota(jnp.int32, (cb, cb), 1)
    same_chunk = (ti // A_CHUNK) == (tj // A_CHUNK)

    lo = lo_ref[0].astype(F32)
    lact = jnp.tanh(lo[:, 0:2 * A_LORA])
    la = lo[:, 2 * A_LORA:4 * A_LORA]

    masks_b = _head_masks(A_CHUNK, BF16)
    masks_f = _head_masks(A_CHUNK, F32)
    ct = lax.broadcasted_iota(jnp.int32, (A_CHUNK, PAIR), 0)
    cs = lax.broadcasted_iota(jnp.int32, (A_CHUNK, PAIR), 1) % A_HEAD
    eye_q = (ct == cs).astype(F32)
    lane_c = lax.broadcasted_iota(jnp.int32, (SUB, PAIR), 1)
    blk_of_lane = (lane_c % A_HEAD) // SUB
    blk_f = [(blk_of_lane == b_).astype(F32) for b_ in range(A_CHUNK // SUB)]
    head_blk_b = [((lane_c // A_HEAD == h_) & (blk_of_lane == b_)).astype(BF16)
                  for h_ in range(2) for b_ in range(A_CHUNK // SUB)]
    eye_c = (lax.broadcasted_iota(jnp.int32, (SUB, PAIR), 0) == lane_c % SUB).astype(F32)
    merges = [(((ct // (2 * w)) == (cs // (2 * w))) & ((ct // w) != (cs // w))).astype(F32)
              for w in (SUB, 2 * SUB, 4 * SUB)]

    bonus = jnp.zeros((tt, QUAD), F32)
    outs = ((rtf_ref, oif_ref, phf_ref, dlf_ref), (rtb_ref, oib_ref, phb_ref, dlb_ref))
    prep = []
    for d in range(2):
        if d == 0:
            tri = (same_chunk & (tj <= ti)).astype(BF16)
            strict = (cs < ct).astype(F32)
            incl = (cs <= ct).astype(F32)
        else:
            tri = (same_chunk & (tj >= ti)).astype(BF16)
            strict = (cs > ct).astype(F32)
            incl = (cs >= ct).astype(F32)
        z = w0_ref[d:d + 1, :] + _dot3(lact, wup_ref[d])
        logw = -DECAY_SCALE * _sigmoid(z)
        av = _sigmoid(a0_ref[d:d + 1, :] + _dot3(la, aup_ref[d]))
        kd = k * (1.0 + (av - 1.0) * kaw_ref[...])
        bonus = bonus + _dot((r * kd * rkw_ref[...]).astype(BF16), msum) * v

        cum = jnp.concatenate([_dot_exact_lhs(tri, logw[j * cb:(j + 1) * cb]) for j in range(tt // cb)], axis=0)
        last = 0 if d == 1 else A_CHUNK - 1
        ctot = jnp.concatenate([jnp.broadcast_to(cum[c * A_CHUNK + last:c * A_CHUNK + last + 1, :], (A_CHUNK, QUAD))
                                for c in range(tt // A_CHUNK)], axis=0)
        e_inv = jnp.exp(-cum)
        e_g = jnp.exp(ctot - cum)
        kka = kk * av
        prep.append(dict(kh=kk * jnp.exp(cum - logw), rh=r * jnp.exp(cum), bt=kka * e_inv, kt=kd * e_inv,
                         bg=kka * e_g, kg=kd * e_g, gam=jnp.exp(ctot), strict=strict, incl=incl))

    units = [(d, c, pr) for d in range(2) for c in range(tt // A_CHUNK) for pr in range(QUAD // PAIR)]

    def cut(x, c, pr):
        return x[c * A_CHUNK:(c + 1) * A_CHUNK, pr * PAIR:(pr + 1) * PAIR]

    def bd(x):
        return _bd2(x.astype(BF16), masks_b)

    def pmm(a, w):
        return _dot(a.astype(BF16), bd(w))

    def stage_a(d, c, pr):
        pd = prep[d]
        kh, rh = cut(pd["kh"], c, pr), cut(pd["rh"], c, pr)
        lhs1 = jnp.concatenate([kh, rh], axis=0).astype(BF16)
        rhs1 = jnp.concatenate([bd(cut(pd["bt"], c, pr)), bd(cut(pd["kt"], c, pr))], axis=0)
        a_all = _dot_nt(lhs1, rhs1)
        return dict(kh=kh, rh=rh,
                    a_ab=a_all[0:A_CHUNK, 0:PAIR] * pd["strict"],
                    a_ak=a_all[0:A_CHUNK, PAIR:] * pd["strict"],
                    a_rb=a_all[A_CHUNK:, 0:PAIR] * pd["incl"],
                    a_rk=a_all[A_CHUNK:, PAIR:] * pd["incl"])

    st = [stage_a(*u_) for u_ in units]

    def compress(x):
        acc = x[0:SUB] * blk_f[0]
        for b_ in range(1, A_CHUNK // SUB):
            acc = acc + x[b_ * SUB:(b_ + 1) * SUB] * blk_f[b_]
        return acc

    def expand_w(xc):
        xb = xc.astype(BF16)
        return jnp.concatenate([xb * m for m in head_blk_b], axis=0)

    ps = [-compress(s_["a_ab"]) for s_ in st]
    ts = [eye_c + p for p in ps]
    ps = [_dot(p.astype(BF16), expand_w(p)) for p in ps]
    both = [_dot(jnp.concatenate([p, t], axis=0).astype(BF16), expand_w(p)) for p, t in zip(ps, ts)]
    ps = [b_[0:SUB] for b_ in both]
    ts = [t + b_[SUB:] for t, b_ in zip(ts, both)]
    ts = [t + _dot(t.astype(BF16), expand_w(p)) for t, p in zip(ts, ps)]
    ts = [jnp.concatenate([t * m for m in blk_f], axis=0) for t in ts]
    for blk in merges:
        gs = [pmm(t, s_["a_ab"] * blk) for t, s_ in zip(ts, st)]
        ts = [t - pmm(g, t) for t, g in zip(ts, gs)]

    vs = [cut(v, c, pr) for d, c, pr in units]
    xas = [pmm(jnp.concatenate([s_["a_ak"], s_["a_rk"]], axis=0), vc) for s_, vc in zip(st, vs)]
    wus = [_dot(t.astype(BF16), jnp.concatenate([bd(s_["kh"]), bd(xa[0:A_CHUNK])], axis=1))
           for t, s_, xa in zip(ts, st, xas)]
    wubs = [wu.astype(BF16) for wu in wus]
    arbs = [_dot(s_["a_rb"].astype(BF16), jnp.concatenate([_bd2(wb[:, 0:PAIR], masks_b),
                                                          _bd2(wb[:, PAIR:], masks_b)], axis=1))
            for s_, wb in zip(st, wubs)]
    zzs = [_dot_tn(cut(prep[d]["bg"], c, pr).astype(BF16), wb) for (d, c, pr), wb in zip(units, wubs)]
    z2s = [_dot_tn(cut(prep[d]["kg"], c, pr).astype(BF16), vc.astype(BF16)) for (d, c, pr), vc in zip(units, vs)]

    for (d, c, pr), s_, xa, arb, zz, z2 in zip(units, st, xas, arbs, zzs, z2s):
        rt_ref, oi_ref, ph_ref, dl_ref = outs[d]
        gam = cut(prep[d]["gam"], c, pr)
        rw = slice(c * A_CHUNK, (c + 1) * A_CHUNK)
        ln = slice(pr * PAIR, (pr + 1) * PAIR)
        rt_ref[0, rw, ln] = (s_["rh"] - arb[:, 0:PAIR]).astype(rt_ref.dtype)
        oi_ref[0, rw, ln] = (xa[A_CHUNK:] - arb[:, PAIR:]).astype(oi_ref.dtype)
        ph_ref[0, rw, ln] = (eye_q * gam - _fold2(zz[:, 0:PAIR], masks_f)).astype(ph_ref.dtype)
        dl_ref[0, rw, ln] = (_fold2(z2, masks_f) - _fold2(zz[:, PAIR:], masks_f)).astype(dl_ref.dtype)
    bon_ref[0] = bonus.astype(bon_ref.dtype)


def _rwkv_par(p, a_conv, k_k, k_a, r_k, w0, a0, wup_pad, aup_pad, tt):
    b, l, _ = p.shape
    n_tiles = l // tt
    h16 = tt // 16
    last16 = l // 16 - 1

    def tok(col0):
        return pl.BlockSpec((1, tt, QUAD), lambda bi, i, q: (bi, i, col0 // QUAD + q))

    def prev(col0):
        return pl.BlockSpec((1, 16, QUAD), lambda bi, i, q: (bi, jnp.maximum(i * h16 - 1, 0), col0 // QUAD + q))

    def nxt(col0):
        return pl.BlockSpec((1, 16, QUAD),
                            lambda bi, i, q: (bi, jnp.minimum((i + 1) * h16, last16), col0 // QUAD + q))

    def vec(rows, col0=0):
        return pl.BlockSpec((rows, QUAD), lambda bi, i, q: (0, col0 // QUAD + q))

    out_tok = pl.BlockSpec((1, tt, QUAD), lambda bi, i, q: (bi, i, q))
    out_sd = jax.ShapeDtypeStruct((b, l, D_MODEL), BF16)
    kern = functools.partial(_rwkv_par_kernel, tt=tt, n_tiles=n_tiles)
    return pl.pallas_call(
        kern,
        grid=(b, n_tiles, N_QUADS),
        in_specs=[tok(0), tok(1024), tok(2048), prev(0), prev(1024), prev(2048),
                  nxt(0), nxt(1024), nxt(2048),
                  pl.BlockSpec((1, tt, 4 * A_LORA), lambda bi, i, q: (bi, i, OFF_LO // (4 * A_LORA))),
                  vec(3, 0), vec(3, 1024), vec(3, 2048),
                  vec(1), vec(1), vec(1), vec(2), vec(2),
                  pl.BlockSpec((2, 2 * A_LORA, QUAD), lambda bi, i, q: (0, 0, q)),
                  pl.BlockSpec((2, 2 * A_LORA, QUAD), lambda bi, i, q: (0, 0, q))],
        out_specs=[out_tok] * 9,
        out_shape=[out_sd] * 9,
        compiler_params=_cparams(("arbitrary", "arbitrary", "arbitrary")),
        name="rwkv_par",
    )(p, p, p, p, p, p, p, p, p, p, a_conv, a_conv, a_conv,
      k_k.reshape(1, D_MODEL), k_a.reshape(1, D_MODEL), r_k.reshape(1, D_MODEL), w0, a0, wup_pad, aup_pad)


def _rwkv_seq_kernel(rtf_ref, oif_ref, phf_ref, dlf_ref, rtb_ref, oib_ref, phb_ref, dlb_ref,
                     h0f_ref, h0b_ref, of_ref, ob_ref, hff_ref, hfb_ref, hf_s, hb_s, *, tt):
    i = pl.program_id(1)

    @pl.when(i == 0)
    def _():
        hf_s[...] = h0f_ref[0]
        hb_s[...] = h0b_ref[0]

    masks_b = _head_masks(A_CHUNK, BF16)
    nc = tt // A_CHUNK
    fwd_refs = (rtf_ref, phf_ref, dlf_ref, oif_ref, of_ref)
    bwd_refs = (rtb_ref, phb_ref, dlb_ref, oib_ref, ob_ref)

    n_pairs = D_MODEL // PAIR

    def pair(j):
        return slice(j * PAIR, (j + 1) * PAIR)

    hs = [hf_s[:, pair(j)] for j in range(n_pairs)] + [hb_s[:, pair(j)] for j in range(n_pairs)]
    for c in range(nc):
        chains = [(fwd_refs, c, j) for j in range(n_pairs)] + [(bwd_refs, nc - 1 - c, j) for j in range(n_pairs)]
        boths = []
        for h, (refs, cc, j) in zip(hs, chains):
            sl = slice(cc * A_CHUNK, (cc + 1) * A_CHUNK)
            lhs = jnp.concatenate([refs[0][0, sl, pair(j)], refs[1][0, sl, pair(j)]], axis=0)
            boths.append(_dot(lhs, _bd2(h.astype(BF16), masks_b)))
        new_hs = []
        for both, (refs, cc, j) in zip(boths, chains):
            sl = slice(cc * A_CHUNK, (cc + 1) * A_CHUNK)
            refs[4][0, sl, pair(j)] = (both[0:A_CHUNK] + refs[3][0, sl, pair(j)].astype(F32)).astype(refs[4].dtype)
            new_hs.append(both[A_CHUNK:] + refs[2][0, sl, pair(j)].astype(F32))
        hs = new_hs
    for j in range(n_pairs):
        hf_s[:, pair(j)] = hs[j]
        hb_s[:, pair(j)] = hs[n_pairs + j]
        hff_ref[0, :, pair(j)] = hs[j]
        hfb_ref[0, :, pair(j)] = hs[n_pairs + j]


def _rwkv_seq(par, h0f, h0b, tt):
    rtf, oif, phf, dlf, rtb, oib, phb, dlb = par
    b, l, _ = rtf.shape
    n_tiles = l // tt
    fwd = pl.BlockSpec((1, tt, D_MODEL), lambda bi, i: (bi, i, 0))
    bwd = pl.BlockSpec((1, tt, D_MODEL), lambda bi, i: (bi, n_tiles - 1 - i, 0))
    st = pl.BlockSpec((1, A_HEAD, D_MODEL), lambda bi, i: (bi, 0, 0))
    o_sd = jax.ShapeDtypeStruct((b, l, D_MODEL), BF16)
    h_sd = jax.ShapeDtypeStruct((b, A_HEAD, D_MODEL), F32)
    return pl.pallas_call(
        functools.partial(_rwkv_seq_kernel, tt=tt),
        grid=(b, n_tiles),
        in_specs=[fwd] * 4 + [bwd] * 4 + [st, st],
        out_specs=[fwd, bwd, st, st],
        out_shape=[o_sd, o_sd, h_sd, h_sd],
        scratch_shapes=[pltpu.VMEM((A_HEAD, D_MODEL), F32), pltpu.VMEM((A_HEAD, D_MODEL), F32)],
        compiler_params=_cparams(("arbitrary", "arbitrary")),
        name="rwkv_seq",
    )(rtf, oif, phf, dlf, rtb, oib, phb, dlb, h0f, h0b)


def _ret_kernel(lg_ref, q_ref, k_ref, v_ref, rc_ref, rs_ref, cc_ref, cs_ref, s0_ref, *rest,
                tt, reverse, with_intra, n_heads, n_tiles):
    if with_intra:
        prev_ref, o_ref, sf_ref, s_s = rest
    else:
        o_ref, sf_ref, s_s = rest
    bh = pl.program_id(0)
    i = pl.program_id(1)
    h = bh % n_heads

    @pl.when(i == 0)
    def _():
        s_s[...] = s0_ref[0]

    lg_f = lg_ref[0, h]
    lg_b = lg_ref[1, h]
    lg = lg_b if reverse else lg_f

    n_rows = tt // GRID_W
    r0 = (n_tiles - 1 - i if reverse else i) * n_rows

    def by_row(ref):
        return jnp.concatenate([jnp.broadcast_to(ref[pl.ds(r0 + j, 1), :], (GRID_W, 128)) for j in range(n_rows)], axis=0)

    def by_col(ref):
        return jnp.concatenate([ref[...]] * n_rows, axis=0)

    cos_r, sin_r, cos_c, sin_c = by_row(rc_ref), by_row(rs_ref), by_col(cc_ref), by_col(cs_ref)

    def rope(x_ref):
        x = x_ref[0].astype(F32)
        xa, xb = x[:, 0:128], x[:, 128:256]
        return jnp.concatenate([xa * cos_r + pltpu.roll(xa, 64, axis=1) * sin_r,
                                xb * cos_c + pltpu.roll(xb, 64, axis=1) * sin_c], axis=1)

    q = rope(q_ref)
    k = rope(k_ref) * (R_QK ** -0.5)

    idx = lax.broadcasted_iota(jnp.int32, (R_BLOCK, R_QK), 0).astype(F32)
    if reverse:
        q_dec = jnp.exp(lg * (R_BLOCK - idx))
        k_dec = jnp.exp(lg * idx)
    else:
        q_dec = jnp.exp(lg * (idx + 1.0))
        k_dec = jnp.exp(lg * (R_BLOCK - 1.0 - idx))
    c_dec = jnp.exp(lg * R_BLOCK)

    nc = tt // R_BLOCK
    order = [nc - 1 - cc if reverse else cc for cc in range(nc)]

    def blk(c):
        return slice(c * R_BLOCK, (c + 1) * R_BLOCK)

    if with_intra:
        di = lax.broadcasted_iota(jnp.int32, (R_BLOCK, R_BLOCK), 0)
        dj = lax.broadcasted_iota(jnp.int32, (R_BLOCK, R_BLOCK), 1)
        diff = (di - dj).astype(F32)
        dmask = (jnp.where(diff >= 0, jnp.exp(lg_f * jnp.maximum(diff, 0.0)), 0.0)
                 + jnp.where(diff <= 0, jnp.exp(lg_b * jnp.maximum(-diff, 0.0)), 0.0))
        scores = {c: _dot_nt(q[blk(c)].astype(BF16), k[blk(c)].astype(BF16)) for c in order}
    kvs = {c: _dot_tn((k[blk(c)] * k_dec).astype(BF16), v_ref[0, blk(c), :]) for c in order}

    s = s_s[...]
    inter = {}
    for c in order:
        inter[c] = _dot((q[blk(c)] * q_dec).astype(BF16), s.astype(BF16))
        s = s * c_dec + kvs[c]
    s_s[...] = s
    sf_ref[0] = s
    for c in order:
        o = inter[c]
        if with_intra:
            o = o + _dot((scores[c] * dmask).astype(BF16), v_ref[0, blk(c), :]) + prev_ref[0, blk(c), :].astype(F32)
        o_ref[0, blk(c), :] = o.astype(o_ref.dtype)


def _ret_pass(p, lg, tabs, s0, prev, tt, reverse):
    with_intra = prev is not None
    b, l, _ = p.shape
    n_tiles = l // tt
    nh = R_HEADS

    def tile(i):
        return n_tiles - 1 - i if reverse else i

    def col(width, off):
        return pl.BlockSpec((1, tt, width), lambda bh, i, lg_: (bh // nh, tile(i), off // width + bh % nh))

    def whole(a):
        return pl.BlockSpec(a.shape, lambda bh, i, lg_: (0, 0))

    st = pl.BlockSpec((1, R_QK, R_V), lambda bh, i, lg_: (bh, 0, 0))
    o_spec = pl.BlockSpec((1, tt, R_V), lambda bh, i, lg_: (bh // nh, tile(i), bh % nh))
    kern = functools.partial(_ret_kernel, tt=tt, reverse=reverse, with_intra=with_intra, n_heads=nh,
                             n_tiles=n_tiles)
    grid_spec = pltpu.PrefetchScalarGridSpec(
        num_scalar_prefetch=1,
        grid=(b * nh, n_tiles),
        in_specs=[col(R_QK, OFF_Q), col(R_QK, OFF_K), col(R_V, OFF_V)] + [whole(t) for t in tabs] + [st]
        + ([o_spec] if with_intra else []),
        out_specs=[o_spec, st],
        scratch_shapes=[pltpu.VMEM((R_QK, R_V), F32)],
    )
    return pl.pallas_call(
        kern,
        grid_spec=grid_spec,
        out_shape=[jax.ShapeDtypeStruct((b, l, nh * R_V), BF16),
                   jax.ShapeDtypeStruct((b * nh, R_QK, R_V), F32)],
        compiler_params=_cparams(("arbitrary", "arbitrary")),
        name="ret_bwd" if reverse else "ret_fwd",
    )(lg, p, p, p, *tabs, s0, *([prev] if with_intra else []))


def _final_kernel(x_ref, ga_ref, gr_ref, ma_ref, mb_ref, of_ref, ob_ref, bon_ref, or_ref,
                  gate_ref, alw_ref, alb_ref, rlw_ref, rlb_ref, fnw_ref, awo_ref, rwo_ref, wo_ref, o_ref):
    li = lax.broadcasted_iota(jnp.int32, (QUAD, QUAD), 0) // A_HEAD
    lj = lax.broadcasted_iota(jnp.int32, (QUAD, QUAD), 1) // A_HEAD
    msum = (li == lj).astype(BF16)
    inv = 1.0 / A_HEAD

    def head_sum(t):
        return jnp.concatenate([_dot(t[:, q * QUAD:(q + 1) * QUAD].astype(BF16), msum) for q in range(N_QUADS)], axis=1)

    oa = of_ref[0].astype(F32) + ob_ref[0].astype(F32)
    mu = head_sum(oa) * inv
    dv = oa - mu
    var = head_sum(dv * dv) * inv
    ya = dv * lax.rsqrt(var + A_GN_EPS) * alw_ref[...] + alb_ref[...] + bon_ref[0].astype(F32)
    ya = _dot((ya * _silu(ga_ref[0].astype(F32))).astype(BF16), awo_ref[...])

    orr = or_ref[0].astype(F32)
    parts = []
    for h in range(R_HEADS):
        oh = orr[:, h * R_V:(h + 1) * R_V]
        m = jnp.mean(oh, axis=-1, keepdims=True)
        dh = oh - m
        vh = jnp.mean(dh * dh, axis=-1, keepdims=True)
        parts.append(dh * lax.rsqrt(vh + R_GN_EPS))
    yr = jnp.concatenate(parts, axis=1) * rlw_ref[...] + rlb_ref[...]
    yr = _dot((yr * _silu(gr_ref[0].astype(F32))).astype(BF16), rwo_ref[...])

    merged = _sigmoid(ma_ref[0].astype(F32)) * ya + _sigmoid(mb_ref[0].astype(F32)) * yr
    y = _dot(merged.astype(BF16), wo_ref[...])
    xo = x_ref[0] + gate_ref[0] * y
    o_ref[0] = xo * lax.rsqrt(jnp.mean(xo * xo, axis=-1, keepdims=True) + NORM_EPS) * fnw_ref[...]


def _final(x, p, of, ob, bonus, o_ret, gate, a_ln_w, a_ln_b, r_ln_w, r_ln_b, final_w, awo, rwo, wo, tm):
    b, l, d = x.shape

    def tok(width, off=0):
        return pl.BlockSpec((1, tm, width), lambda bi, i: (bi, i, off // width))

    def vec(width):
        return pl.BlockSpec((1, width), lambda bi, i: (0, 0))

    def mat(r, c):
        return pl.BlockSpec((r, c), lambda bi, i: (0, 0))

    rv = R_HEADS * R_V
    return pl.pallas_call(
        _final_kernel,
        grid=(b, l // tm),
        in_specs=[tok(d), tok(d, OFF_GA), tok(rv, OFF_GR), tok(d, OFF_MA), tok(d, OFF_MB),
                  tok(d), tok(d), tok(d), tok(rv),
                  pl.BlockSpec((1, 1, d), lambda bi, i: (bi, 0, 0)),
                  vec(d), vec(d), vec(rv), vec(rv), vec(d), mat(d, d), mat(rv, d), mat(d, d)],
        out_specs=tok(d),
        out_shape=jax.ShapeDtypeStruct((b, l, d), F32),
        compiler_params=_cparams(("arbitrary", "arbitrary")),
        name="final",
    )(x, p, p, p, p, of, ob, bonus, o_ret, gate,
      a_ln_w.reshape(1, d), a_ln_b.reshape(1, d), r_ln_w.reshape(1, rv), r_ln_b.reshape(1, rv),
      final_w.reshape(1, d), awo, rwo, wo)


def _rope_tables(l, rotate):
    n_rows = l // GRID_W
    if not rotate:
        return (jnp.ones((n_rows, 128), F32), jnp.zeros((n_rows, 128), F32),
                jnp.ones((GRID_W, 128), F32), jnp.zeros((GRID_W, 128), F32))
    half = R_QK // 4
    freqs = ROPE_BASE ** (-jnp.arange(half, dtype=F32) / half)

    def tab(n):
        ang = jnp.arange(n, dtype=F32)[:, None] * freqs[None, :]
        return (jnp.concatenate([jnp.cos(ang), jnp.cos(ang)], axis=1),
                jnp.concatenate([-jnp.sin(ang), jnp.sin(ang)], axis=1))

    return tab(n_rows) + tab(GRID_W)


def _reorder_w_in(w):
    return jnp.concatenate([w[:, 0:4096], w[:, 4352:], w[:, 4096:4352]], axis=1)


def _pick_tile(l, pref):
    t = min(l, pref)
    while l % t:
        t //= 2
    return t


def kernel(x, c, ctx, c_ctx, norm_w, ada_w, ada_b, w_in, a_conv, a_w_up, a_w0, a_a_up, a_a0, a_k_k, a_k_a,
           a_r_k, a_ln_w, a_ln_b, a_w_out, r_decay, r_ln_w, r_ln_b, r_w_out, w_out, final_norm_w):
    b, l, d = x.shape
    lc = ctx.shape[1]
    assert d == D_MODEL and l % R_BLOCK == 0 and lc % R_BLOCK == 0
    lyr = 0

    cc = jnp.zeros((16, d), F32).at[:b].set(c).at[b].set(c_ctx)
    mod = _adaln(cc, ada_w[lyr], ada_b[lyr])
    shift, scale, gate = mod[:b, :d], mod[:b, d:2 * d], mod[:b, 2 * d:]
    shift_c = jnp.broadcast_to(mod[b, :d], (b, d))
    scale_c = jnp.broadcast_to(mod[b, d:2 * d], (b, d))

    w_bf = _reorder_w_in(w_in[lyr]).astype(BF16)
    p_x = _inproj(x, (1.0 + scale)[:, None, :], shift[:, None, :], norm_w[lyr], w_bf, _pick_tile(l, 2048))
    p_c = _inproj(ctx, (1.0 + scale_c)[:, None, :], shift_c[:, None, :], norm_w[lyr], w_bf, _pick_tile(lc, 1024))

    zpad = jnp.zeros((A_LORA, d), F32)
    wup_pad = jnp.stack([jnp.concatenate([a_w_up[lyr, 0], zpad], 0), jnp.concatenate([zpad, a_w_up[lyr, 1]], 0)])
    aup_pad = jnp.stack([jnp.concatenate([a_a_up[lyr, 0], zpad], 0), jnp.concatenate([zpad, a_a_up[lyr, 1]], 0)])
    a_args = (a_conv[lyr], a_k_k[lyr], a_k_a[lyr], a_r_k[lyr].reshape(-1), a_w0[lyr], a_a0[lyr], wup_pad, aup_pad)
    h_zero = jnp.zeros((b, A_HEAD, d), F32)
    par_c = _rwkv_par(p_c, *a_args, tt=_pick_tile(lc, 256))
    _, _, hcf, hcb = _rwkv_seq(par_c[:8], h_zero, h_zero, _pick_tile(lc, 512))
    par_x = _rwkv_par(p_x, *a_args, tt=_pick_tile(l, 1024))
    o_af, o_ab, _, _ = _rwkv_seq(par_x[:8], hcf, hcb, _pick_tile(l, 512))
    bonus = par_x[8]

    lg = -jnp.exp(r_decay[lyr].astype(F32))
    tabs_x = _rope_tables(l, True)
    tabs_c = _rope_tables(lc, False)
    s_zero = jnp.zeros((b * R_HEADS, R_QK, R_V), F32)
    tc = _pick_tile(lc, 1024)
    _, sc_b = _ret_pass(p_c, lg, tabs_c, s_zero, None, tc, True)
    _, sc_f = _ret_pass(p_c, lg, tabs_c, s_zero, None, tc, False)
    tx = _pick_tile(l, 2048)
    o_rb, _ = _ret_pass(p_x, lg, tabs_x, sc_b, None, tx, True)
    o_r, _ = _ret_pass(p_x, lg, tabs_x, sc_f, o_rb, tx, False)

    return _final(x, p_x, o_af, o_ab, bonus, o_r, gate[:, None, :], a_ln_w[lyr], a_ln_b[lyr],
                  r_ln_w[lyr], r_ln_b[lyr], final_norm_w, a_w_out[lyr].astype(BF16),
                  r_w_out[lyr].astype(BF16), w_out[lyr].astype(BF16), _pick_tile(l, 512))
```

```python
import functools
import math

import jax
import jax.numpy as jnp
from jax import lax
from jax.experimental import pallas as pl
from jax.experimental.pallas import tpu as pltpu

F32 = jnp.float32
BF16 = jnp.bfloat16

D_MODEL = 1024
A_HEAD = 64
A_HEADS = D_MODEL // A_HEAD
A_LORA = 64
QUAD = 4 * A_HEAD
PAIR = 2 * A_HEAD
N_QUADS = D_MODEL // QUAD
A_CHUNK = 64
SUB = 8
CUM_BLOCK = 256
R_HEADS = 4
R_QK = 256
R_V = 512
R_BLOCK = 256
GRID_W = 64
ROPE_BASE = 10000.0
NORM_EPS = 1e-6
A_GN_EPS = 64e-5
R_GN_EPS = 1e-5
L2_EPS = 1e-12
DECAY_SCALE = math.exp(-0.5)

OFF_RKV, OFF_GA, OFF_Q, OFF_K, OFF_V, OFF_GR, OFF_MA, OFF_MB, OFF_LO = (
    0, 3072, 4096, 5120, 6144, 8192, 10240, 11264, 12288)
V7X_VMEM_LIMIT = 56 * 1024 * 1024


def _cparams(sem):
    return pltpu.CompilerParams(dimension_semantics=sem, vmem_limit_bytes=V7X_VMEM_LIMIT)


def _dot(a, b):
    return jnp.dot(a, b, preferred_element_type=F32)


def _dot_nt(a, b):
    return lax.dot_general(a, b, (((1,), (1,)), ((), ())), preferred_element_type=F32)


def _dot_tn(a, b):
    return lax.dot_general(a, b, (((0,), (0,)), ((), ())), preferred_element_type=F32)


def _split(x):
    hi = x.astype(BF16)
    lo = (x - hi.astype(F32)).astype(BF16)
    return hi, lo


def _dot_exact_lhs(m, x):
    hi, lo = _split(x)
    return _dot(m, hi) + _dot(m, lo)


def _dot3(a, b):
    ah, al = _split(a)
    bh, bl = _split(b)
    return _dot(ah, bh) + _dot(ah, bl) + _dot(al, bh)


def _sigmoid(x):
    return 0.5 * jnp.tanh(0.5 * x) + 0.5


def _silu(x):
    return x * _sigmoid(x)


def _adaln_kernel(c_ref, w_ref, b_ref, o_ref):
    cs = _silu(c_ref[...])
    o_ref[...] = _dot3(cs, w_ref[...]) + b_ref[...]


def _adaln(cc, ada_w, ada_b):
    n = ada_w.shape[1]
    tn = 512
    return pl.pallas_call(
        _adaln_kernel,
        grid=(n // tn,),
        in_specs=[pl.BlockSpec(cc.shape, lambda j: (0, 0)),
                  pl.BlockSpec((ada_w.shape[0], tn), lambda j: (0, j)),
                  pl.BlockSpec((1, tn), lambda j: (0, j))],
        out_specs=pl.BlockSpec((cc.shape[0], tn), lambda j: (0, j)),
        out_shape=jax.ShapeDtypeStruct((cc.shape[0], n), F32),
        compiler_params=_cparams(("arbitrary",)),
        name="adaln",
    )(cc, ada_w, ada_b.reshape(1, n))


def _inproj_kernel(x_ref, sc_ref, sh_ref, nw_ref, w_ref, o_ref, u_ref):
    @pl.when(pl.program_id(2) == 0)
    def _():
        xf = x_ref[0]
        y = xf * lax.rsqrt(jnp.mean(xf * xf, axis=-1, keepdims=True) + NORM_EPS) * nw_ref[...]
        u_ref[...] = (y * sc_ref[0] + sh_ref[0]).astype(BF16)

    o_ref[0] = _dot(u_ref[...], w_ref[...]).astype(o_ref.dtype)


def _inproj(x, scale1p, shift, norm_w, w_bf16, tm):
    b, l, d = x.shape
    n = w_bf16.shape[1]
    tn = 1792
    return pl.pallas_call(
        _inproj_kernel,
        grid=(b, l // tm, n // tn),
        in_specs=[pl.BlockSpec((1, tm, d), lambda bi, i, j: (bi, i, 0)),
                  pl.BlockSpec((1, 1, d), lambda bi, i, j: (bi, 0, 0)),
                  pl.BlockSpec((1, 1, d), lambda bi, i, j: (bi, 0, 0)),
                  pl.BlockSpec((1, d), lambda bi, i, j: (0, 0)),
                  pl.BlockSpec((d, tn), lambda bi, i, j: (0, j))],
        out_specs=pl.BlockSpec((1, tm, tn), lambda bi, i, j: (bi, i, j)),
        out_shape=jax.ShapeDtypeStruct((b, l, n), BF16),
        scratch_shapes=[pltpu.VMEM((tm, d), BF16)],
        compiler_params=_cparams(("arbitrary", "arbitrary", "arbitrary")),
        name="inproj",
    )(x, scale1p, shift, norm_w.reshape(1, d), w_bf16)


def _head_masks(rows, dtype):
    lane_head = lax.broadcasted_iota(jnp.int32, (rows, PAIR), 1) // A_HEAD
    return [(lane_head == h).astype(dtype) for h in range(2)]


def _bd2(x_bf16, masks):
    return jnp.concatenate([x_bf16 * m for m in masks], axis=0)


def _fold2(z, masks_f32):
    return z[0:A_HEAD] * masks_f32[0] + z[A_HEAD:2 * A_HEAD] * masks_f32[1]


def _rwkv_par_kernel(r_ref, k_ref, v_ref, rp_ref, kp_ref, vp_ref, rn_ref, kn_ref, vn_ref, lo_ref,
                     cr_ref, ck_ref, cv_ref, kkw_ref, kaw_ref, rkw_ref, w0_ref, a0_ref, wup_ref, aup_ref,
                     rtf_ref, oif_ref, phf_ref, dlf_ref, rtb_ref, oib_ref, phb_ref, dlb_ref, bon_ref,
                     *, tt, n_tiles):
    i = pl.program_id(1)
    row = lax.broadcasted_iota(jnp.int32, (tt, QUAD), 0)
    has_prev = (i > 0).astype(F32)
    has_next = (i < n_tiles - 1).astype(F32)

    def conv(x_ref, p_ref, n_ref, c_ref):
        x = x_ref[0].astype(F32)
        prev_row = p_ref[0][15:16, :].astype(F32) * has_prev
        next_row = n_ref[0][0:1, :].astype(F32) * has_next
        x_prev = jnp.where(row == 0, prev_row, pltpu.roll(x, 1, axis=0))
        x_next = jnp.where(row == tt - 1, next_row, pltpu.roll(x, tt - 1, axis=0))
        cw = c_ref[...]
        return x_prev * cw[0:1] + x * cw[1:2] + x_next * cw[2:3]

    r = conv(r_ref, rp_ref, rn_ref, cr_ref)
    k = conv(k_ref, kp_ref, kn_ref, ck_ref)
    v = conv(v_ref, vp_ref, vn_ref, cv_ref)

    li = lax.broadcasted_iota(jnp.int32, (QUAD, QUAD), 0) // A_HEAD
    lj = lax.broadcasted_iota(jnp.int32, (QUAD, QUAD), 1) // A_HEAD
    msum = (li == lj).astype(BF16)

    kk = k * kkw_ref[...]
    kk = kk * lax.rsqrt(_dot((kk * kk).astype(BF16), msum) + L2_EPS)

    cb = min(tt, CUM_BLOCK)
    ti = lax.broadcasted_iota(jnp.int32, (cb, cb), 0)
    tj = lax.broadcasted_iota(jnp.int32, (cb, cb), 1)
    same_chunk = (ti // A_CHUNK) == (tj // A_CHUNK)

    lo = lo_ref[0].astype(F32)
    lact = jnp.tanh(lo[:, 0:2 * A_LORA])
    la = lo[:, 2 * A_LORA:4 * A_LORA]

    masks_b = _head_masks(A_CHUNK, BF16)
    masks_f = _head_masks(A_CHUNK, F32)
    ct = lax.broadcasted_iota(jnp.int32, (A_CHUNK, PAIR), 0)
    cs = lax.broadcasted_iota(jnp.int32, (A_CHUNK, PAIR), 1) % A_HEAD
    eye_q = (ct == cs).astype(F32)
    lane_c = lax.broadcasted_iota(jnp.int32, (SUB, PAIR), 1)
    blk_of_lane = (lane_c % A_HEAD) // SUB
    blk_f = [(blk_of_lane == b_).astype(F32) for b_ in range(A_CHUNK // SUB)]
    head_blk_b = [((lane_c // A_HEAD == h_) & (blk_of_lane == b_)).astype(BF16)
                  for h_ in range(2) for b_ in range(A_CHUNK // SUB)]
    eye_c = (lax.broadcasted_iota(jnp.int32, (SUB, PAIR), 0) == lane_c % SUB).astype(F32)
    merges = [(((ct // (2 * w)) == (cs // (2 * w))) & ((ct // w) != (cs // w))).astype(F32)
              for w in (SUB, 2 * SUB, 4 * SUB)]

    bonus = jnp.zeros((tt, QUAD), F32)
    outs = ((rtf_ref, oif_ref, phf_ref, dlf_ref), (rtb_ref, oib_ref, phb_ref, dlb_ref))
    prep = []
    for d in range(2):
        if d == 0:
            tri = (same_chunk & (tj <= ti)).astype(BF16)
            strict = (cs < ct).astype(F32)
            incl = (cs <= ct).astype(F32)
        else:
            tri = (same_chunk & (tj >= ti)).astype(BF16)
            strict = (cs > ct).astype(F32)
            incl = (cs >= ct).astype(F32)
        z = w0_ref[d:d + 1, :] + _dot3(lact, wup_ref[d])
        logw = -DECAY_SCALE * _sigmoid(z)
        av = _sigmoid(a0_ref[d:d + 1, :] + _dot3(la, aup_ref[d]))
        kd = k * (1.0 + (av - 1.0) * kaw_ref[...])
        bonus = bonus + _dot((r * kd * rkw_ref[...]).astype(BF16), msum) * v

        cum = jnp.concatenate([_dot_exact_lhs(tri, logw[j * cb:(j + 1) * cb]) for j in range(tt // cb)], axis=0)
        last = 0 if d == 1 else A_CHUNK - 1
        ctot = jnp.concatenate([jnp.broadcast_to(cum[c * A_CHUNK + last:c * A_CHUNK + last + 1, :], (A_CHUNK, QUAD))
                                for c in range(tt // A_CHUNK)], axis=0)
        e_inv = jnp.exp(-cum)
        e_g = jnp.exp(ctot - cum)
        kka = kk * av
        prep.append(dict(kh=kk * jnp.exp(cum - logw), rh=r * jnp.exp(cum), bt=kka * e_inv, kt=kd * e_inv,
                         bg=kka * e_g, kg=kd * e_g, gam=jnp.exp(ctot), strict=strict, incl=incl))

    units = [(d, c, pr) for d in range(2) for c in range(tt // A_CHUNK) for pr in range(QUAD // PAIR)]

    def cut(x, c, pr):
        return x[c * A_CHUNK:(c + 1) * A_CHUNK, pr * PAIR:(pr + 1) * PAIR]

    def bd(x):
        return _bd2(x.astype(BF16), masks_b)

    def pmm(a, w):
        return _dot(a.astype(BF16), bd(w))

    def stage_a(d, c, pr):
        pd = prep[d]
        kh, rh = cut(pd["kh"], c, pr), cut(pd["rh"], c, pr)
        lhs1 = jnp.concatenate([kh, rh], axis=0).astype(BF16)
        rhs1 = jnp.concatenate([bd(cut(pd["bt"], c, pr)), bd(cut(pd["kt"], c, pr))], axis=0)
        a_all = _dot_nt(lhs1, rhs1)
        return dict(kh=kh, rh=rh,
                    a_ab=a_all[0:A_CHUNK, 0:PAIR] * pd["strict"],
                    a_ak=a_all[0:A_CHUNK, PAIR:] * pd["strict"],
                    a_rb=a_all[A_CHUNK:, 0:PAIR] * pd["incl"],
                    a_rk=a_all[A_CHUNK:, PAIR:] * pd["incl"])

    st = [stage_a(*u_) for u_ in units]

    def compress(x):
        acc = x[0:SUB] * blk_f[0]
        for b_ in range(1, A_CHUNK // SUB):
            acc = acc + x[b_ * SUB:(b_ + 1) * SUB] * blk_f[b_]
        return acc

    def expand_w(xc):
        xb = xc.astype(BF16)
        return jnp.concatenate([xb * m for m in head_blk_b], axis=0)

    ps = [-compress(s_["a_ab"]) for s_ in st]
    ts = [eye_c + p for p in ps]
    ps = [_dot(p.astype(BF16), expand_w(p)) for p in ps]
    both = [_dot(jnp.concatenate([p, t], axis=0).astype(BF16), expand_w(p)) for p, t in zip(ps, ts)]
    ps = [b_[0:SUB] for b_ in both]
    ts = [t + b_[SUB:] for t, b_ in zip(ts, both)]
    ts = [t + _dot(t.astype(BF16), expand_w(p)) for t, p in zip(ts, ps)]
    ts = [jnp.concatenate([t * m for m in blk_f], axis=0) for t in ts]
    for blk in merges:
        gs = [pmm(t, s_["a_ab"] * blk) for t, s_ in zip(ts, st)]
        ts = [t - pmm(g, t) for t, g in zip(ts, gs)]

    vs = [cut(v, c, pr) for d, c, pr in units]
    xas = [pmm(jnp.concatenate([s_["a_ak"], s_["a_rk"]], axis=0), vc) for s_, vc in zip(st, vs)]
    wus = [_dot(t.astype(BF16), jnp.concatenate([bd(s_["kh"]), bd(xa[0:A_CHUNK])], axis=1))
           for t, s_, xa in zip(ts, st, xas)]
    wubs = [wu.astype(BF16) for wu in wus]
    arbs = [_dot(s_["a_rb"].astype(BF16), jnp.concatenate([_bd2(wb[:, 0:PAIR], masks_b),
                                                          _bd2(wb[:, PAIR:], masks_b)], axis=1))
            for s_, wb in zip(st, wubs)]
    zzs = [_dot_tn(cut(prep[d]["bg"], c, pr).astype(BF16), wb) for (d, c, pr), wb in zip(units, wubs)]
    z2s = [_dot_tn(cut(prep[d]["kg"], c, pr).astype(BF16), vc.astype(BF16)) for (d, c, pr), vc in zip(units, vs)]

    for (d, c, pr), s_, xa, arb, zz, z2 in zip(units, st, xas, arbs, zzs, z2s):
        rt_ref, oi_ref, ph_ref, dl_ref = outs[d]
        gam = cut(prep[d]["gam"], c, pr)
        rw = slice(c * A_CHUNK, (c + 1) * A_CHUNK)
        ln = slice(pr * PAIR, (pr + 1) * PAIR)
        rt_ref[0, rw, ln] = (s_["rh"] - arb[:, 0:PAIR]).astype(rt_ref.dtype)
        oi_ref[0, rw, ln] = (xa[A_CHUNK:] - arb[:, PAIR:]).astype(oi_ref.dtype)
        ph_ref[0, rw, ln] = (eye_q * gam - _fold2(zz[:, 0:PAIR], masks_f)).astype(ph_ref.dtype)
        dl_ref[0, rw, ln] = (_fold2(z2, masks_f) - _fold2(zz[:, PAIR:], masks_f)).astype(dl_ref.dtype)
    bon_ref[0] = bonus.astype(bon_ref.dtype)


def _rwkv_par(p, a_conv, k_k, k_a, r_k, w0, a0, wup_pad, aup_pad, tt):
    b, l, _ = p.shape
    n_tiles = l // tt
    h16 = tt // 16
    last16 = l // 16 - 1

    def tok(col0):
        return pl.BlockSpec((1, tt, QUAD), lambda bi, i, q: (bi, i, col0 // QUAD + q))

    def prev(col0):
        return pl.BlockSpec((1, 16, QUAD), lambda bi, i, q: (bi, jnp.maximum(i * h16 - 1, 0), col0 // QUAD + q))

    def nxt(col0):
        return pl.BlockSpec((1, 16, QUAD),
                            lambda bi, i, q: (bi, jnp.minimum((i + 1) * h16, last16), col0 // QUAD + q))

    def vec(rows, col0=0):
        return pl.BlockSpec((rows, QUAD), lambda bi, i, q: (0, col0 // QUAD + q))

    out_tok = pl.BlockSpec((1, tt, QUAD), lambda bi, i, q: (bi, i, q))
    out_sd = jax.ShapeDtypeStruct((b, l, D_MODEL), BF16)
    kern = functools.partial(_rwkv_par_kernel, tt=tt, n_tiles=n_tiles)
    return pl.pallas_call(
        kern,
        grid=(b, n_tiles, N_QUADS),
        in_specs=[tok(0), tok(1024), tok(2048), prev(0), prev(1024), prev(2048),
                  nxt(0), nxt(1024), nxt(2048),
                  pl.BlockSpec((1, tt, 4 * A_LORA), lambda bi, i, q: (bi, i, OFF_LO // (4 * A_LORA))),
                  vec(3, 0), vec(3, 1024), vec(3, 2048),
                  vec(1), vec(1), vec(1), vec(2), vec(2),
                  pl.BlockSpec((2, 2 * A_LORA, QUAD), lambda bi, i, q: (0, 0, q)),
                  pl.BlockSpec((2, 2 * A_LORA, QUAD), lambda bi, i, q: (0, 0, q))],
        out_specs=[out_tok] * 9,
        out_shape=[out_sd] * 9,
        compiler_params=_cparams(("arbitrary", "arbitrary", "arbitrary")),
        name="rwkv_par",
    )(p, p, p, p, p, p, p, p, p, p, a_conv, a_conv, a_conv,
      k_k.reshape(1, D_MODEL), k_a.reshape(1, D_MODEL), r_k.reshape(1, D_MODEL), w0, a0, wup_pad, aup_pad)


def _rwkv_seq_kernel(rtf_ref, oif_ref, phf_ref, dlf_ref, rtb_ref, oib_ref, phb_ref, dlb_ref,
                     h0f_ref, h0b_ref, of_ref, ob_ref, hff_ref, hfb_ref, hf_s, hb_s, *, tt):
    i = pl.program_id(1)

    @pl.when(i == 0)
    def _():
        hf_s[...] = h0f_ref[0]
        hb_s[...] = h0b_ref[0]

    masks_b = _head_masks(A_CHUNK, BF16)
    nc = tt // A_CHUNK
    fwd_refs = (rtf_ref, phf_ref, dlf_ref, oif_ref, of_ref)
    bwd_refs = (rtb_ref, phb_ref, dlb_ref, oib_ref, ob_ref)

    n_pairs = D_MODEL // PAIR

    def pair(j):
        return slice(j * PAIR, (j + 1) * PAIR)

    hs = [hf_s[:, pair(j)] for j in range(n_pairs)] + [hb_s[:, pair(j)] for j in range(n_pairs)]
    for c in range(nc):
        chains = [(fwd_refs, c, j) for j in range(n_pairs)] + [(bwd_refs, nc - 1 - c, j) for j in range(n_pairs)]
        boths = []
        for h, (refs, cc, j) in zip(hs, chains):
            sl = slice(cc * A_CHUNK, (cc + 1) * A_CHUNK)
            lhs = jnp.concatenate([refs[0][0, sl, pair(j)], refs[1][0, sl, pair(j)]], axis=0)
            boths.append(_dot(lhs, _bd2(h.astype(BF16), masks_b)))
        new_hs = []
        for both, (refs, cc, j) in zip(boths, chains):
            sl = slice(cc * A_CHUNK, (cc + 1) * A_CHUNK)
            refs[4][0, sl, pair(j)] = (both[0:A_CHUNK] + refs[3][0, sl, pair(j)].astype(F32)).astype(refs[4].dtype)
            new_hs.append(both[A_CHUNK:] + refs[2][0, sl, pair(j)].astype(F32))
        hs = new_hs
    for j in range(n_pairs):
        hf_s[:, pair(j)] = hs[j]
        hb_s[:, pair(j)] = hs[n_pairs + j]
        hff_ref[0, :, pair(j)] = hs[j]
        hfb_ref[0, :, pair(j)] = hs[n_pairs + j]


def _rwkv_seq(par, h0f, h0b, tt):
    rtf, oif, phf, dlf, rtb, oib, phb, dlb = par
    b, l, _ = rtf.shape
    n_tiles = l // tt
    fwd = pl.BlockSpec((1, tt, D_MODEL), lambda bi, i: (bi, i, 0))
    bwd = pl.BlockSpec((1, tt, D_MODEL), lambda bi, i: (bi, n_tiles - 1 - i, 0))
    st = pl.BlockSpec((1, A_HEAD, D_MODEL), lambda bi, i: (bi, 0, 0))
    o_sd = jax.ShapeDtypeStruct((b, l, D_MODEL), BF16)
    h_sd = jax.ShapeDtypeStruct((b, A_HEAD, D_MODEL), F32)
    return pl.pallas_call(
        functools.partial(_rwkv_seq_kernel, tt=tt),
        grid=(b, n_tiles),
        in_specs=[fwd] * 4 + [bwd] * 4 + [st, st],
        out_specs=[fwd, bwd, st, st],
        out_shape=[o_sd, o_sd, h_sd, h_sd],
        scratch_shapes=[pltpu.VMEM((A_HEAD, D_MODEL), F32), pltpu.VMEM((A_HEAD, D_MODEL), F32)],
        compiler_params=_cparams(("arbitrary", "arbitrary")),
        name="rwkv_seq",
    )(rtf, oif, phf, dlf, rtb, oib, phb, dlb, h0f, h0b)


def _ret_kernel(lg_ref, q_ref, k_ref, v_ref, rc_ref, rs_ref, cc_ref, cs_ref, s0_ref, *rest,
                tt, reverse, with_intra, rotate, n_heads, n_tiles):
    rest = list(rest)
    prev_ref = rest.pop(0) if with_intra else None
    o_ref, sf_ref = rest[0:2]
    qr_ref, kr_ref = rest[2:4] if rotate else (None, None)
    s_s = rest[-1]
    bh = pl.program_id(0)
    i = pl.program_id(1)
    h = bh % n_heads

    @pl.when(i == 0)
    def _():
        s_s[...] = s0_ref[0]

    lg_f = lg_ref[0, h]
    lg_b = lg_ref[1, h]
    lg = lg_b if reverse else lg_f

    n_rows = tt // GRID_W
    r0 = (n_tiles - 1 - i if reverse else i) * n_rows

    def by_row(ref):
        return jnp.concatenate([jnp.broadcast_to(ref[pl.ds(r0 + j, 1), :], (GRID_W, 128)) for j in range(n_rows)], axis=0)

    def by_col(ref):
        return jnp.concatenate([ref[...]] * n_rows, axis=0)

    cos_r, sin_r, cos_c, sin_c = by_row(rc_ref), by_row(rs_ref), by_col(cc_ref), by_col(cs_ref)

    def rope(x_ref):
        x = x_ref[0].astype(F32)
        xa, xb = x[:, 0:128], x[:, 128:256]
        return jnp.concatenate([xa * cos_r + pltpu.roll(xa, 64, axis=1) * sin_r,
                                xb * cos_c + pltpu.roll(xb, 64, axis=1) * sin_c], axis=1)

    if not rotate:
        q = q_ref[0].astype(F32)
        k = k_ref[0].astype(F32)
    else:
        q = rope(q_ref)
        k = rope(k_ref) * (R_QK ** -0.5)
        qr_ref[0] = q.astype(qr_ref.dtype)
        kr_ref[0] = k.astype(kr_ref.dtype)

    idx = lax.broadcasted_iota(jnp.int32, (R_BLOCK, R_QK), 0).astype(F32)
    if reverse:
        q_dec = jnp.exp(lg * (R_BLOCK - idx))
        k_dec = jnp.exp(lg * idx)
    else:
        q_dec = jnp.exp(lg * (idx + 1.0))
        k_dec = jnp.exp(lg * (R_BLOCK - 1.0 - idx))
    c_dec = jnp.exp(lg * R_BLOCK)

    nc = tt // R_BLOCK
    order = [nc - 1 - cc if reverse else cc for cc in range(nc)]

    def blk(c):
        return slice(c * R_BLOCK, (c + 1) * R_BLOCK)

    if with_intra:
        di = lax.broadcasted_iota(jnp.int32, (R_BLOCK, R_BLOCK), 0)
        dj = lax.broadcasted_iota(jnp.int32, (R_BLOCK, R_BLOCK), 1)
        diff = (di - dj).astype(F32)
        dmask = (jnp.where(diff >= 0, jnp.exp(lg_f * jnp.maximum(diff, 0.0)), 0.0)
                 + jnp.where(diff <= 0, jnp.exp(lg_b * jnp.maximum(-diff, 0.0)), 0.0))
        scores = {c: _dot_nt(q[blk(c)].astype(BF16), k[blk(c)].astype(BF16)) for c in order}
    kvs = {c: _dot_tn((k[blk(c)] * k_dec).astype(BF16), v_ref[0, blk(c), :]) for c in order}

    s = s_s[...]
    inter = {}
    for c in order:
        inter[c] = _dot((q[blk(c)] * q_dec).astype(BF16), s.astype(BF16))
        s = s * c_dec + kvs[c]
    s_s[...] = s
    sf_ref[0] = s
    for c in order:
        o = inter[c]
        if with_intra:
            o = o + _dot((scores[c] * dmask).astype(BF16), v_ref[0, blk(c), :]) + prev_ref[0, blk(c), :].astype(F32)
        o_ref[0, blk(c), :] = o.astype(o_ref.dtype)


def _ret_pass(p, lg, tabs, s0, prev, tt, reverse, rotated=None):
    with_intra = prev is not None
    rotate = rotated is None
    b, l, _ = p.shape
    n_tiles = l // tt
    nh = R_HEADS

    def tile(i):
        return n_tiles - 1 - i if reverse else i

    def col(width, off):
        return pl.BlockSpec((1, tt, width), lambda bh, i, lg_: (bh // nh, tile(i), off // width + bh % nh))

    def whole(a):
        return pl.BlockSpec(a.shape, lambda bh, i, lg_: (0, 0))

    st = pl.BlockSpec((1, R_QK, R_V), lambda bh, i, lg_: (bh, 0, 0))
    o_spec = pl.BlockSpec((1, tt, R_V), lambda bh, i, lg_: (bh // nh, tile(i), bh % nh))
    rot_spec = pl.BlockSpec((1, tt, R_QK), lambda bh, i, lg_: (bh // nh, tile(i), bh % nh))
    rot_sd = jax.ShapeDtypeStruct((b, l, nh * R_QK), BF16)
    qk_specs = [col(R_QK, OFF_Q), col(R_QK, OFF_K)] if rotate else [rot_spec, rot_spec]
    qk_args = [p, p] if rotate else list(rotated)
    kern = functools.partial(_ret_kernel, tt=tt, reverse=reverse, with_intra=with_intra, rotate=rotate, n_heads=nh,
                             n_tiles=n_tiles)
    grid_spec = pltpu.PrefetchScalarGridSpec(
        num_scalar_prefetch=1,
        grid=(b * nh, n_tiles),
        in_specs=qk_specs + [col(R_V, OFF_V)] + [whole(t) for t in tabs] + [st]
        + ([o_spec] if with_intra else []),
        out_specs=[o_spec, st] + ([rot_spec, rot_spec] if rotate else []),
        scratch_shapes=[pltpu.VMEM((R_QK, R_V), F32)],
    )
    return pl.pallas_call(
        kern,
        grid_spec=grid_spec,
        out_shape=[jax.ShapeDtypeStruct((b, l, nh * R_V), BF16),
                   jax.ShapeDtypeStruct((b * nh, R_QK, R_V), F32)] + ([rot_sd, rot_sd] if rotate else []),
        compiler_params=_cparams(("arbitrary", "arbitrary")),
        name="ret_bwd" if reverse else "ret_fwd",
    )(lg, *qk_args, p, *tabs, s0, *([prev] if with_intra else []))


def _final_kernel(x_ref, ga_ref, gr_ref, ma_ref, mb_ref, of_ref, ob_ref, bon_ref, or_ref,
                  gate_ref, alw_ref, alb_ref, rlw_ref, rlb_ref, fnw_ref, awo_ref, rwo_ref, wo_ref, o_ref):
    li = lax.broadcasted_iota(jnp.int32, (QUAD, QUAD), 0) // A_HEAD
    lj = lax.broadcasted_iota(jnp.int32, (QUAD, QUAD), 1) // A_HEAD
    msum = (li == lj).astype(BF16)
    inv = 1.0 / A_HEAD

    def head_sum(t):
        return jnp.concatenate([_dot(t[:, q * QUAD:(q + 1) * QUAD].astype(BF16), msum) for q in range(N_QUADS)], axis=1)

    oa = of_ref[0].astype(F32) + ob_ref[0].astype(F32)
    mu = head_sum(oa) * inv
    dv = oa - mu
    var = head_sum(dv * dv) * inv
    ya = dv * lax.rsqrt(var + A_GN_EPS) * alw_ref[...] + alb_ref[...] + bon_ref[0].astype(F32)
    ya = _dot((ya * _silu(ga_ref[0].astype(F32))).astype(BF16), awo_ref[...])

    orr = or_ref[0].astype(F32)
    parts = []
    for h in range(R_HEADS):
        oh = orr[:, h * R_V:(h + 1) * R_V]
        m = jnp.mean(oh, axis=-1, keepdims=True)
        dh = oh - m
        vh = jnp.mean(dh * dh, axis=-1, keepdims=True)
        parts.append(dh * lax.rsqrt(vh + R_GN_EPS))
    yr = jnp.concatenate(parts, axis=1) * rlw_ref[...] + rlb_ref[...]
    yr = _dot((yr * _silu(gr_ref[0].astype(F32))).astype(BF16), rwo_ref[...])

    merged = _sigmoid(ma_ref[0].astype(F32)) * ya + _sigmoid(mb_ref[0].astype(F32)) * yr
    y = _dot(merged.astype(BF16), wo_ref[...])
    xo = x_ref[0] + gate_ref[0] * y
    o_ref[0] = xo * lax.rsqrt(jnp.mean(xo * xo, axis=-1, keepdims=True) + NORM_EPS) * fnw_ref[...]


def _final(x, p, of, ob, bonus, o_ret, gate, a_ln_w, a_ln_b, r_ln_w, r_ln_b, final_w, awo, rwo, wo, tm):
    b, l, d = x.shape

    def tok(width, off=0):
        return pl.BlockSpec((1, tm, width), lambda bi, i: (bi, i, off // width))

    def vec(width):
        return pl.BlockSpec((1, width), lambda bi, i: (0, 0))

    def mat(r, c):
        return pl.BlockSpec((r, c), lambda bi, i: (0, 0))

    rv = R_HEADS * R_V
    return pl.pallas_call(
        _final_kernel,
        grid=(b, l // tm),
        in_specs=[tok(d), tok(d, OFF_GA), tok(rv, OFF_GR), tok(d, OFF_MA), tok(d, OFF_MB),
                  tok(d), tok(d), tok(d), tok(rv),
                  pl.BlockSpec((1, 1, d), lambda bi, i: (bi, 0, 0)),
                  vec(d), vec(d), vec(rv), vec(rv), vec(d), mat(d, d), mat(rv, d), mat(d, d)],
        out_specs=tok(d),
        out_shape=jax.ShapeDtypeStruct((b, l, d), F32),
        compiler_params=_cparams(("arbitrary", "arbitrary")),
        name="final",
    )(x, p, p, p, p, of, ob, bonus, o_ret, gate,
      a_ln_w.reshape(1, d), a_ln_b.reshape(1, d), r_ln_w.reshape(1, rv), r_ln_b.reshape(1, rv),
      final_w.reshape(1, d), awo, rwo, wo)


def _rope_tables(l, rotate):
    n_rows = l // GRID_W
    if not rotate:
        return (jnp.ones((n_rows, 128), F32), jnp.zeros((n_rows, 128), F32),
                jnp.ones((GRID_W, 128), F32), jnp.zeros((GRID_W, 128), F32))
    half = R_QK // 4
    freqs = ROPE_BASE ** (-jnp.arange(half, dtype=F32) / half)

    def tab(n):
        ang = jnp.arange(n, dtype=F32)[:, None] * freqs[None, :]
        return (jnp.concatenate([jnp.cos(ang), jnp.cos(ang)], axis=1),
                jnp.concatenate([-jnp.sin(ang), jnp.sin(ang)], axis=1))

    return tab(n_rows) + tab(GRID_W)


def _reorder_w_in(w):
    return jnp.concatenate([w[:, 0:4096], w[:, 4352:], w[:, 4096:4352]], axis=1)


def _pick_tile(l, pref):
    t = min(l, pref)
    while l % t:
        t //= 2
    return t


def kernel(x, c, ctx, c_ctx, norm_w, ada_w, ada_b, w_in, a_conv, a_w_up, a_w0, a_a_up, a_a0, a_k_k, a_k_a,
           a_r_k, a_ln_w, a_ln_b, a_w_out, r_decay, r_ln_w, r_ln_b, r_w_out, w_out, final_norm_w):
    b, l, d = x.shape
    lc = ctx.shape[1]
    assert d == D_MODEL and l % R_BLOCK == 0 and lc % R_BLOCK == 0
    lyr = 0

    cc = jnp.zeros((16, d), F32).at[:b].set(c).at[b].set(c_ctx)
    mod = _adaln(cc, ada_w[lyr], ada_b[lyr])
    shift, scale, gate = mod[:b, :d], mod[:b, d:2 * d], mod[:b, 2 * d:]
    shift_c = jnp.broadcast_to(mod[b, :d], (b, d))
    scale_c = jnp.broadcast_to(mod[b, d:2 * d], (b, d))

    w_bf = _reorder_w_in(w_in[lyr]).astype(BF16)
    p_x = _inproj(x, (1.0 + scale)[:, None, :], shift[:, None, :], norm_w[lyr], w_bf, _pick_tile(l, 2048))
    p_c = _inproj(ctx.reshape(1, b * lc, d), (1.0 + scale_c)[:1, None, :], shift_c[:1, None, :], norm_w[lyr], w_bf,
                  _pick_tile(b * lc, 2048)).reshape(b, lc, -1)

    zpad = jnp.zeros((A_LORA, d), F32)
    wup_pad = jnp.stack([jnp.concatenate([a_w_up[lyr, 0], zpad], 0), jnp.concatenate([zpad, a_w_up[lyr, 1]], 0)])
    aup_pad = jnp.stack([jnp.concatenate([a_a_up[lyr, 0], zpad], 0), jnp.concatenate([zpad, a_a_up[lyr, 1]], 0)])
    a_args = (a_conv[lyr], a_k_k[lyr], a_k_a[lyr], a_r_k[lyr].reshape(-1), a_w0[lyr], a_a0[lyr], wup_pad, aup_pad)
    h_zero = jnp.zeros((b, A_HEAD, d), F32)
    par_c = _rwkv_par(p_c, *a_args, tt=_pick_tile(lc, 256))
    _, _, hcf, hcb = _rwkv_seq(par_c[:8], h_zero, h_zero, _pick_tile(lc, 512))
    par_x = _rwkv_par(p_x, *a_args, tt=_pick_tile(l, 1024))
    o_af, o_ab, _, _ = _rwkv_seq(par_x[:8], hcf, hcb, _pick_tile(l, 512))
    bonus = par_x[8]

    lg = -jnp.exp(r_decay[lyr].astype(F32))
    tabs_x = _rope_tables(l, True)
    tabs_c = _rope_tables(lc, False)
    s_zero = jnp.zeros((b * R_HEADS, R_QK, R_V), F32)
    tc = _pick_tile(lc, 1024)
    _, sc_b, qc_rot, kc_rot = _ret_pass(p_c, lg, tabs_c, s_zero, None, tc, True)
    _, sc_f = _ret_pass(p_c, lg, tabs_c, s_zero, None, tc, False, (qc_rot, kc_rot))
    tx = _pick_tile(l, 2048)
    o_rb, _, q_rot, k_rot = _ret_pass(p_x, lg, tabs_x, sc_b, None, tx, True)
    o_r, _ = _ret_pass(p_x, lg, tabs_x, sc_f, o_rb, tx, False, (q_rot, k_rot))

    return _final(x, p_x, o_af, o_ab, bonus, o_r, gate[:, None, :], a_ln_w[lyr], a_ln_b[lyr],
                  r_ln_w[lyr], r_ln_b[lyr], final_norm_w, a_w_out[lyr].astype(BF16),
                  r_w_out[lyr].astype(BF16), w_out[lyr].astype(BF16), _pick_tile(l, 512))
```

```python
import functools
import math

import jax
import jax.numpy as jnp
from jax import lax
from jax.experimental import pallas as pl
from jax.experimental.pallas import tpu as pltpu

F32 = jnp.float32
BF16 = jnp.bfloat16

D_MODEL = 1024
A_HEAD = 64
A_HEADS = D_MODEL // A_HEAD
A_LORA = 64
QUAD = 4 * A_HEAD
PAIR = 2 * A_HEAD
N_QUADS = D_MODEL // QUAD
A_CHUNK = 64
SUB = 8
CUM_BLOCK = 256
R_HEADS = 4
R_QK = 256
R_V = 512
R_BLOCK = 256
GRID_W = 64
ROPE_BASE = 10000.0
NORM_EPS = 1e-6
A_GN_EPS = 64e-5
R_GN_EPS = 1e-5
L2_EPS = 1e-12
DECAY_SCALE = math.exp(-0.5)

OFF_RKV, OFF_GA, OFF_Q, OFF_K, OFF_V, OFF_GR, OFF_MA, OFF_MB, OFF_LO = (
    0, 3072, 4096, 5120, 6144, 8192, 10240, 11264, 12288)
V7X_VMEM_LIMIT = 56 * 1024 * 1024


def _cparams(sem):
    return pltpu.CompilerParams(dimension_semantics=sem, vmem_limit_bytes=V7X_VMEM_LIMIT)


def _dot(a, b):
    return jnp.dot(a, b, preferred_element_type=F32)


def _dot_nt(a, b):
    return lax.dot_general(a, b, (((1,), (1,)), ((), ())), preferred_element_type=F32)


def _dot_tn(a, b):
    return lax.dot_general(a, b, (((0,), (0,)), ((), ())), preferred_element_type=F32)


def _split(x):
    hi = x.astype(BF16)
    lo = (x - hi.astype(F32)).astype(BF16)
    return hi, lo


def _dot_exact_lhs(m, x):
    hi, lo = _split(x)
    return _dot(m, hi) + _dot(m, lo)


def _dot3(a, b):
    ah, al = _split(a)
    bh, bl = _split(b)
    return _dot(ah, bh) + _dot(ah, bl) + _dot(al, bh)


def _sigmoid(x):
    return 0.5 * jnp.tanh(0.5 * x) + 0.5


def _silu(x):
    return x * _sigmoid(x)


def _silu_half(h):
    return h + h * jnp.tanh(h)


def _adaln_kernel(c_ref, w_ref, b_ref, o_ref):
    cs = _silu(c_ref[...])
    o_ref[...] = _dot3(cs, w_ref[...]) + b_ref[...]


def _adaln(cc, ada_w, ada_b):
    n = ada_w.shape[1]
    tn = 512
    return pl.pallas_call(
        _adaln_kernel,
        grid=(n // tn,),
        in_specs=[pl.BlockSpec(cc.shape, lambda j: (0, 0)),
                  pl.BlockSpec((ada_w.shape[0], tn), lambda j: (0, j)),
                  pl.BlockSpec((1, tn), lambda j: (0, j))],
        out_specs=pl.BlockSpec((cc.shape[0], tn), lambda j: (0, j)),
        out_shape=jax.ShapeDtypeStruct((cc.shape[0], n), F32),
        compiler_params=_cparams(("arbitrary",)),
        name="adaln",
    )(cc, ada_w, ada_b.reshape(1, n))


def _inproj_kernel(x_ref, sc_ref, sh_ref, nw_ref, w_ref, o_ref, u_ref):
    @pl.when(pl.program_id(2) == 0)
    def _():
        xf = x_ref[0]
        y = xf * lax.rsqrt(jnp.mean(xf * xf, axis=-1, keepdims=True) + NORM_EPS) * nw_ref[...]
        u_ref[...] = (y * sc_ref[0] + sh_ref[0]).astype(BF16)

    o_ref[0] = _dot(u_ref[...], w_ref[...]).astype(o_ref.dtype)


def _inproj(x, scale1p, shift, norm_w, w_bf16, tm):
    b, l, d = x.shape
    n = w_bf16.shape[1]
    tn = 1792
    return pl.pallas_call(
        _inproj_kernel,
        grid=(b, l // tm, n // tn),
        in_specs=[pl.BlockSpec((1, tm, d), lambda bi, i, j: (bi, i, 0)),
                  pl.BlockSpec((1, 1, d), lambda bi, i, j: (bi, 0, 0)),
                  pl.BlockSpec((1, 1, d), lambda bi, i, j: (bi, 0, 0)),
                  pl.BlockSpec((1, d), lambda bi, i, j: (0, 0)),
                  pl.BlockSpec((d, tn), lambda bi, i, j: (0, j))],
        out_specs=pl.BlockSpec((1, tm, tn), lambda bi, i, j: (bi, i, j)),
        out_shape=jax.ShapeDtypeStruct((b, l, n), BF16),
        scratch_shapes=[pltpu.VMEM((tm, d), BF16)],
        compiler_params=_cparams(("arbitrary", "arbitrary", "arbitrary")),
        name="inproj",
    )(x, scale1p, shift, norm_w.reshape(1, d), w_bf16)


def _head_masks(rows, dtype):
    lane_head = lax.broadcasted_iota(jnp.int32, (rows, PAIR), 1) // A_HEAD
    return [(lane_head == h).astype(dtype) for h in range(2)]


def _bd2(x_bf16, masks):
    return jnp.concatenate([x_bf16 * m for m in masks], axis=0)


def _fold2(z, masks_f32):
    return z[0:A_HEAD] * masks_f32[0] + z[A_HEAD:2 * A_HEAD] * masks_f32[1]


def _rwkv_par_kernel(r_ref, k_ref, v_ref, rp_ref, kp_ref, vp_ref, rn_ref, kn_ref, vn_ref, lo_ref,
                     cr_ref, ck_ref, cv_ref, kkw_ref, kaw_ref, rkw_ref, w0_ref, a0_ref, wup_ref, aup_ref,
                     rtf_ref, oif_ref, phf_ref, dlf_ref, rtb_ref, oib_ref, phb_ref, dlb_ref, bon_ref,
                     *, tt, n_tiles):
    i = pl.program_id(1)
    row = lax.broadcasted_iota(jnp.int32, (tt, QUAD), 0)
    has_prev = (i > 0).astype(F32)
    has_next = (i < n_tiles - 1).astype(F32)

    def conv(x_ref, p_ref, n_ref, c_ref):
        x = x_ref[0].astype(F32)
        prev_row = p_ref[0][15:16, :].astype(F32) * has_prev
        next_row = n_ref[0][0:1, :].astype(F32) * has_next
        x_prev = jnp.where(row == 0, prev_row, pltpu.roll(x, 1, axis=0))
        x_next = jnp.where(row == tt - 1, next_row, pltpu.roll(x, tt - 1, axis=0))
        cw = c_ref[...]
        return x_prev * cw[0:1] + x * cw[1:2] + x_next * cw[2:3]

    r = conv(r_ref, rp_ref, rn_ref, cr_ref)
    k = conv(k_ref, kp_ref, kn_ref, ck_ref)
    v = conv(v_ref, vp_ref, vn_ref, cv_ref)

    li = lax.broadcasted_iota(jnp.int32, (QUAD, QUAD), 0) // A_HEAD
    lj = lax.broadcasted_iota(jnp.int32, (QUAD, QUAD), 1) // A_HEAD
    msum = (li == lj).astype(BF16)

    kk = k * kkw_ref[...]
    kk = kk * lax.rsqrt(_dot((kk * kk).astype(BF16), msum) + L2_EPS)

    cb = min(tt, CUM_BLOCK)
    ti = lax.broadcasted_iota(jnp.int32, (cb, cb), 0)
    tj = lax.broadcasted_iota(jnp.int32, (cb, cb), 1)
    same_chunk = (ti // A_CHUNK) == (tj // A_CHUNK)

    lo = lo_ref[0].astype(F32)
    lact = jnp.tanh(lo[:, 0:2 * A_LORA])
    la = lo[:, 2 * A_LORA:4 * A_LORA]

    masks_b = _head_masks(A_CHUNK, BF16)
    masks_f = _head_masks(A_CHUNK, F32)
    ct = lax.broadcasted_iota(jnp.int32, (A_CHUNK, PAIR), 0)
    cs = lax.broadcasted_iota(jnp.int32, (A_CHUNK, PAIR), 1) % A_HEAD
    eye_q = (ct == cs).astype(F32)
    lane_c = lax.broadcasted_iota(jnp.int32, (SUB, PAIR), 1)
    blk_of_lane = (lane_c % A_HEAD) // SUB
    blk_f = [(blk_of_lane == b_).astype(F32) for b_ in range(A_CHUNK // SUB)]
    head_blk_b = [((lane_c // A_HEAD == h_) & (blk_of_lane == b_)).astype(BF16)
                  for h_ in range(2) for b_ in range(A_CHUNK // SUB)]
    eye_c = (lax.broadcasted_iota(jnp.int32, (SUB, PAIR), 0) == lane_c % SUB).astype(F32)
    merges_b = [(((ct // (2 * w)) == (cs // (2 * w))) & ((ct // w) != (cs // w))).astype(BF16)
                for w in (SUB, 2 * SUB, 4 * SUB)]

    bonus = jnp.zeros((tt, QUAD), F32)
    outs = ((rtf_ref, oif_ref, phf_ref, dlf_ref), (rtb_ref, oib_ref, phb_ref, dlb_ref))
    prep = []
    for d in range(2):
        if d == 0:
            tri = (same_chunk & (tj <= ti)).astype(BF16)
            strict = (cs < ct).astype(F32)
            incl = (cs <= ct).astype(F32)
        else:
            tri = (same_chunk & (tj >= ti)).astype(BF16)
            strict = (cs > ct).astype(F32)
            incl = (cs >= ct).astype(F32)
        z = w0_ref[d:d + 1, :] + _dot3(lact, wup_ref[d])
        logw = -DECAY_SCALE * _sigmoid(z)
        av = _sigmoid(a0_ref[d:d + 1, :] + _dot3(la, aup_ref[d]))
        kd = k * (1.0 + (av - 1.0) * kaw_ref[...])
        bonus = bonus + _dot((r * kd * rkw_ref[...]).astype(BF16), msum) * v

        cum = jnp.concatenate([_dot_exact_lhs(tri, logw[j * cb:(j + 1) * cb]) for j in range(tt // cb)], axis=0)
        last = 0 if d == 1 else A_CHUNK - 1
        ctot = jnp.concatenate([jnp.broadcast_to(cum[c * A_CHUNK + last:c * A_CHUNK + last + 1, :], (A_CHUNK, QUAD))
                                for c in range(tt // A_CHUNK)], axis=0)
        e_inv = jnp.exp(-cum)
        e_g = jnp.exp(ctot - cum)
        kka = kk * av
        prep.append(dict(kh=kk * jnp.exp(cum - logw), rh=r * jnp.exp(cum), bt=kka * e_inv, kt=kd * e_inv,
                         bg=kka * e_g, kg=kd * e_g, gam=jnp.exp(ctot), strict_b=strict.astype(BF16), incl_b=incl.astype(BF16),
                         diag=[strict[b_ * SUB:(b_ + 1) * SUB] * blk_f[b_] for b_ in range(A_CHUNK // SUB)]))

    units = [(d, c, pr) for d in range(2) for c in range(tt // A_CHUNK) for pr in range(QUAD // PAIR)]

    def cut(x, c, pr):
        return x[c * A_CHUNK:(c + 1) * A_CHUNK, pr * PAIR:(pr + 1) * PAIR]

    def bd(x):
        return _bd2(x.astype(BF16), masks_b)

    def stage_a(d, c, pr):
        pd = prep[d]
        kh, rh = cut(pd["kh"], c, pr), cut(pd["rh"], c, pr)
        lhs1 = jnp.concatenate([kh, rh], axis=0).astype(BF16)
        rhs1 = jnp.concatenate([bd(cut(pd["bt"], c, pr)), bd(cut(pd["kt"], c, pr))], axis=0)
        a_all = _dot_nt(lhs1, rhs1)
        a_b = a_all.astype(BF16)
        diag = a_all[0:SUB, 0:PAIR] * pd["diag"][0]
        for b_ in range(1, A_CHUNK // SUB):
            diag = diag + a_all[b_ * SUB:(b_ + 1) * SUB, 0:PAIR] * pd["diag"][b_]
        return dict(kh=kh, rh=rh, diag=diag,
                    a_ab=a_b[0:A_CHUNK, 0:PAIR] * pd["strict_b"],
                    a_kr=jnp.concatenate([a_b[0:A_CHUNK, PAIR:] * pd["strict_b"],
                                          a_b[A_CHUNK:, PAIR:] * pd["incl_b"]], axis=0),
                    a_rb=a_b[A_CHUNK:, 0:PAIR] * pd["incl_b"])

    st = [stage_a(*u_) for u_ in units]

    def expand_w(xc):
        xb = xc.astype(BF16)
        return jnp.concatenate([xb * m for m in head_blk_b], axis=0)

    ps = [-s_["diag"] for s_ in st]
    ts = [eye_c + p for p in ps]
    ps = [_dot(p.astype(BF16), expand_w(p)) for p in ps]
    both = [_dot(jnp.concatenate([p, t], axis=0).astype(BF16), expand_w(p)) for p, t in zip(ps, ts)]
    ps = [b_[0:SUB] for b_ in both]
    ts = [t + b_[SUB:] for t, b_ in zip(ts, both)]
    ts = [t + _dot(t.astype(BF16), expand_w(p)) for t, p in zip(ts, ps)]
    ts = [jnp.concatenate([t * m for m in blk_f], axis=0).astype(BF16) for t in ts]
    for blk in merges_b:
        gs = [_dot(t, _bd2(s_["a_ab"] * blk, masks_b)).astype(BF16) for t, s_ in zip(ts, st)]
        ts = [t - _dot(g, _bd2(t, masks_b)).astype(BF16) for t, g in zip(ts, gs)]

    vs = [cut(v, c, pr) for d, c, pr in units]
    xas = [_dot(s_["a_kr"], bd(vc)) for s_, vc in zip(st, vs)]
    wus = [_dot(t, jnp.concatenate([bd(s_["kh"]), bd(xa[0:A_CHUNK])], axis=1))
           for t, s_, xa in zip(ts, st, xas)]
    wubs = [wu.astype(BF16) for wu in wus]
    arbs = [_dot(s_["a_rb"], jnp.concatenate([_bd2(wb[:, 0:PAIR], masks_b),
                                              _bd2(wb[:, PAIR:], masks_b)], axis=1))
            for s_, wb in zip(st, wubs)]
    zzs = [_dot_tn(cut(prep[d]["bg"], c, pr).astype(BF16), wb) for (d, c, pr), wb in zip(units, wubs)]
    z2s = [_dot_tn(cut(prep[d]["kg"], c, pr).astype(BF16), vc.astype(BF16)) for (d, c, pr), vc in zip(units, vs)]

    for (d, c, pr), s_, xa, arb, zz, z2 in zip(units, st, xas, arbs, zzs, z2s):
        rt_ref, oi_ref, ph_ref, dl_ref = outs[d]
        gam = cut(prep[d]["gam"], c, pr)
        rw = slice(c * A_CHUNK, (c + 1) * A_CHUNK)
        ln = slice(pr * PAIR, (pr + 1) * PAIR)
        rt_ref[0, rw, ln] = (s_["rh"] - arb[:, 0:PAIR]).astype(rt_ref.dtype)
        oi_ref[0, rw, ln] = (xa[A_CHUNK:] - arb[:, PAIR:]).astype(oi_ref.dtype)
        ph_ref[0, rw, ln] = (eye_q * gam - _fold2(zz[:, 0:PAIR], masks_f)).astype(ph_ref.dtype)
        dl_ref[0, rw, ln] = (_fold2(z2, masks_f) - _fold2(zz[:, PAIR:], masks_f)).astype(dl_ref.dtype)
    bon_ref[0] = bonus.astype(bon_ref.dtype)


def _rwkv_par(p, a_conv, k_k, k_a, r_k, w0, a0, wup_pad, aup_pad, tt):
    b, l, _ = p.shape
    n_tiles = l // tt
    h16 = tt // 16
    last16 = l // 16 - 1

    def tok(col0):
        return pl.BlockSpec((1, tt, QUAD), lambda bi, i, q: (bi, i, col0 // QUAD + q))

    def prev(col0):
        return pl.BlockSpec((1, 16, QUAD), lambda bi, i, q: (bi, jnp.maximum(i * h16 - 1, 0), col0 // QUAD + q))

    def nxt(col0):
        return pl.BlockSpec((1, 16, QUAD),
                            lambda bi, i, q: (bi, jnp.minimum((i + 1) * h16, last16), col0 // QUAD + q))

    def vec(rows, col0=0):
        return pl.BlockSpec((rows, QUAD), lambda bi, i, q: (0, col0 // QUAD + q))

    out_tok = pl.BlockSpec((1, tt, QUAD), lambda bi, i, q: (bi, i, q))
    out_sd = jax.ShapeDtypeStruct((b, l, D_MODEL), BF16)
    kern = functools.partial(_rwkv_par_kernel, tt=tt, n_tiles=n_tiles)
    return pl.pallas_call(
        kern,
        grid=(b, n_tiles, N_QUADS),
        in_specs=[tok(0), tok(1024), tok(2048), prev(0), prev(1024), prev(2048),
                  nxt(0), nxt(1024), nxt(2048),
                  pl.BlockSpec((1, tt, 4 * A_LORA), lambda bi, i, q: (bi, i, OFF_LO // (4 * A_LORA))),
                  vec(3, 0), vec(3, 1024), vec(3, 2048),
                  vec(1), vec(1), vec(1), vec(2), vec(2),
                  pl.BlockSpec((2, 2 * A_LORA, QUAD), lambda bi, i, q: (0, 0, q)),
                  pl.BlockSpec((2, 2 * A_LORA, QUAD), lambda bi, i, q: (0, 0, q))],
        out_specs=[out_tok] * 9,
        out_shape=[out_sd] * 9,
        compiler_params=_cparams(("arbitrary", "arbitrary", "arbitrary")),
        name="rwkv_par",
    )(p, p, p, p, p, p, p, p, p, p, a_conv, a_conv, a_conv,
      k_k.reshape(1, D_MODEL), k_a.reshape(1, D_MODEL), r_k.reshape(1, D_MODEL), w0, a0, wup_pad, aup_pad)


def _rwkv_seq_kernel(rtf_ref, oif_ref, phf_ref, dlf_ref, rtb_ref, oib_ref, phb_ref, dlb_ref,
                     h0f_ref, h0b_ref, of_ref, ob_ref, hff_ref, hfb_ref, hf_s, hb_s, *, tt):
    i = pl.program_id(1)

    @pl.when(i == 0)
    def _():
        hf_s[...] = h0f_ref[0]
        hb_s[...] = h0b_ref[0]

    masks_b = _head_masks(A_CHUNK, BF16)
    nc = tt // A_CHUNK
    fwd_refs = (rtf_ref, phf_ref, dlf_ref, oif_ref, of_ref)
    bwd_refs = (rtb_ref, phb_ref, dlb_ref, oib_ref, ob_ref)

    n_pairs = D_MODEL // PAIR

    def pair(j):
        return slice(j * PAIR, (j + 1) * PAIR)

    hs = [hf_s[:, pair(j)] for j in range(n_pairs)] + [hb_s[:, pair(j)] for j in range(n_pairs)]
    for c in range(nc):
        chains = [(fwd_refs, c, j) for j in range(n_pairs)] + [(bwd_refs, nc - 1 - c, j) for j in range(n_pairs)]
        boths = []
        for h, (refs, cc, j) in zip(hs, chains):
            sl = slice(cc * A_CHUNK, (cc + 1) * A_CHUNK)
            lhs = jnp.concatenate([refs[0][0, sl, pair(j)], refs[1][0, sl, pair(j)]], axis=0)
            boths.append(_dot(lhs, _bd2(h.astype(BF16), masks_b)))
        new_hs = []
        for both, (refs, cc, j) in zip(boths, chains):
            sl = slice(cc * A_CHUNK, (cc + 1) * A_CHUNK)
            refs[4][0, sl, pair(j)] = (both[0:A_CHUNK] + refs[3][0, sl, pair(j)].astype(F32)).astype(refs[4].dtype)
            new_hs.append(both[A_CHUNK:] + refs[2][0, sl, pair(j)].astype(F32))
        hs = new_hs
    for j in range(n_pairs):
        hf_s[:, pair(j)] = hs[j]
        hb_s[:, pair(j)] = hs[n_pairs + j]
        hff_ref[0, :, pair(j)] = hs[j]
        hfb_ref[0, :, pair(j)] = hs[n_pairs + j]


def _rwkv_seq(par, h0f, h0b, tt):
    rtf, oif, phf, dlf, rtb, oib, phb, dlb = par
    b, l, _ = rtf.shape
    n_tiles = l // tt
    fwd = pl.BlockSpec((1, tt, D_MODEL), lambda bi, i: (bi, i, 0))
    bwd = pl.BlockSpec((1, tt, D_MODEL), lambda bi, i: (bi, n_tiles - 1 - i, 0))
    st = pl.BlockSpec((1, A_HEAD, D_MODEL), lambda bi, i: (bi, 0, 0))
    o_sd = jax.ShapeDtypeStruct((b, l, D_MODEL), BF16)
    h_sd = jax.ShapeDtypeStruct((b, A_HEAD, D_MODEL), F32)
    return pl.pallas_call(
        functools.partial(_rwkv_seq_kernel, tt=tt),
        grid=(b, n_tiles),
        in_specs=[fwd] * 4 + [bwd] * 4 + [st, st],
        out_specs=[fwd, bwd, st, st],
        out_shape=[o_sd, o_sd, h_sd, h_sd],
        scratch_shapes=[pltpu.VMEM((A_HEAD, D_MODEL), F32), pltpu.VMEM((A_HEAD, D_MODEL), F32)],
        compiler_params=_cparams(("arbitrary", "arbitrary")),
        name="rwkv_seq",
    )(rtf, oif, phf, dlf, rtb, oib, phb, dlb, h0f, h0b)


def _ret_kernel(lg_ref, q_ref, k_ref, v_ref, rc_ref, rs_ref, cc_ref, cs_ref, s0_ref, *rest,
                tt, reverse, with_intra, rotate, n_heads, n_tiles):
    rest = list(rest)
    prev_ref = rest.pop(0) if with_intra else None
    o_ref, sf_ref = rest[0:2]
    qr_ref, kr_ref = rest[2:4] if rotate else (None, None)
    s_s = rest[-1]
    bh = pl.program_id(0)
    i = pl.program_id(1)
    h = bh % n_heads

    @pl.when(i == 0)
    def _():
        s_s[...] = s0_ref[0]

    lg_f = lg_ref[0, h]
    lg_b = lg_ref[1, h]
    lg = lg_b if reverse else lg_f

    n_rows = tt // GRID_W
    r0 = (n_tiles - 1 - i if reverse else i) * n_rows

    def by_row(ref):
        return jnp.concatenate([jnp.broadcast_to(ref[pl.ds(r0 + j, 1), :], (GRID_W, 128)) for j in range(n_rows)], axis=0)

    def by_col(ref):
        return jnp.concatenate([ref[...]] * n_rows, axis=0)

    cos_r, sin_r, cos_c, sin_c = by_row(rc_ref), by_row(rs_ref), by_col(cc_ref), by_col(cs_ref)

    def rope(x_ref):
        x = x_ref[0].astype(F32)
        xa, xb = x[:, 0:128], x[:, 128:256]
        return jnp.concatenate([xa * cos_r + pltpu.roll(xa, 64, axis=1) * sin_r,
                                xb * cos_c + pltpu.roll(xb, 64, axis=1) * sin_c], axis=1)

    if not rotate:
        q = q_ref[0].astype(F32)
        k = k_ref[0].astype(F32)
    else:
        q = rope(q_ref)
        k = rope(k_ref) * (R_QK ** -0.5)
        qr_ref[0] = q.astype(qr_ref.dtype)
        kr_ref[0] = k.astype(kr_ref.dtype)

    idx = lax.broadcasted_iota(jnp.int32, (R_BLOCK, R_QK), 0).astype(F32)
    if reverse:
        q_dec = jnp.exp(lg * (R_BLOCK - idx))
        k_dec = jnp.exp(lg * idx)
    else:
        q_dec = jnp.exp(lg * (idx + 1.0))
        k_dec = jnp.exp(lg * (R_BLOCK - 1.0 - idx))
    c_dec = jnp.exp(lg * R_BLOCK)

    nc = tt // R_BLOCK
    order = [nc - 1 - cc if reverse else cc for cc in range(nc)]

    def blk(c):
        return slice(c * R_BLOCK, (c + 1) * R_BLOCK)

    if with_intra:
        di = lax.broadcasted_iota(jnp.int32, (R_BLOCK, R_BLOCK), 0)
        dj = lax.broadcasted_iota(jnp.int32, (R_BLOCK, R_BLOCK), 1)
        diff = (di - dj).astype(F32)
        dmask = (jnp.where(diff >= 0, jnp.exp(lg_f * jnp.maximum(diff, 0.0)), 0.0)
                 + jnp.where(diff <= 0, jnp.exp(lg_b * jnp.maximum(-diff, 0.0)), 0.0))
        scores = {c: _dot_nt(q[blk(c)].astype(BF16), k[blk(c)].astype(BF16)) for c in order}
    kvs = {c: _dot_tn((k[blk(c)] * k_dec).astype(BF16), v_ref[0, blk(c), :]) for c in order}

    s = s_s[...]
    inter = {}
    for c in order:
        inter[c] = _dot((q[blk(c)] * q_dec).astype(BF16), s.astype(BF16))
        s = s * c_dec + kvs[c]
    s_s[...] = s
    sf_ref[0] = s
    for c in order:
        o = inter[c]
        if with_intra:
            o = o + _dot((scores[c] * dmask).astype(BF16), v_ref[0, blk(c), :]) + prev_ref[0, blk(c), :].astype(F32)
        o_ref[0, blk(c), :] = o.astype(o_ref.dtype)


def _ret_pass(p, lg, tabs, s0, prev, tt, reverse, rotated=None):
    with_intra = prev is not None
    rotate = rotated is None
    b, l, _ = p.shape
    n_tiles = l // tt
    nh = R_HEADS

    def tile(i):
        return n_tiles - 1 - i if reverse else i

    def col(width, off):
        return pl.BlockSpec((1, tt, width), lambda bh, i, lg_: (bh // nh, tile(i), off // width + bh % nh))

    def whole(a):
        return pl.BlockSpec(a.shape, lambda bh, i, lg_: (0, 0))

    st = pl.BlockSpec((1, R_QK, R_V), lambda bh, i, lg_: (bh, 0, 0))
    o_spec = pl.BlockSpec((1, tt, R_V), lambda bh, i, lg_: (bh // nh, tile(i), bh % nh))
    rot_spec = pl.BlockSpec((1, tt, R_QK), lambda bh, i, lg_: (bh // nh, tile(i), bh % nh))
    rot_sd = jax.ShapeDtypeStruct((b, l, nh * R_QK), BF16)
    qk_specs = [col(R_QK, OFF_Q), col(R_QK, OFF_K)] if rotate else [rot_spec, rot_spec]
    qk_args = [p, p] if rotate else list(rotated)
    kern = functools.partial(_ret_kernel, tt=tt, reverse=reverse, with_intra=with_intra, rotate=rotate, n_heads=nh,
                             n_tiles=n_tiles)
    grid_spec = pltpu.PrefetchScalarGridSpec(
        num_scalar_prefetch=1,
        grid=(b * nh, n_tiles),
        in_specs=qk_specs + [col(R_V, OFF_V)] + [whole(t) for t in tabs] + [st]
        + ([o_spec] if with_intra else []),
        out_specs=[o_spec, st] + ([rot_spec, rot_spec] if rotate else []),
        scratch_shapes=[pltpu.VMEM((R_QK, R_V), F32)],
    )
    return pl.pallas_call(
        kern,
        grid_spec=grid_spec,
        out_shape=[jax.ShapeDtypeStruct((b, l, nh * R_V), BF16),
                   jax.ShapeDtypeStruct((b * nh, R_QK, R_V), F32)] + ([rot_sd, rot_sd] if rotate else []),
        compiler_params=_cparams(("arbitrary", "arbitrary")),
        name="ret_bwd" if reverse else "ret_fwd",
    )(lg, *qk_args, p, *tabs, s0, *([prev] if with_intra else []))


def _ret_ctx_kernel(lg_ref, k_ref, v_ref, sf_ref, sb_ref, *, n_heads):
    h = pl.program_id(0) % n_heads
    k = k_ref[0].astype(F32) * (R_QK ** -0.5)
    v = v_ref[0]
    n = k.shape[0]
    idx = lax.broadcasted_iota(jnp.int32, k.shape, 0).astype(F32)
    sf_ref[0] = _dot_tn((k * jnp.exp(lg_ref[0, h] * (n - 1.0 - idx))).astype(BF16), v)
    sb_ref[0] = _dot_tn((k * jnp.exp(lg_ref[1, h] * idx)).astype(BF16), v)


def _ret_ctx_states(p, lg):
    b, l, _ = p.shape
    nh = R_HEADS
    st = pl.BlockSpec((1, R_QK, R_V), lambda bh, lg_: (bh, 0, 0))
    sd = jax.ShapeDtypeStruct((b * nh, R_QK, R_V), F32)
    grid_spec = pltpu.PrefetchScalarGridSpec(
        num_scalar_prefetch=1,
        grid=(b * nh,),
        in_specs=[pl.BlockSpec((1, l, R_QK), lambda bh, lg_: (bh // nh, 0, OFF_K // R_QK + bh % nh)),
                  pl.BlockSpec((1, l, R_V), lambda bh, lg_: (bh // nh, 0, OFF_V // R_V + bh % nh))],
        out_specs=[st, st],
    )
    return pl.pallas_call(
        functools.partial(_ret_ctx_kernel, n_heads=nh),
        grid_spec=grid_spec,
        out_shape=[sd, sd],
        compiler_params=_cparams(("arbitrary",)),
        name="ret_ctx",
    )(lg, p, p)


def _final_kernel(x_ref, ga_ref, gr_ref, ma_ref, mb_ref, of_ref, ob_ref, bon_ref, or_ref,
                  gate_ref, alw_ref, alb_ref, rlw_ref, rlb_ref, fnw_ref, awo_ref, rwo_ref, wo_ref, o_ref):
    li = lax.broadcasted_iota(jnp.int32, (QUAD, QUAD), 0) // A_HEAD
    lj = lax.broadcasted_iota(jnp.int32, (QUAD, QUAD), 1) // A_HEAD
    msum = (li == lj).astype(BF16)
    inv = 1.0 / A_HEAD

    def head_sum(t):
        return jnp.concatenate([_dot(t[:, q * QUAD:(q + 1) * QUAD].astype(BF16), msum) for q in range(N_QUADS)], axis=1)

    oa = of_ref[0].astype(F32) + ob_ref[0].astype(F32)
    mu = head_sum(oa) * inv
    dv = oa - mu
    var = head_sum(dv * dv) * inv
    ya = dv * lax.rsqrt(var + A_GN_EPS) * alw_ref[...] + alb_ref[...] + bon_ref[0].astype(F32)
    ya = _dot((ya * _silu_half(ga_ref[0].astype(F32))).astype(BF16), awo_ref[...])

    orr = or_ref[0].astype(F32)
    parts = []
    for h in range(R_HEADS):
        oh = orr[:, h * R_V:(h + 1) * R_V]
        m = jnp.mean(oh, axis=-1, keepdims=True)
        dh = oh - m
        vh = jnp.mean(dh * dh, axis=-1, keepdims=True)
        parts.append(dh * lax.rsqrt(vh + R_GN_EPS))
    yr = jnp.concatenate(parts, axis=1) * rlw_ref[...] + rlb_ref[...]
    yr = _dot((yr * _silu_half(gr_ref[0].astype(F32))).astype(BF16), rwo_ref[...])

    merged = (1.0 + jnp.tanh(ma_ref[0].astype(F32))) * ya + (1.0 + jnp.tanh(mb_ref[0].astype(F32))) * yr
    y = _dot(merged.astype(BF16), wo_ref[...])
    xo = x_ref[0] + gate_ref[0] * y
    o_ref[0] = xo * lax.rsqrt(jnp.mean(xo * xo, axis=-1, keepdims=True) + NORM_EPS) * fnw_ref[...]


def _final(x, p, of, ob, bonus, o_ret, gate, a_ln_w, a_ln_b, r_ln_w, r_ln_b, final_w, awo, rwo, wo, tm):
    b, l, d = x.shape

    def tok(width, off=0):
        return pl.BlockSpec((1, tm, width), lambda bi, i: (bi, i, off // width))

    def vec(width):
        return pl.BlockSpec((1, width), lambda bi, i: (0, 0))

    def mat(r, c):
        return pl.BlockSpec((r, c), lambda bi, i: (0, 0))

    rv = R_HEADS * R_V
    return pl.pallas_call(
        _final_kernel,
        grid=(b, l // tm),
        in_specs=[tok(d), tok(d, OFF_GA), tok(rv, OFF_GR), tok(d, OFF_MA), tok(d, OFF_MB),
                  tok(d), tok(d), tok(d), tok(rv),
                  pl.BlockSpec((1, 1, d), lambda bi, i: (bi, 0, 0)),
                  vec(d), vec(d), vec(rv), vec(rv), vec(d), mat(d, d), mat(rv, d), mat(d, d)],
        out_specs=tok(d),
        out_shape=jax.ShapeDtypeStruct((b, l, d), F32),
        compiler_params=_cparams(("arbitrary", "arbitrary")),
        name="final",
    )(x, p, p, p, p, of, ob, bonus, o_ret, gate,
      a_ln_w.reshape(1, d), a_ln_b.reshape(1, d), r_ln_w.reshape(1, rv), r_ln_b.reshape(1, rv),
      final_w.reshape(1, d), awo, rwo, wo)


def _rope_tables(l):
    half = R_QK // 4
    freqs = ROPE_BASE ** (-jnp.arange(half, dtype=F32) / half)

    def tab(n):
        ang = jnp.arange(n, dtype=F32)[:, None] * freqs[None, :]
        return (jnp.concatenate([jnp.cos(ang), jnp.cos(ang)], axis=1),
                jnp.concatenate([-jnp.sin(ang), jnp.sin(ang)], axis=1))

    return tab(l // GRID_W) + tab(GRID_W)


def _reorder_w_in(w):
    return jnp.concatenate([w[:, 0:3072], 0.5 * w[:, 3072:4096], w[:, 4352:8448], 0.5 * w[:, 8448:], w[:, 4096:4352]],
                           axis=1)


def _pick_tile(l, pref):
    t = min(l, pref)
    while l % t:
        t //= 2
    return t


def kernel(x, c, ctx, c_ctx, norm_w, ada_w, ada_b, w_in, a_conv, a_w_up, a_w0, a_a_up, a_a0, a_k_k, a_k_a,
           a_r_k, a_ln_w, a_ln_b, a_w_out, r_decay, r_ln_w, r_ln_b, r_w_out, w_out, final_norm_w):
    b, l, d = x.shape
    lc = ctx.shape[1]
    assert d == D_MODEL and l % R_BLOCK == 0 and lc % R_BLOCK == 0
    lyr = 0

    cc = jnp.zeros((16, d), F32).at[:b].set(c).at[b].set(c_ctx)
    mod = _adaln(cc, ada_w[lyr], ada_b[lyr])
    shift, scale, gate = mod[:b, :d], mod[:b, d:2 * d], mod[:b, 2 * d:]
    shift_c = jnp.broadcast_to(mod[b, :d], (b, d))
    scale_c = jnp.broadcast_to(mod[b, d:2 * d], (b, d))

    w_bf = _reorder_w_in(w_in[lyr]).astype(BF16)
    p_x = _inproj(x, (1.0 + scale)[:, None, :], shift[:, None, :], norm_w[lyr], w_bf, _pick_tile(l, 2048))
    p_c = _inproj(ctx.reshape(1, b * lc, d), (1.0 + scale_c)[:1, None, :], shift_c[:1, None, :], norm_w[lyr], w_bf,
                  _pick_tile(b * lc, 2048)).reshape(b, lc, -1)

    zpad = jnp.zeros((A_LORA, d), F32)
    wup_pad = jnp.stack([jnp.concatenate([a_w_up[lyr, 0], zpad], 0), jnp.concatenate([zpad, a_w_up[lyr, 1]], 0)])
    aup_pad = jnp.stack([jnp.concatenate([a_a_up[lyr, 0], zpad], 0), jnp.concatenate([zpad, a_a_up[lyr, 1]], 0)])
    a_args = (a_conv[lyr], a_k_k[lyr], a_k_a[lyr], a_r_k[lyr].reshape(-1), a_w0[lyr], a_a0[lyr], wup_pad, aup_pad)
    h_zero = jnp.zeros((b, A_HEAD, d), F32)
    par_c = _rwkv_par(p_c, *a_args, tt=_pick_tile(lc, 256))
    _, _, hcf, hcb = _rwkv_seq(par_c[:8], h_zero, h_zero, _pick_tile(lc, 512))
    par_x = _rwkv_par(p_x, *a_args, tt=_pick_tile(l, 1024))
    o_af, o_ab, _, _ = _rwkv_seq(par_x[:8], hcf, hcb, _pick_tile(l, 512))
    bonus = par_x[8]

    lg = -jnp.exp(r_decay[lyr].astype(F32))
    tabs_x = _rope_tables(l)
    sc_f, sc_b = _ret_ctx_states(p_c, lg)
    tx = _pick_tile(l, 2048)
    o_rb, _, q_rot, k_rot = _ret_pass(p_x, lg, tabs_x, sc_b, None, tx, True)
    o_r, _ = _ret_pass(p_x, lg, tabs_x, sc_f, o_rb, tx, False, (q_rot, k_rot))

    return _final(x, p_x, o_af, o_ab, bonus, o_r, gate[:, None, :], a_ln_w[lyr], a_ln_b[lyr],
                  r_ln_w[lyr], r_ln_b[lyr], final_norm_w, (0.5 * a_w_out[lyr]).astype(BF16),
                  (0.5 * r_w_out[lyr]).astype(BF16), w_out[lyr].astype(BF16), _pick_tile(l, 512))
```

```python
import functools
import math

import jax
import jax.numpy as jnp
from jax import lax
from jax.experimental import pallas as pl
from jax.experimental.pallas import tpu as pltpu

F32 = jnp.float32
BF16 = jnp.bfloat16

D_MODEL = 1024
A_HEAD = 64
A_HEADS = D_MODEL // A_HEAD
A_LORA = 64
QUAD = 4 * A_HEAD
PAIR = 2 * A_HEAD
N_QUADS = D_MODEL // QUAD
A_CHUNK = 64
SUB = 8
CUM_BLOCK = 256
R_HEADS = 4
R_QK = 256
R_V = 512
R_BLOCK = 256
GRID_W = 64
ROPE_BASE = 10000.0
NORM_EPS = 1e-6
A_GN_EPS = 64e-5
R_GN_EPS = 1e-5
L2_EPS = 1e-12
DECAY_SCALE = math.exp(-0.5)

OFF_RKV, OFF_GA, OFF_Q, OFF_K, OFF_V, OFF_GR, OFF_MA, OFF_MB, OFF_LO = (
    0, 3072, 4096, 5120, 6144, 8192, 10240, 11264, 12288)
V7X_VMEM_LIMIT = 56 * 1024 * 1024


def _cparams(sem):
    return pltpu.CompilerParams(dimension_semantics=sem, vmem_limit_bytes=V7X_VMEM_LIMIT)


def _dot(a, b):
    return jnp.dot(a, b, preferred_element_type=F32)


def _dot_nt(a, b):
    return lax.dot_general(a, b, (((1,), (1,)), ((), ())), preferred_element_type=F32)


def _dot_tn(a, b):
    return lax.dot_general(a, b, (((0,), (0,)), ((), ())), preferred_element_type=F32)


def _split(x):
    hi = x.astype(BF16)
    lo = (x - hi.astype(F32)).astype(BF16)
    return hi, lo


def _dot_exact_lhs(m, x):
    hi, lo = _split(x)
    return _dot(m, hi) + _dot(m, lo)


def _dot3(a, b):
    ah, al = _split(a)
    bh, bl = _split(b)
    return _dot(ah, bh) + _dot(ah, bl) + _dot(al, bh)


def _sigmoid(x):
    return 0.5 * jnp.tanh(0.5 * x) + 0.5


def _silu(x):
    return x * _sigmoid(x)


def _silu_half(h):
    return h + h * jnp.tanh(h)


def _adaln_kernel(c_ref, w_ref, b_ref, o_ref):
    cs = _silu(c_ref[...])
    o_ref[...] = _dot3(cs, w_ref[...]) + b_ref[...]


def _adaln(cc, ada_w, ada_b):
    n = ada_w.shape[1]
    tn = 512
    return pl.pallas_call(
        _adaln_kernel,
        grid=(n // tn,),
        in_specs=[pl.BlockSpec(cc.shape, lambda j: (0, 0)),
                  pl.BlockSpec((ada_w.shape[0], tn), lambda j: (0, j)),
                  pl.BlockSpec((1, tn), lambda j: (0, j))],
        out_specs=pl.BlockSpec((cc.shape[0], tn), lambda j: (0, j)),
        out_shape=jax.ShapeDtypeStruct((cc.shape[0], n), F32),
        compiler_params=_cparams(("arbitrary",)),
        name="adaln",
    )(cc, ada_w, ada_b.reshape(1, n))


def _inproj_kernel(x_ref, sc_ref, sh_ref, nw_ref, w_ref, o_ref, u_ref):
    @pl.when(pl.program_id(2) == 0)
    def _():
        xf = x_ref[0]
        y = xf * lax.rsqrt(jnp.mean(xf * xf, axis=-1, keepdims=True) + NORM_EPS) * nw_ref[...]
        u_ref[...] = (y * sc_ref[0] + sh_ref[0]).astype(BF16)

    o_ref[0] = _dot(u_ref[...], w_ref[...]).astype(o_ref.dtype)


def _inproj(x, scale1p, shift, norm_w, w_bf16, tm):
    b, l, d = x.shape
    n = w_bf16.shape[1]
    tn = 1792
    return pl.pallas_call(
        _inproj_kernel,
        grid=(b, l // tm, n // tn),
        in_specs=[pl.BlockSpec((1, tm, d), lambda bi, i, j: (bi, i, 0)),
                  pl.BlockSpec((1, 1, d), lambda bi, i, j: (bi, 0, 0)),
                  pl.BlockSpec((1, 1, d), lambda bi, i, j: (bi, 0, 0)),
                  pl.BlockSpec((1, d), lambda bi, i, j: (0, 0)),
                  pl.BlockSpec((d, tn), lambda bi, i, j: (0, j))],
        out_specs=pl.BlockSpec((1, tm, tn), lambda bi, i, j: (bi, i, j)),
        out_shape=jax.ShapeDtypeStruct((b, l, n), BF16),
        scratch_shapes=[pltpu.VMEM((tm, d), BF16)],
        compiler_params=_cparams(("arbitrary", "arbitrary", "arbitrary")),
        name="inproj",
    )(x, scale1p, shift, norm_w.reshape(1, d), w_bf16)


def _head_masks(rows, dtype):
    lane_head = lax.broadcasted_iota(jnp.int32, (rows, PAIR), 1) // A_HEAD
    return [(lane_head == h).astype(dtype) for h in range(2)]


def _bd2(x_bf16, masks):
    return jnp.concatenate([x_bf16 * m for m in masks], axis=0)


def _rwkv_par_kernel(r_ref, k_ref, v_ref, rp_ref, kp_ref, vp_ref, rn_ref, kn_ref, vn_ref, lo_ref,
                     cr_ref, ck_ref, cv_ref, kkw_ref, kaw_ref, rkw_ref, w0_ref, a0_ref, wup_ref, aup_ref,
                     rtf_ref, oif_ref, phf_ref, dlf_ref, rtb_ref, oib_ref, phb_ref, dlb_ref, bon_ref,
                     *, tt, n_tiles):
    i = pl.program_id(1)
    row = lax.broadcasted_iota(jnp.int32, (tt, QUAD), 0)
    has_prev = (i > 0).astype(F32)
    has_next = (i < n_tiles - 1).astype(F32)

    def conv(x_ref, p_ref, n_ref, c_ref):
        x = x_ref[0].astype(F32)
        prev_row = p_ref[0][15:16, :].astype(F32) * has_prev
        next_row = n_ref[0][0:1, :].astype(F32) * has_next
        x_prev = jnp.where(row == 0, prev_row, pltpu.roll(x, 1, axis=0))
        x_next = jnp.where(row == tt - 1, next_row, pltpu.roll(x, tt - 1, axis=0))
        cw = c_ref[...]
        return x_prev * cw[0:1] + x * cw[1:2] + x_next * cw[2:3]

    r = conv(r_ref, rp_ref, rn_ref, cr_ref)
    k = conv(k_ref, kp_ref, kn_ref, ck_ref)
    v = conv(v_ref, vp_ref, vn_ref, cv_ref)

    li = lax.broadcasted_iota(jnp.int32, (QUAD, QUAD), 0) // A_HEAD
    lj = lax.broadcasted_iota(jnp.int32, (QUAD, QUAD), 1) // A_HEAD
    msum = (li == lj).astype(BF16)

    kk = k * kkw_ref[...]
    kk = kk * lax.rsqrt(_dot((kk * kk).astype(BF16), msum) + L2_EPS)

    cb = min(tt, CUM_BLOCK)
    ti = lax.broadcasted_iota(jnp.int32, (cb, cb), 0)
    tj = lax.broadcasted_iota(jnp.int32, (cb, cb), 1)
    same_chunk = (ti // A_CHUNK) == (tj // A_CHUNK)

    lo = lo_ref[0].astype(F32)
    lact = jnp.tanh(lo[:, 0:2 * A_LORA])
    la = lo[:, 2 * A_LORA:4 * A_LORA]

    masks_b = _head_masks(A_CHUNK, BF16)
    ct = lax.broadcasted_iota(jnp.int32, (A_CHUNK, PAIR), 0)
    cs = lax.broadcasted_iota(jnp.int32, (A_CHUNK, PAIR), 1) % A_HEAD
    eye_q = (ct == cs).astype(F32)
    lane_c = lax.broadcasted_iota(jnp.int32, (SUB, PAIR), 1)
    blk_of_lane = (lane_c % A_HEAD) // SUB
    blk_f = [(blk_of_lane == b_).astype(F32) for b_ in range(A_CHUNK // SUB)]
    head_blk_b = [((lane_c // A_HEAD == h_) & (blk_of_lane == b_)).astype(BF16)
                  for h_ in range(2) for b_ in range(A_CHUNK // SUB)]
    eye_c = (lax.broadcasted_iota(jnp.int32, (SUB, PAIR), 0) == lane_c % SUB).astype(F32)
    merges_b = [(((ct // (2 * w)) == (cs // (2 * w))) & ((ct // w) != (cs // w))).astype(BF16)
                for w in (SUB, 2 * SUB, 4 * SUB)]

    second_head = (lax.broadcasted_iota(jnp.int32, (tt, QUAD), 1) // A_HEAD) % 2 == 1

    def transposed(x):
        xb = x.astype(BF16)
        ahead = jnp.concatenate([xb[tt - A_CHUNK:], xb[:tt - A_CHUNK]], axis=0)
        return jnp.where(second_head, ahead, xb).T

    bonus = jnp.zeros((tt, QUAD), F32)
    outs = ((rtf_ref, oif_ref, phf_ref, dlf_ref), (rtb_ref, oib_ref, phb_ref, dlb_ref))
    prep = []
    for d in range(2):
        if d == 0:
            tri = (same_chunk & (tj <= ti)).astype(BF16)
            strict = (cs < ct).astype(F32)
            incl = (cs <= ct).astype(F32)
        else:
            tri = (same_chunk & (tj >= ti)).astype(BF16)
            strict = (cs > ct).astype(F32)
            incl = (cs >= ct).astype(F32)
        z = w0_ref[d:d + 1, :] + _dot3(lact, wup_ref[d])
        logw = -DECAY_SCALE * _sigmoid(z)
        av = _sigmoid(a0_ref[d:d + 1, :] + _dot3(la, aup_ref[d]))
        kd = k * (1.0 + (av - 1.0) * kaw_ref[...])
        bonus = bonus + _dot((r * kd * rkw_ref[...]).astype(BF16), msum) * v

        cum = jnp.concatenate([_dot_exact_lhs(tri, logw[j * cb:(j + 1) * cb]) for j in range(tt // cb)], axis=0)
        last = 0 if d == 1 else A_CHUNK - 1
        ctot = jnp.concatenate([jnp.broadcast_to(cum[c * A_CHUNK + last:c * A_CHUNK + last + 1, :], (A_CHUNK, QUAD))
                                for c in range(tt // A_CHUNK)], axis=0)
        e_inv = jnp.exp(-cum)
        e_g = jnp.exp(ctot - cum)
        kka = kk * av
        prep.append(dict(kh=kk * jnp.exp(cum - logw), rh=r * jnp.exp(cum), gam=jnp.exp(ctot),
                         bt_t=transposed(kka * e_inv), kt_t=transposed(kd * e_inv),
                         bg_t=transposed(kka * e_g), kg_t=transposed(kd * e_g),
                         strict_b=strict.astype(BF16), incl_b=incl.astype(BF16),
                         diag=[strict[b_ * SUB:(b_ + 1) * SUB] * blk_f[b_] for b_ in range(A_CHUNK // SUB)]))

    units = [(d, c, pr) for d in range(2) for c in range(tt // A_CHUNK) for pr in range(QUAD // PAIR)]

    def cut(x, c, pr):
        return x[c * A_CHUNK:(c + 1) * A_CHUNK, pr * PAIR:(pr + 1) * PAIR]

    n_cols = tt // PAIR

    def head_masks(c):
        return masks_b if c % 2 == 0 else masks_b[::-1]

    def bd(x, c):
        return _bd2(x.astype(BF16), head_masks(c))

    def cut_t(xt, c, pr):
        m_top, m_bot = head_masks(c)
        col_top = c // 2
        col_bot = c // 2 if c % 2 == 0 else ((c + 1) // 2) % n_cols
        top = xt[pr * PAIR:pr * PAIR + A_HEAD, col_top * PAIR:(col_top + 1) * PAIR] * m_top
        bot = xt[pr * PAIR + A_HEAD:(pr + 1) * PAIR, col_bot * PAIR:(col_bot + 1) * PAIR] * m_bot
        return jnp.concatenate([top, bot], axis=0), top + bot

    def stage_a(d, c, pr):
        pd = prep[d]
        kh, rh = cut(pd["kh"], c, pr), cut(pd["rh"], c, pr)
        lhs1 = jnp.concatenate([kh, rh], axis=0).astype(BF16)
        w_bt, _ = cut_t(pd["bt_t"], c, pr)
        w_kt, _ = cut_t(pd["kt_t"], c, pr)
        a_all = _dot(lhs1, jnp.concatenate([w_bt, w_kt], axis=1))
        a_b = a_all.astype(BF16)
        diag = a_all[0:SUB, 0:PAIR] * pd["diag"][0]
        for b_ in range(1, A_CHUNK // SUB):
            diag = diag + a_all[b_ * SUB:(b_ + 1) * SUB, 0:PAIR] * pd["diag"][b_]
        return dict(kh=kh, rh=rh, diag=diag,
                    a_ab=a_b[0:A_CHUNK, 0:PAIR] * pd["strict_b"],
                    a_krk=jnp.concatenate([a_b[0:A_CHUNK, PAIR:] * pd["strict_b"],
                                           a_b[A_CHUNK:, PAIR:] * pd["incl_b"],
                                           cut_t(pd["kg_t"], c, pr)[1]], axis=0),
                    a_rbb=jnp.concatenate([a_b[A_CHUNK:, 0:PAIR] * pd["incl_b"],
                                           cut_t(pd["bg_t"], c, pr)[1]], axis=0))

    st = [stage_a(*u_) for u_ in units]

    def expand_w(xc):
        xb = xc.astype(BF16)
        return jnp.concatenate([xb * m for m in head_blk_b], axis=0)

    ps = [-s_["diag"] for s_ in st]
    ts = [eye_c + p for p in ps]
    ps = [_dot(p.astype(BF16), expand_w(p)) for p in ps]
    both = [_dot(jnp.concatenate([p, t], axis=0).astype(BF16), expand_w(p)) for p, t in zip(ps, ts)]
    ps = [b_[0:SUB] for b_ in both]
    ts = [t + b_[SUB:] for t, b_ in zip(ts, both)]
    ts = [t + _dot(t.astype(BF16), expand_w(p)) for t, p in zip(ts, ps)]
    ts = [jnp.concatenate([t * m for m in blk_f], axis=0).astype(BF16) for t in ts]
    for blk in merges_b:
        gs = [_dot(t, _bd2(s_["a_ab"] * blk, masks_b)).astype(BF16) for t, s_ in zip(ts, st)]
        ts = [t - _dot(g, _bd2(t, masks_b)).astype(BF16) for t, g in zip(ts, gs)]

    xas = [_dot(s_["a_krk"], bd(cut(v, c, pr), c)) for s_, (d, c, pr) in zip(st, units)]
    wus = [_dot(t, jnp.concatenate([bd(s_["kh"], c), bd(xa[0:A_CHUNK], c)], axis=1))
           for t, s_, xa, (d, c, pr) in zip(ts, st, xas, units)]
    wubs = [wu.astype(BF16) for wu in wus]
    arbs = [_dot(s_["a_rbb"], jnp.concatenate([_bd2(wb[:, 0:PAIR], head_masks(c)),
                                               _bd2(wb[:, PAIR:], head_masks(c))], axis=1))
            for s_, wb, (d, c, pr) in zip(st, wubs, units)]

    for (d, c, pr), s_, xa, arb in zip(units, st, xas, arbs):
        rt_ref, oi_ref, ph_ref, dl_ref = outs[d]
        gam = cut(prep[d]["gam"], c, pr)
        rw = slice(c * A_CHUNK, (c + 1) * A_CHUNK)
        ln = slice(pr * PAIR, (pr + 1) * PAIR)
        rt_ref[0, rw, ln] = (s_["rh"] - arb[0:A_CHUNK, 0:PAIR]).astype(rt_ref.dtype)
        oi_ref[0, rw, ln] = (xa[A_CHUNK:2 * A_CHUNK] - arb[0:A_CHUNK, PAIR:]).astype(oi_ref.dtype)
        ph_ref[0, rw, ln] = (eye_q * gam - arb[A_CHUNK:, 0:PAIR]).astype(ph_ref.dtype)
        dl_ref[0, rw, ln] = (xa[2 * A_CHUNK:] - arb[A_CHUNK:, PAIR:]).astype(dl_ref.dtype)
    bon_ref[0] = bonus.astype(bon_ref.dtype)


def _rwkv_par(p, a_conv, k_k, k_a, r_k, w0, a0, wup_pad, aup_pad, tt):
    b, l, _ = p.shape
    n_tiles = l // tt
    h16 = tt // 16
    last16 = l // 16 - 1

    def tok(col0):
        return pl.BlockSpec((1, tt, QUAD), lambda bi, i, q: (bi, i, col0 // QUAD + q))

    def prev(col0):
        return pl.BlockSpec((1, 16, QUAD), lambda bi, i, q: (bi, jnp.maximum(i * h16 - 1, 0), col0 // QUAD + q))

    def nxt(col0):
        return pl.BlockSpec((1, 16, QUAD),
                            lambda bi, i, q: (bi, jnp.minimum((i + 1) * h16, last16), col0 // QUAD + q))

    def vec(rows, col0=0):
        return pl.BlockSpec((rows, QUAD), lambda bi, i, q: (0, col0 // QUAD + q))

    out_tok = pl.BlockSpec((1, tt, QUAD), lambda bi, i, q: (bi, i, q))
    out_sd = jax.ShapeDtypeStruct((b, l, D_MODEL), BF16)
    kern = functools.partial(_rwkv_par_kernel, tt=tt, n_tiles=n_tiles)
    return pl.pallas_call(
        kern,
        grid=(b, n_tiles, N_QUADS),
        in_specs=[tok(0), tok(1024), tok(2048), prev(0), prev(1024), prev(2048),
                  nxt(0), nxt(1024), nxt(2048),
                  pl.BlockSpec((1, tt, 4 * A_LORA), lambda bi, i, q: (bi, i, OFF_LO // (4 * A_LORA))),
                  vec(3, 0), vec(3, 1024), vec(3, 2048),
                  vec(1), vec(1), vec(1), vec(2), vec(2),
                  pl.BlockSpec((2, 2 * A_LORA, QUAD), lambda bi, i, q: (0, 0, q)),
                  pl.BlockSpec((2, 2 * A_LORA, QUAD), lambda bi, i, q: (0, 0, q))],
        out_specs=[out_tok] * 9,
        out_shape=[out_sd] * 9,
        compiler_params=_cparams(("arbitrary", "arbitrary", "arbitrary")),
        name="rwkv_par",
    )(p, p, p, p, p, p, p, p, p, p, a_conv, a_conv, a_conv,
      k_k.reshape(1, D_MODEL), k_a.reshape(1, D_MODEL), r_k.reshape(1, D_MODEL), w0, a0, wup_pad, aup_pad)


def _rwkv_seq_kernel(rtf_ref, oif_ref, phf_ref, dlf_ref, rtb_ref, oib_ref, phb_ref, dlb_ref,
                     h0f_ref, h0b_ref, of_ref, ob_ref, hff_ref, hfb_ref, hf_s, hb_s, *, tt):
    i = pl.program_id(1)

    @pl.when(i == 0)
    def _():
        hf_s[...] = h0f_ref[0]
        hb_s[...] = h0b_ref[0]

    masks_b = _head_masks(A_CHUNK, BF16)
    nc = tt // A_CHUNK
    fwd_refs = (rtf_ref, phf_ref, dlf_ref, oif_ref, of_ref)
    bwd_refs = (rtb_ref, phb_ref, dlb_ref, oib_ref, ob_ref)

    n_pairs = D_MODEL // PAIR

    def pair(j):
        return slice(j * PAIR, (j + 1) * PAIR)

    hs = [hf_s[:, pair(j)] for j in range(n_pairs)] + [hb_s[:, pair(j)] for j in range(n_pairs)]
    for c in range(nc):
        chains = [(fwd_refs, c, j) for j in range(n_pairs)] + [(bwd_refs, nc - 1 - c, j) for j in range(n_pairs)]
        boths = []
        for h, (refs, cc, j) in zip(hs, chains):
            sl = slice(cc * A_CHUNK, (cc + 1) * A_CHUNK)
            lhs = jnp.concatenate([refs[0][0, sl, pair(j)], refs[1][0, sl, pair(j)]], axis=0)
            boths.append(_dot(lhs, _bd2(h.astype(BF16), masks_b)))
        new_hs = []
        for both, (refs, cc, j) in zip(boths, chains):
            sl = slice(cc * A_CHUNK, (cc + 1) * A_CHUNK)
            refs[4][0, sl, pair(j)] = (both[0:A_CHUNK] + refs[3][0, sl, pair(j)].astype(F32)).astype(refs[4].dtype)
            new_hs.append(both[A_CHUNK:] + refs[2][0, sl, pair(j)].astype(F32))
        hs = new_hs
    for j in range(n_pairs):
        hf_s[:, pair(j)] = hs[j]
        hb_s[:, pair(j)] = hs[n_pairs + j]
        hff_ref[0, :, pair(j)] = hs[j]
        hfb_ref[0, :, pair(j)] = hs[n_pairs + j]


def _rwkv_seq(par, h0f, h0b, tt):
    rtf, oif, phf, dlf, rtb, oib, phb, dlb = par
    b, l, _ = rtf.shape
    n_tiles = l // tt
    fwd = pl.BlockSpec((1, tt, D_MODEL), lambda bi, i: (bi, i, 0))
    bwd = pl.BlockSpec((1, tt, D_MODEL), lambda bi, i: (bi, n_tiles - 1 - i, 0))
    st = pl.BlockSpec((1, A_HEAD, D_MODEL), lambda bi, i: (bi, 0, 0))
    o_sd = jax.ShapeDtypeStruct((b, l, D_MODEL), BF16)
    h_sd = jax.ShapeDtypeStruct((b, A_HEAD, D_MODEL), F32)
    return pl.pallas_call(
        functools.partial(_rwkv_seq_kernel, tt=tt),
        grid=(b, n_tiles),
        in_specs=[fwd] * 4 + [bwd] * 4 + [st, st],
        out_specs=[fwd, bwd, st, st],
        out_shape=[o_sd, o_sd, h_sd, h_sd],
        scratch_shapes=[pltpu.VMEM((A_HEAD, D_MODEL), F32), pltpu.VMEM((A_HEAD, D_MODEL), F32)],
        compiler_params=_cparams(("arbitrary", "arbitrary")),
        name="rwkv_seq",
    )(rtf, oif, phf, dlf, rtb, oib, phb, dlb, h0f, h0b)


def _ret_kernel(lg_ref, q_ref, k_ref, v_ref, rc_ref, rs_ref, cc_ref, cs_ref, s0_ref, *rest,
                tt, reverse, with_intra, rotate, n_heads, n_tiles):
    rest = list(rest)
    prev_ref = rest.pop(0) if with_intra else None
    o_ref, sf_ref = rest[0:2]
    qr_ref, kr_ref = rest[2:4] if rotate else (None, None)
    s_s = rest[-1]
    bh = pl.program_id(0)
    i = pl.program_id(1)
    h = bh % n_heads

    @pl.when(i == 0)
    def _():
        s_s[...] = s0_ref[0]

    lg_f = lg_ref[0, h]
    lg_b = lg_ref[1, h]
    lg = lg_b if reverse else lg_f

    n_rows = tt // GRID_W
    r0 = (n_tiles - 1 - i if reverse else i) * n_rows

    def by_row(ref):
        return jnp.concatenate([jnp.broadcast_to(ref[pl.ds(r0 + j, 1), :], (GRID_W, 128)) for j in range(n_rows)], axis=0)

    def by_col(ref):
        return jnp.concatenate([ref[...]] * n_rows, axis=0)

    cos_r, sin_r, cos_c, sin_c = by_row(rc_ref), by_row(rs_ref), by_col(cc_ref), by_col(cs_ref)

    def rope(x_ref):
        x = x_ref[0].astype(F32)
        xa, xb = x[:, 0:128], x[:, 128:256]
        return jnp.concatenate([xa * cos_r + pltpu.roll(xa, 64, axis=1) * sin_r,
                                xb * cos_c + pltpu.roll(xb, 64, axis=1) * sin_c], axis=1)

    if not rotate:
        q = q_ref[0].astype(F32)
        k = k_ref[0].astype(F32)
    else:
        q = rope(q_ref)
        k = rope(k_ref) * (R_QK ** -0.5)
        qr_ref[0] = q.astype(qr_ref.dtype)
        kr_ref[0] = k.astype(kr_ref.dtype)

    idx = lax.broadcasted_iota(jnp.int32, (R_BLOCK, R_QK), 0).astype(F32)
    if reverse:
        q_dec = jnp.exp(lg * (R_BLOCK - idx))
        k_dec = jnp.exp(lg * idx)
    else:
        q_dec = jnp.exp(lg * (idx + 1.0))
        k_dec = jnp.exp(lg * (R_BLOCK - 1.0 - idx))
    c_dec = jnp.exp(lg * R_BLOCK)

    nc = tt // R_BLOCK
    order = [nc - 1 - cc if reverse else cc for cc in range(nc)]

    def blk(c):
        return slice(c * R_BLOCK, (c + 1) * R_BLOCK)

    if with_intra:
        di = lax.broadcasted_iota(jnp.int32, (R_BLOCK, R_BLOCK), 0)
        dj = lax.broadcasted_iota(jnp.int32, (R_BLOCK, R_BLOCK), 1)
        diff = (di - dj).astype(F32)
        dmask = (jnp.where(diff >= 0, jnp.exp(lg_f * jnp.maximum(diff, 0.0)), 0.0)
                 + jnp.where(diff <= 0, jnp.exp(lg_b * jnp.maximum(-diff, 0.0)), 0.0))
        scores = {c: _dot_nt(q[blk(c)].astype(BF16), k[blk(c)].astype(BF16)) for c in order}
    kvs = {c: _dot_tn((k[blk(c)] * k_dec).astype(BF16), v_ref[0, blk(c), :]) for c in order}

    s = s_s[...]
    inter = {}
    for c in order:
        inter[c] = _dot((q[blk(c)] * q_dec).astype(BF16), s.astype(BF16))
        s = s * c_dec + kvs[c]
    s_s[...] = s
    sf_ref[0] = s
    for c in order:
        o = inter[c]
        if with_intra:
            o = o + _dot((scores[c] * dmask).astype(BF16), v_ref[0, blk(c), :]) + prev_ref[0, blk(c), :].astype(F32)
        o_ref[0, blk(c), :] = o.astype(o_ref.dtype)


def _ret_pass(p, lg, tabs, s0, prev, tt, reverse, rotated=None):
    with_intra = prev is not None
    rotate = rotated is None
    b, l, _ = p.shape
    n_tiles = l // tt
    nh = R_HEADS

    def tile(i):
        return n_tiles - 1 - i if reverse else i

    def col(width, off):
        return pl.BlockSpec((1, tt, width), lambda bh, i, lg_: (bh // nh, tile(i), off // width + bh % nh))

    def whole(a):
        return pl.BlockSpec(a.shape, lambda bh, i, lg_: (0, 0))

    st = pl.BlockSpec((1, R_QK, R_V), lambda bh, i, lg_: (bh, 0, 0))
    o_spec = pl.BlockSpec((1, tt, R_V), lambda bh, i, lg_: (bh // nh, tile(i), bh % nh))
    rot_spec = pl.BlockSpec((1, tt, R_QK), lambda bh, i, lg_: (bh // nh, tile(i), bh % nh))
    rot_sd = jax.ShapeDtypeStruct((b, l, nh * R_QK), BF16)
    qk_specs = [col(R_QK, OFF_Q), col(R_QK, OFF_K)] if rotate else [rot_spec, rot_spec]
    qk_args = [p, p] if rotate else list(rotated)
    kern = functools.partial(_ret_kernel, tt=tt, reverse=reverse, with_intra=with_intra, rotate=rotate, n_heads=nh,
                             n_tiles=n_tiles)
    grid_spec = pltpu.PrefetchScalarGridSpec(
        num_scalar_prefetch=1,
        grid=(b * nh, n_tiles),
        in_specs=qk_specs + [col(R_V, OFF_V)] + [whole(t) for t in tabs] + [st]
        + ([o_spec] if with_intra else []),
        out_specs=[o_spec, st] + ([rot_spec, rot_spec] if rotate else []),
        scratch_shapes=[pltpu.VMEM((R_QK, R_V), F32)],
    )
    return pl.pallas_call(
        kern,
        grid_spec=grid_spec,
        out_shape=[jax.ShapeDtypeStruct((b, l, nh * R_V), BF16),
                   jax.ShapeDtypeStruct((b * nh, R_QK, R_V), F32)] + ([rot_sd, rot_sd] if rotate else []),
        compiler_params=_cparams(("arbitrary", "arbitrary")),
        name="ret_bwd" if reverse else "ret_fwd",
    )(lg, *qk_args, p, *tabs, s0, *([prev] if with_intra else []))


def _ret_ctx_kernel(lg_ref, k_ref, v_ref, sf_ref, sb_ref, *, n_heads):
    h = pl.program_id(0) % n_heads
    k = k_ref[0].astype(F32) * (R_QK ** -0.5)
    v = v_ref[0]
    n = k.shape[0]
    idx = lax.broadcasted_iota(jnp.int32, k.shape, 0).astype(F32)
    sf_ref[0] = _dot_tn((k * jnp.exp(lg_ref[0, h] * (n - 1.0 - idx))).astype(BF16), v)
    sb_ref[0] = _dot_tn((k * jnp.exp(lg_ref[1, h] * idx)).astype(BF16), v)


def _ret_ctx_states(p, lg):
    b, l, _ = p.shape
    nh = R_HEADS
    st = pl.BlockSpec((1, R_QK, R_V), lambda bh, lg_: (bh, 0, 0))
    sd = jax.ShapeDtypeStruct((b * nh, R_QK, R_V), F32)
    grid_spec = pltpu.PrefetchScalarGridSpec(
        num_scalar_prefetch=1,
        grid=(b * nh,),
        in_specs=[pl.BlockSpec((1, l, R_QK), lambda bh, lg_: (bh // nh, 0, OFF_K // R_QK + bh % nh)),
                  pl.BlockSpec((1, l, R_V), lambda bh, lg_: (bh // nh, 0, OFF_V // R_V + bh % nh))],
        out_specs=[st, st],
    )
    return pl.pallas_call(
        functools.partial(_ret_ctx_kernel, n_heads=nh),
        grid_spec=grid_spec,
        out_shape=[sd, sd],
        compiler_params=_cparams(("arbitrary",)),
        name="ret_ctx",
    )(lg, p, p)


def _final_kernel(x_ref, ga_ref, gr_ref, ma_ref, mb_ref, of_ref, ob_ref, bon_ref, or_ref,
                  gate_ref, alw_ref, alb_ref, rlw_ref, rlb_ref, fnw_ref, awo_ref, rwo_ref, wo_ref, o_ref):
    li = lax.broadcasted_iota(jnp.int32, (QUAD, QUAD), 0) // A_HEAD
    lj = lax.broadcasted_iota(jnp.int32, (QUAD, QUAD), 1) // A_HEAD
    msum = (li == lj).astype(BF16)
    inv = 1.0 / A_HEAD

    def head_sum(t):
        return jnp.concatenate([_dot(t[:, q * QUAD:(q + 1) * QUAD].astype(BF16), msum) for q in range(N_QUADS)], axis=1)

    oa = of_ref[0].astype(F32) + ob_ref[0].astype(F32)
    mu = head_sum(oa) * inv
    dv = oa - mu
    var = head_sum(dv * dv) * inv
    ya = dv * lax.rsqrt(var + A_GN_EPS) * alw_ref[...] + alb_ref[...] + bon_ref[0].astype(F32)
    ya = _dot((ya * _silu_half(ga_ref[0].astype(F32))).astype(BF16), awo_ref[...])

    orr = or_ref[0].astype(F32)
    parts = []
    for h in range(R_HEADS):
        oh = orr[:, h * R_V:(h + 1) * R_V]
        m = jnp.mean(oh, axis=-1, keepdims=True)
        dh = oh - m
        vh = jnp.mean(dh * dh, axis=-1, keepdims=True)
        parts.append(dh * lax.rsqrt(vh + R_GN_EPS))
    yr = jnp.concatenate(parts, axis=1) * rlw_ref[...] + rlb_ref[...]
    yr = _dot((yr * _silu_half(gr_ref[0].astype(F32))).astype(BF16), rwo_ref[...])

    merged = (1.0 + jnp.tanh(ma_ref[0].astype(F32))) * ya + (1.0 + jnp.tanh(mb_ref[0].astype(F32))) * yr
    y = _dot(merged.astype(BF16), wo_ref[...])
    xo = x_ref[0] + gate_ref[0] * y
    o_ref[0] = xo * lax.rsqrt(jnp.mean(xo * xo, axis=-1, keepdims=True) + NORM_EPS) * fnw_ref[...]


def _final(x, p, of, ob, bonus, o_ret, gate, a_ln_w, a_ln_b, r_ln_w, r_ln_b, final_w, awo, rwo, wo, tm):
    b, l, d = x.shape

    def tok(width, off=0):
        return pl.BlockSpec((1, tm, width), lambda bi, i: (bi, i, off // width))

    def vec(width):
        return pl.BlockSpec((1, width), lambda bi, i: (0, 0))

    def mat(r, c):
        return pl.BlockSpec((r, c), lambda bi, i: (0, 0))

    rv = R_HEADS * R_V
    return pl.pallas_call(
        _final_kernel,
        grid=(b, l // tm),
        in_specs=[tok(d), tok(d, OFF_GA), tok(rv, OFF_GR), tok(d, OFF_MA), tok(d, OFF_MB),
                  tok(d), tok(d), tok(d), tok(rv),
                  pl.BlockSpec((1, 1, d), lambda bi, i: (bi, 0, 0)),
                  vec(d), vec(d), vec(rv), vec(rv), vec(d), mat(d, d), mat(rv, d), mat(d, d)],
        out_specs=tok(d),
        out_shape=jax.ShapeDtypeStruct((b, l, d), F32),
        compiler_params=_cparams(("arbitrary", "arbitrary")),
        name="final",
    )(x, p, p, p, p, of, ob, bonus, o_ret, gate,
      a_ln_w.reshape(1, d), a_ln_b.reshape(1, d), r_ln_w.reshape(1, rv), r_ln_b.reshape(1, rv),
      final_w.reshape(1, d), awo, rwo, wo)


def _rope_tables(l):
    half = R_QK // 4
    freqs = ROPE_BASE ** (-jnp.arange(half, dtype=F32) / half)

    def tab(n):
        ang = jnp.arange(n, dtype=F32)[:, None] * freqs[None, :]
        return (jnp.concatenate([jnp.cos(ang), jnp.cos(ang)], axis=1),
                jnp.concatenate([-jnp.sin(ang), jnp.sin(ang)], axis=1))

    return tab(l // GRID_W) + tab(GRID_W)


def _reorder_w_in(w):
    return jnp.concatenate([w[:, 0:3072], 0.5 * w[:, 3072:4096], w[:, 4352:8448], 0.5 * w[:, 8448:], w[:, 4096:4352]],
                           axis=1)


def _pick_tile(l, pref):
    t = min(l, pref)
    while l % t:
        t //= 2
    return t


def kernel(x, c, ctx, c_ctx, norm_w, ada_w, ada_b, w_in, a_conv, a_w_up, a_w0, a_a_up, a_a0, a_k_k, a_k_a,
           a_r_k, a_ln_w, a_ln_b, a_w_out, r_decay, r_ln_w, r_ln_b, r_w_out, w_out, final_norm_w):
    b, l, d = x.shape
    lc = ctx.shape[1]
    assert d == D_MODEL and l % R_BLOCK == 0 and lc % R_BLOCK == 0
    lyr = 0

    cc = jnp.zeros((16, d), F32).at[:b].set(c).at[b].set(c_ctx)
    mod = _adaln(cc, ada_w[lyr], ada_b[lyr])
    shift, scale, gate = mod[:b, :d], mod[:b, d:2 * d], mod[:b, 2 * d:]
    shift_c = jnp.broadcast_to(mod[b, :d], (b, d))
    scale_c = jnp.broadcast_to(mod[b, d:2 * d], (b, d))

    w_bf = _reorder_w_in(w_in[lyr]).astype(BF16)
    p_x = _inproj(x, (1.0 + scale)[:, None, :], shift[:, None, :], norm_w[lyr], w_bf, _pick_tile(l, 2048))
    p_c = _inproj(ctx.reshape(1, b * lc, d), (1.0 + scale_c)[:1, None, :], shift_c[:1, None, :], norm_w[lyr], w_bf,
                  _pick_tile(b * lc, 2048)).reshape(b, lc, -1)

    zpad = jnp.zeros((A_LORA, d), F32)
    wup_pad = jnp.stack([jnp.concatenate([a_w_up[lyr, 0], zpad], 0), jnp.concatenate([zpad, a_w_up[lyr, 1]], 0)])
    aup_pad = jnp.stack([jnp.concatenate([a_a_up[lyr, 0], zpad], 0), jnp.concatenate([zpad, a_a_up[lyr, 1]], 0)])
    a_args = (a_conv[lyr], a_k_k[lyr], a_k_a[lyr], a_r_k[lyr].reshape(-1), a_w0[lyr], a_a0[lyr], wup_pad, aup_pad)
    h_zero = jnp.zeros((b, A_HEAD, d), F32)
    par_c = _rwkv_par(p_c, *a_args, tt=_pick_tile(lc, 256))
    _, _, hcf, hcb = _rwkv_seq(par_c[:8], h_zero, h_zero, _pick_tile(lc, 512))
    par_x = _rwkv_par(p_x, *a_args, tt=_pick_tile(l, 1024))
    o_af, o_ab, _, _ = _rwkv_seq(par_x[:8], hcf, hcb, _pick_tile(l, 512))
    bonus = par_x[8]

    lg = -jnp.exp(r_decay[lyr].astype(F32))
    tabs_x = _rope_tables(l)
    sc_f, sc_b = _ret_ctx_states(p_c, lg)
    tx = _pick_tile(l, 2048)
    o_rb, _, q_rot, k_rot = _ret_pass(p_x, lg, tabs_x, sc_b, None, tx, True)
    o_r, _ = _ret_pass(p_x, lg, tabs_x, sc_f, o_rb, tx, False, (q_rot, k_rot))

    return _final(x, p_x, o_af, o_ab, bonus, o_r, gate[:, None, :], a_ln_w[lyr], a_ln_b[lyr],
                  r_ln_w[lyr], r_ln_b[lyr], final_norm_w, (0.5 * a_w_out[lyr]).astype(BF16),
                  (0.5 * r_w_out[lyr]).astype(BF16), w_out[lyr].astype(BF16), _pick_tile(l, 512))
```

```python
import functools
import math

import jax
import jax.numpy as jnp
from jax import lax
from jax.experimental import pallas as pl
from jax.experimental.pallas import tpu as pltpu

F32 = jnp.float32
BF16 = jnp.bfloat16

D_MODEL = 1024
A_HEAD = 64
A_HEADS = D_MODEL // A_HEAD
A_LORA = 64
QUAD = 4 * A_HEAD
PAIR = 2 * A_HEAD
N_QUADS = D_MODEL // QUAD
A_CHUNK = 64
SUB = 8
CUM_BLOCK = 256
UNIT_GROUP = 16
R_HEADS = 4
R_QK = 256
R_V = 512
R_BLOCK = 256
GRID_W = 64
ROPE_BASE = 10000.0
NORM_EPS = 1e-6
A_GN_EPS = 64e-5
R_GN_EPS = 1e-5
L2_EPS = 1e-12
DECAY_SCALE = math.exp(-0.5)

OFF_RKV, OFF_GA, OFF_Q, OFF_K, OFF_V, OFF_GR, OFF_MA, OFF_MB, OFF_LO = (
    0, 3072, 4096, 5120, 6144, 8192, 10240, 11264, 12288)
V7X_VMEM_LIMIT = 56 * 1024 * 1024


def _cparams(sem):
    return pltpu.CompilerParams(dimension_semantics=sem, vmem_limit_bytes=V7X_VMEM_LIMIT)


def _dot(a, b):
    return jnp.dot(a, b, preferred_element_type=F32)


def _dot_nt(a, b):
    return lax.dot_general(a, b, (((1,), (1,)), ((), ())), preferred_element_type=F32)


def _dot_tn(a, b):
    return lax.dot_general(a, b, (((0,), (0,)), ((), ())), preferred_element_type=F32)


def _split(x):
    hi = x.astype(BF16)
    lo = (x - hi.astype(F32)).astype(BF16)
    return hi, lo


def _dot_exact_lhs(m, x):
    hi, lo = _split(x)
    return _dot(m, hi) + _dot(m, lo)


def _dot3(a, b):
    ah, al = _split(a)
    bh, bl = _split(b)
    return _dot(ah, bh) + _dot(ah, bl) + _dot(al, bh)


def _sigmoid(x):
    return 0.5 * jnp.tanh(0.5 * x) + 0.5


def _silu(x):
    return x * _sigmoid(x)


def _silu_half(h):
    return h + h * jnp.tanh(h)


def _adaln_kernel(c_ref, w_ref, b_ref, o_ref):
    cs = _silu(c_ref[...])
    o_ref[...] = _dot3(cs, w_ref[...]) + b_ref[...]


def _adaln(cc, ada_w, ada_b):
    n = ada_w.shape[1]
    tn = 512
    return pl.pallas_call(
        _adaln_kernel,
        grid=(n // tn,),
        in_specs=[pl.BlockSpec(cc.shape, lambda j: (0, 0)),
                  pl.BlockSpec((ada_w.shape[0], tn), lambda j: (0, j)),
                  pl.BlockSpec((1, tn), lambda j: (0, j))],
        out_specs=pl.BlockSpec((cc.shape[0], tn), lambda j: (0, j)),
        out_shape=jax.ShapeDtypeStruct((cc.shape[0], n), F32),
        compiler_params=_cparams(("arbitrary",)),
        name="adaln",
    )(cc, ada_w, ada_b.reshape(1, n))


def _inproj_kernel(x_ref, sc_ref, sh_ref, nw_ref, w_ref, o_ref, u_ref):
    @pl.when(pl.program_id(2) == 0)
    def _():
        xf = x_ref[0]
        y = xf * lax.rsqrt(jnp.mean(xf * xf, axis=-1, keepdims=True) + NORM_EPS) * nw_ref[...]
        u_ref[...] = (y * sc_ref[0] + sh_ref[0]).astype(BF16)

    o_ref[0] = _dot(u_ref[...], w_ref[...]).astype(o_ref.dtype)


def _inproj(x, scale1p, shift, norm_w, w_bf16, tm):
    b, l, d = x.shape
    n = w_bf16.shape[1]
    tn = 1792
    return pl.pallas_call(
        _inproj_kernel,
        grid=(b, l // tm, n // tn),
        in_specs=[pl.BlockSpec((1, tm, d), lambda bi, i, j: (bi, i, 0)),
                  pl.BlockSpec((1, 1, d), lambda bi, i, j: (bi, 0, 0)),
                  pl.BlockSpec((1, 1, d), lambda bi, i, j: (bi, 0, 0)),
                  pl.BlockSpec((1, d), lambda bi, i, j: (0, 0)),
                  pl.BlockSpec((d, tn), lambda bi, i, j: (0, j))],
        out_specs=pl.BlockSpec((1, tm, tn), lambda bi, i, j: (bi, i, j)),
        out_shape=jax.ShapeDtypeStruct((b, l, n), BF16),
        scratch_shapes=[pltpu.VMEM((tm, d), BF16)],
        compiler_params=_cparams(("arbitrary", "arbitrary", "arbitrary")),
        name="inproj",
    )(x, scale1p, shift, norm_w.reshape(1, d), w_bf16)


def _head_masks(rows, dtype):
    lane_head = lax.broadcasted_iota(jnp.int32, (rows, PAIR), 1) // A_HEAD
    return [(lane_head == h).astype(dtype) for h in range(2)]


def _bd2(x_bf16, masks):
    return jnp.concatenate([x_bf16 * m for m in masks], axis=0)


def _rwkv_par_kernel(r_ref, k_ref, v_ref, rp_ref, kp_ref, vp_ref, rn_ref, kn_ref, vn_ref, lo_ref,
                     cr_ref, ck_ref, cv_ref, kkw_ref, kaw_ref, rkw_ref, w0_ref, a0_ref, wup_ref, aup_ref,
                     rtf_ref, oif_ref, phf_ref, dlf_ref, rtb_ref, oib_ref, phb_ref, dlb_ref, bon_ref,
                     *, tt, n_tiles):
    i = pl.program_id(1)
    row = lax.broadcasted_iota(jnp.int32, (tt, QUAD), 0)
    has_prev = (i > 0).astype(F32)
    has_next = (i < n_tiles - 1).astype(F32)

    def conv(x_ref, p_ref, n_ref, c_ref):
        x = x_ref[0].astype(F32)
        prev_row = p_ref[0][15:16, :].astype(F32) * has_prev
        next_row = n_ref[0][0:1, :].astype(F32) * has_next
        x_prev = jnp.where(row == 0, prev_row, pltpu.roll(x, 1, axis=0))
        x_next = jnp.where(row == tt - 1, next_row, pltpu.roll(x, tt - 1, axis=0))
        cw = c_ref[...]
        return x_prev * cw[0:1] + x * cw[1:2] + x_next * cw[2:3]

    r = conv(r_ref, rp_ref, rn_ref, cr_ref)
    k = conv(k_ref, kp_ref, kn_ref, ck_ref)
    v = conv(v_ref, vp_ref, vn_ref, cv_ref)

    li = lax.broadcasted_iota(jnp.int32, (QUAD, QUAD), 0) // A_HEAD
    lj = lax.broadcasted_iota(jnp.int32, (QUAD, QUAD), 1) // A_HEAD
    msum = (li == lj).astype(BF16)

    kk = k * kkw_ref[...]
    kk = kk * lax.rsqrt(_dot((kk * kk).astype(BF16), msum) + L2_EPS)

    cb = min(tt, CUM_BLOCK)
    ti = lax.broadcasted_iota(jnp.int32, (cb, cb), 0)
    tj = lax.broadcasted_iota(jnp.int32, (cb, cb), 1)
    same_chunk = (ti // A_CHUNK) == (tj // A_CHUNK)

    lo = lo_ref[0].astype(F32)
    lact = jnp.tanh(lo[:, 0:2 * A_LORA])
    la = lo[:, 2 * A_LORA:4 * A_LORA]

    masks_b = _head_masks(A_CHUNK, BF16)
    ct = lax.broadcasted_iota(jnp.int32, (A_CHUNK, PAIR), 0)
    cs = lax.broadcasted_iota(jnp.int32, (A_CHUNK, PAIR), 1) % A_HEAD
    eye_q = (ct == cs).astype(F32)
    lane_c = lax.broadcasted_iota(jnp.int32, (SUB, PAIR), 1)
    blk_of_lane = (lane_c % A_HEAD) // SUB
    blk_f = [(blk_of_lane == b_).astype(F32) for b_ in range(A_CHUNK // SUB)]
    head_blk_b = [((lane_c // A_HEAD == h_) & (blk_of_lane == b_)).astype(BF16)
                  for h_ in range(2) for b_ in range(A_CHUNK // SUB)]
    eye_c = (lax.broadcasted_iota(jnp.int32, (SUB, PAIR), 0) == lane_c % SUB).astype(F32)
    merges_b = [(((ct // (2 * w)) == (cs // (2 * w))) & ((ct // w) != (cs // w))).astype(BF16)
                for w in (SUB, 2 * SUB, 4 * SUB)]

    second_head = (lax.broadcasted_iota(jnp.int32, (tt, QUAD), 1) // A_HEAD) % 2 == 1

    def transposed(x):
        xb = x.astype(BF16)
        ahead = jnp.concatenate([xb[tt - A_CHUNK:], xb[:tt - A_CHUNK]], axis=0)
        return jnp.where(second_head, ahead, xb).T

    bonus = jnp.zeros((tt, QUAD), F32)
    outs = ((rtf_ref, oif_ref, phf_ref, dlf_ref), (rtb_ref, oib_ref, phb_ref, dlb_ref))
    prep = []
    for d in range(2):
        if d == 0:
            tri = (same_chunk & (tj <= ti)).astype(BF16)
            strict = (cs < ct).astype(F32)
            incl = (cs <= ct).astype(F32)
        else:
            tri = (same_chunk & (tj >= ti)).astype(BF16)
            strict = (cs > ct).astype(F32)
            incl = (cs >= ct).astype(F32)
        z = w0_ref[d:d + 1, :] + _dot3(lact, wup_ref[d])
        logw = -DECAY_SCALE * _sigmoid(z)
        av = _sigmoid(a0_ref[d:d + 1, :] + _dot3(la, aup_ref[d]))
        kd = k * (1.0 + (av - 1.0) * kaw_ref[...])
        bonus = bonus + _dot((r * kd * rkw_ref[...]).astype(BF16), msum) * v

        cum = jnp.concatenate([_dot_exact_lhs(tri, logw[j * cb:(j + 1) * cb]) for j in range(tt // cb)], axis=0)
        last = 0 if d == 1 else A_CHUNK - 1
        ctot = jnp.concatenate([jnp.broadcast_to(cum[c * A_CHUNK + last:c * A_CHUNK + last + 1, :], (A_CHUNK, QUAD))
                                for c in range(tt // A_CHUNK)], axis=0)
        e_inv = jnp.exp(-cum)
        e_g = jnp.exp(ctot - cum)
        kka = kk * av
        prep.append(dict(kh=kk * jnp.exp(cum - logw), rh=r * jnp.exp(cum), gam=jnp.exp(ctot),
                         bt_t=transposed(kka * e_inv), kt_t=transposed(kd * e_inv),
                         bg_t=transposed(kka * e_g), kg_t=transposed(kd * e_g),
                         strict_b=strict.astype(BF16), incl_b=incl.astype(BF16),
                         diag=[strict[b_ * SUB:(b_ + 1) * SUB] * blk_f[b_] for b_ in range(A_CHUNK // SUB)]))

    all_units = [(d, c, pr) for d in range(2) for c in range(tt // A_CHUNK) for pr in range(QUAD // PAIR)]

    def cut(x, c, pr):
        return x[c * A_CHUNK:(c + 1) * A_CHUNK, pr * PAIR:(pr + 1) * PAIR]

    n_cols = tt // PAIR

    def head_masks(c):
        return masks_b if c % 2 == 0 else masks_b[::-1]

    def bd(x, c):
        return _bd2(x.astype(BF16), head_masks(c))

    def cut_t(xt, c, pr):
        m_top, m_bot = head_masks(c)
        col_top = c // 2
        col_bot = c // 2 if c % 2 == 0 else ((c + 1) // 2) % n_cols
        top = xt[pr * PAIR:pr * PAIR + A_HEAD, col_top * PAIR:(col_top + 1) * PAIR] * m_top
        bot = xt[pr * PAIR + A_HEAD:(pr + 1) * PAIR, col_bot * PAIR:(col_bot + 1) * PAIR] * m_bot
        return jnp.concatenate([top, bot], axis=0), top + bot

    def stage_a(d, c, pr):
        pd = prep[d]
        kh, rh = cut(pd["kh"], c, pr), cut(pd["rh"], c, pr)
        lhs1 = jnp.concatenate([kh, rh], axis=0).astype(BF16)
        w_bt, _ = cut_t(pd["bt_t"], c, pr)
        w_kt, _ = cut_t(pd["kt_t"], c, pr)
        a_all = _dot(lhs1, jnp.concatenate([w_bt, w_kt], axis=1))
        a_b = a_all.astype(BF16)
        diag = a_all[0:SUB, 0:PAIR] * pd["diag"][0]
        for b_ in range(1, A_CHUNK // SUB):
            diag = diag + a_all[b_ * SUB:(b_ + 1) * SUB, 0:PAIR] * pd["diag"][b_]
        return dict(kh=kh, rh=rh, diag=diag,
                    a_ab=a_b[0:A_CHUNK, 0:PAIR] * pd["strict_b"],
                    a_krk=jnp.concatenate([a_b[0:A_CHUNK, PAIR:] * pd["strict_b"],
                                           a_b[A_CHUNK:, PAIR:] * pd["incl_b"],
                                           cut_t(pd["kg_t"], c, pr)[1]], axis=0),
                    a_rbb=jnp.concatenate([a_b[A_CHUNK:, 0:PAIR] * pd["incl_b"],
                                           cut_t(pd["bg_t"], c, pr)[1]], axis=0))

    def run_group(units):
        st = [stage_a(*u_) for u_ in units]

        def expand_w(xc):
            xb = xc.astype(BF16)
            return jnp.concatenate([xb * m for m in head_blk_b], axis=0)

        ps = [-s_["diag"] for s_ in st]
        ts = [eye_c + p for p in ps]
        ps = [_dot(p.astype(BF16), expand_w(p)) for p in ps]
        both = [_dot(jnp.concatenate([p, t], axis=0).astype(BF16), expand_w(p)) for p, t in zip(ps, ts)]
        ps = [b_[0:SUB] for b_ in both]
        ts = [t + b_[SUB:] for t, b_ in zip(ts, both)]
        ts = [t + _dot(t.astype(BF16), expand_w(p)) for t, p in zip(ts, ps)]
        ts = [jnp.concatenate([t * m for m in blk_f], axis=0).astype(BF16) for t in ts]
        for blk in merges_b:
            gs = [_dot(t, _bd2(s_["a_ab"] * blk, masks_b)).astype(BF16) for t, s_ in zip(ts, st)]
            ts = [t - _dot(g, _bd2(t, masks_b)).astype(BF16) for t, g in zip(ts, gs)]

        xas = [_dot(s_["a_krk"], bd(cut(v, c, pr), c)) for s_, (d, c, pr) in zip(st, units)]
        wus = [_dot(t, jnp.concatenate([bd(s_["kh"], c), bd(xa[0:A_CHUNK], c)], axis=1))
               for t, s_, xa, (d, c, pr) in zip(ts, st, xas, units)]
        wubs = [wu.astype(BF16) for wu in wus]
        arbs = [_dot(s_["a_rbb"], jnp.concatenate([_bd2(wb[:, 0:PAIR], head_masks(c)),
                                                   _bd2(wb[:, PAIR:], head_masks(c))], axis=1))
                for s_, wb, (d, c, pr) in zip(st, wubs, units)]

        for (d, c, pr), s_, xa, arb in zip(units, st, xas, arbs):
            rt_ref, oi_ref, ph_ref, dl_ref = outs[d]
            gam = cut(prep[d]["gam"], c, pr)
            rw = slice(c * A_CHUNK, (c + 1) * A_CHUNK)
            ln = slice(pr * PAIR, (pr + 1) * PAIR)
            rt_ref[0, rw, ln] = (s_["rh"] - arb[0:A_CHUNK, 0:PAIR]).astype(rt_ref.dtype)
            oi_ref[0, rw, ln] = (xa[A_CHUNK:2 * A_CHUNK] - arb[0:A_CHUNK, PAIR:]).astype(oi_ref.dtype)
            ph_ref[0, rw, ln] = (eye_q * gam - arb[A_CHUNK:, 0:PAIR]).astype(ph_ref.dtype)
            dl_ref[0, rw, ln] = (xa[2 * A_CHUNK:] - arb[A_CHUNK:, PAIR:]).astype(dl_ref.dtype)

    for g_ in range(0, len(all_units), UNIT_GROUP):
        run_group(all_units[g_:g_ + UNIT_GROUP])
    bon_ref[0] = bonus.astype(bon_ref.dtype)


def _rwkv_par(p, a_conv, k_k, k_a, r_k, w0, a0, wup_pad, aup_pad, tt):
    b, l, _ = p.shape
    n_tiles = l // tt
    h16 = tt // 16
    last16 = l // 16 - 1

    def tok(col0):
        return pl.BlockSpec((1, tt, QUAD), lambda bi, i, q: (bi, i, col0 // QUAD + q))

    def prev(col0):
        return pl.BlockSpec((1, 16, QUAD), lambda bi, i, q: (bi, jnp.maximum(i * h16 - 1, 0), col0 // QUAD + q))

    def nxt(col0):
        return pl.BlockSpec((1, 16, QUAD),
                            lambda bi, i, q: (bi, jnp.minimum((i + 1) * h16, last16), col0 // QUAD + q))

    def vec(rows, col0=0):
        return pl.BlockSpec((rows, QUAD), lambda bi, i, q: (0, col0 // QUAD + q))

    out_tok = pl.BlockSpec((1, tt, QUAD), lambda bi, i, q: (bi, i, q))
    out_sd = jax.ShapeDtypeStruct((b, l, D_MODEL), BF16)
    kern = functools.partial(_rwkv_par_kernel, tt=tt, n_tiles=n_tiles)
    return pl.pallas_call(
        kern,
        grid=(b, n_tiles, N_QUADS),
        in_specs=[tok(0), tok(1024), tok(2048), prev(0), prev(1024), prev(2048),
                  nxt(0), nxt(1024), nxt(2048),
                  pl.BlockSpec((1, tt, 4 * A_LORA), lambda bi, i, q: (bi, i, OFF_LO // (4 * A_LORA))),
                  vec(3, 0), vec(3, 1024), vec(3, 2048),
                  vec(1), vec(1), vec(1), vec(2), vec(2),
                  pl.BlockSpec((2, 2 * A_LORA, QUAD), lambda bi, i, q: (0, 0, q)),
                  pl.BlockSpec((2, 2 * A_LORA, QUAD), lambda bi, i, q: (0, 0, q))],
        out_specs=[out_tok] * 9,
        out_shape=[out_sd] * 9,
        compiler_params=_cparams(("arbitrary", "arbitrary", "arbitrary")),
        name="rwkv_par",
    )(p, p, p, p, p, p, p, p, p, p, a_conv, a_conv, a_conv,
      k_k.reshape(1, D_MODEL), k_a.reshape(1, D_MODEL), r_k.reshape(1, D_MODEL), w0, a0, wup_pad, aup_pad)


def _rwkv_seq_kernel(rtf_ref, oif_ref, phf_ref, dlf_ref, rtb_ref, oib_ref, phb_ref, dlb_ref,
                     h0f_ref, h0b_ref, of_ref, ob_ref, hff_ref, hfb_ref, hf_s, hb_s, *, tt):
    i = pl.program_id(1)

    @pl.when(i == 0)
    def _():
        hf_s[...] = h0f_ref[0]
        hb_s[...] = h0b_ref[0]

    masks_b = _head_masks(A_CHUNK, BF16)
    nc = tt // A_CHUNK
    fwd_refs = (rtf_ref, phf_ref, dlf_ref, oif_ref, of_ref)
    bwd_refs = (rtb_ref, phb_ref, dlb_ref, oib_ref, ob_ref)

    n_pairs = D_MODEL // PAIR

    def pair(j):
        return slice(j * PAIR, (j + 1) * PAIR)

    hs = [hf_s[:, pair(j)] for j in range(n_pairs)] + [hb_s[:, pair(j)] for j in range(n_pairs)]
    for c in range(nc):
        chains = [(fwd_refs, c, j) for j in range(n_pairs)] + [(bwd_refs, nc - 1 - c, j) for j in range(n_pairs)]
        boths = []
        for h, (refs, cc, j) in zip(hs, chains):
            sl = slice(cc * A_CHUNK, (cc + 1) * A_CHUNK)
            lhs = jnp.concatenate([refs[0][0, sl, pair(j)], refs[1][0, sl, pair(j)]], axis=0)
            boths.append(_dot(lhs, _bd2(h.astype(BF16), masks_b)))
        new_hs = []
        for both, (refs, cc, j) in zip(boths, chains):
            sl = slice(cc * A_CHUNK, (cc + 1) * A_CHUNK)
            refs[4][0, sl, pair(j)] = (both[0:A_CHUNK] + refs[3][0, sl, pair(j)].astype(F32)).astype(refs[4].dtype)
            new_hs.append(both[A_CHUNK:] + refs[2][0, sl, pair(j)].astype(F32))
        hs = new_hs
    for j in range(n_pairs):
        hf_s[:, pair(j)] = hs[j]
        hb_s[:, pair(j)] = hs[n_pairs + j]
        hff_ref[0, :, pair(j)] = hs[j]
        hfb_ref[0, :, pair(j)] = hs[n_pairs + j]


def _rwkv_seq(par, h0f, h0b, tt):
    rtf, oif, phf, dlf, rtb, oib, phb, dlb = par
    b, l, _ = rtf.shape
    n_tiles = l // tt
    fwd = pl.BlockSpec((1, tt, D_MODEL), lambda bi, i: (bi, i, 0))
    bwd = pl.BlockSpec((1, tt, D_MODEL), lambda bi, i: (bi, n_tiles - 1 - i, 0))
    st = pl.BlockSpec((1, A_HEAD, D_MODEL), lambda bi, i: (bi, 0, 0))
    o_sd = jax.ShapeDtypeStruct((b, l, D_MODEL), BF16)
    h_sd = jax.ShapeDtypeStruct((b, A_HEAD, D_MODEL), F32)
    return pl.pallas_call(
        functools.partial(_rwkv_seq_kernel, tt=tt),
        grid=(b, n_tiles),
        in_specs=[fwd] * 4 + [bwd] * 4 + [st, st],
        out_specs=[fwd, bwd, st, st],
        out_shape=[o_sd, o_sd, h_sd, h_sd],
        scratch_shapes=[pltpu.VMEM((A_HEAD, D_MODEL), F32), pltpu.VMEM((A_HEAD, D_MODEL), F32)],
        compiler_params=_cparams(("arbitrary", "arbitrary")),
        name="rwkv_seq",
    )(rtf, oif, phf, dlf, rtb, oib, phb, dlb, h0f, h0b)


def _ret_kernel(lg_ref, q_ref, k_ref, v_ref, gr_ref, rc_ref, rs_ref, cc_ref, cs_ref, s0f_ref, s0b_ref,
                lnw_ref, lnb_ref, o_ref, s_s, qr_s, kr_s, ob_s, *, tt, n_heads, n_tiles):
    bh = pl.program_id(0)
    j = pl.program_id(1)
    h = bh % n_heads
    lg_f = lg_ref[0, h]
    lg_b = lg_ref[1, h]
    idx = lax.broadcasted_iota(jnp.int32, (R_BLOCK, R_QK), 0).astype(F32)
    nc = tt // R_BLOCK
    n_rows = tt // GRID_W

    def blk(c):
        return slice(c * R_BLOCK, (c + 1) * R_BLOCK)

    @pl.when(j < n_tiles)
    def _backward():
        tile = n_tiles - 1 - j

        @pl.when(j == 0)
        def _():
            s_s[...] = s0b_ref[0]

        r0 = tile * n_rows

        def by_row(ref):
            return jnp.concatenate([jnp.broadcast_to(ref[pl.ds(r0 + t_, 1), :], (GRID_W, 128)) for t_ in range(n_rows)],
                                   axis=0)

        def by_col(ref):
            return jnp.concatenate([ref[...]] * n_rows, axis=0)

        cos_r, sin_r, cos_c, sin_c = by_row(rc_ref), by_row(rs_ref), by_col(cc_ref), by_col(cs_ref)

        def rope(x_ref):
            x = x_ref[0].astype(F32)
            xa, xb = x[:, 0:128], x[:, 128:256]
            return jnp.concatenate([xa * cos_r + pltpu.roll(xa, 64, axis=1) * sin_r,
                                    xb * cos_c + pltpu.roll(xb, 64, axis=1) * sin_c], axis=1)

        q = rope(q_ref)
        k = rope(k_ref) * (R_QK ** -0.5)
        row0 = pl.multiple_of(tile * tt, tt)
        qr_s[pl.ds(row0, tt), :] = q.astype(BF16)
        kr_s[pl.ds(row0, tt), :] = k.astype(BF16)

        q_dec = jnp.exp(lg_b * (R_BLOCK - idx))
        k_dec = jnp.exp(lg_b * idx)
        c_dec = jnp.exp(lg_b * R_BLOCK)
        order = [nc - 1 - c for c in range(nc)]
        kvs = {c: _dot_tn((k[blk(c)] * k_dec).astype(BF16), v_ref[0, blk(c), :]) for c in order}
        s = s_s[...]
        for c in order:
            inter = _dot((q[blk(c)] * q_dec).astype(BF16), s.astype(BF16))
            ob_s[pl.ds(pl.multiple_of(row0 + c * R_BLOCK, R_BLOCK), R_BLOCK), :] = inter.astype(BF16)
            s = s * c_dec + kvs[c]
        s_s[...] = s

    @pl.when(j >= n_tiles)
    def _forward():
        tile = j - n_tiles

        @pl.when(j == n_tiles)
        def _():
            s_s[...] = s0f_ref[0]

        row0 = pl.multiple_of(tile * tt, tt)
        q = qr_s[pl.ds(row0, tt), :].astype(F32)
        k = kr_s[pl.ds(row0, tt), :].astype(F32)
        q_dec = jnp.exp(lg_f * (idx + 1.0))
        k_dec = jnp.exp(lg_f * (R_BLOCK - 1.0 - idx))
        c_dec = jnp.exp(lg_f * R_BLOCK)
        di = lax.broadcasted_iota(jnp.int32, (R_BLOCK, R_BLOCK), 0)
        dj = lax.broadcasted_iota(jnp.int32, (R_BLOCK, R_BLOCK), 1)
        diff = (di - dj).astype(F32)
        dmask = (jnp.where(diff >= 0, jnp.exp(lg_f * jnp.maximum(diff, 0.0)), 0.0)
                 + jnp.where(diff <= 0, jnp.exp(lg_b * jnp.maximum(-diff, 0.0)), 0.0))
        scores = [_dot_nt(q[blk(c)].astype(BF16), k[blk(c)].astype(BF16)) for c in range(nc)]
        kvs = [_dot_tn((k[blk(c)] * k_dec).astype(BF16), v_ref[0, blk(c), :]) for c in range(nc)]
        s = s_s[...]
        inter = []
        for c in range(nc):
            inter.append(_dot((q[blk(c)] * q_dec).astype(BF16), s.astype(BF16)))
            s = s * c_dec + kvs[c]
        s_s[...] = s
        for c in range(nc):
            back = ob_s[pl.ds(pl.multiple_of(row0 + c * R_BLOCK, R_BLOCK), R_BLOCK), :].astype(F32)
            o = inter[c] + _dot((scores[c] * dmask).astype(BF16), v_ref[0, blk(c), :]) + back
            dv = o - jnp.mean(o, axis=-1, keepdims=True)
            var = jnp.mean(dv * dv, axis=-1, keepdims=True)
            o = (dv * lax.rsqrt(var + R_GN_EPS) * lnw_ref[...] + lnb_ref[...]) * _silu_half(gr_ref[0, blk(c), :].astype(F32))
            o_ref[0, blk(c), :] = o.astype(o_ref.dtype)


def _retention(p, lg, tabs, s0f, s0b, ln_w, ln_b, tt):
    b, l, _ = p.shape
    n_tiles = l // tt
    nh = R_HEADS

    def back_tile(j):
        return jnp.maximum(n_tiles - 1 - j, 0)

    def fwd_tile(j):
        return jnp.maximum(j - n_tiles, 0)

    def col(width, off, tile):
        return pl.BlockSpec((1, tt, width), lambda bh, j, lg_: (bh // nh, tile(j), off // width + bh % nh))

    def whole(a):
        return pl.BlockSpec(a.shape, lambda bh, j, lg_: (0, 0))

    st = pl.BlockSpec((1, R_QK, R_V), lambda bh, j, lg_: (bh, 0, 0))
    lnv = pl.BlockSpec((1, R_V), lambda bh, j, lg_: (0, bh % nh))
    grid_spec = pltpu.PrefetchScalarGridSpec(
        num_scalar_prefetch=1,
        grid=(b * nh, 2 * n_tiles),
        in_specs=[col(R_QK, OFF_Q, back_tile), col(R_QK, OFF_K, back_tile),
                  col(R_V, OFF_V, lambda j: jnp.where(j < n_tiles, n_tiles - 1 - j, j - n_tiles)),
                  col(R_V, OFF_GR, fwd_tile)] + [whole(t) for t in tabs] + [st, st, lnv, lnv],
        out_specs=pl.BlockSpec((1, tt, R_V), lambda bh, j, lg_: (bh // nh, fwd_tile(j), bh % nh)),
        scratch_shapes=[pltpu.VMEM((R_QK, R_V), F32), pltpu.VMEM((l, R_QK), BF16), pltpu.VMEM((l, R_QK), BF16),
                        pltpu.VMEM((l, R_V), BF16)],
    )
    return pl.pallas_call(
        functools.partial(_ret_kernel, tt=tt, n_heads=nh, n_tiles=n_tiles),
        grid_spec=grid_spec,
        out_shape=jax.ShapeDtypeStruct((b, l, nh * R_V), BF16),
        compiler_params=_cparams(("arbitrary", "arbitrary")),
        name="retention",
    )(lg, p, p, p, p, *tabs, s0f, s0b, ln_w.reshape(1, -1), ln_b.reshape(1, -1))


def _ret_ctx_kernel(lg_ref, k_ref, v_ref, sf_ref, sb_ref, *, n_heads):
    h = pl.program_id(0) % n_heads
    k = k_ref[0].astype(F32) * (R_QK ** -0.5)
    v = v_ref[0]
    n = k.shape[0]
    idx = lax.broadcasted_iota(jnp.int32, k.shape, 0).astype(F32)
    sf_ref[0] = _dot_tn((k * jnp.exp(lg_ref[0, h] * (n - 1.0 - idx))).astype(BF16), v)
    sb_ref[0] = _dot_tn((k * jnp.exp(lg_ref[1, h] * idx)).astype(BF16), v)


def _ret_ctx_states(p, lg):
    b, l, _ = p.shape
    nh = R_HEADS
    st = pl.BlockSpec((1, R_QK, R_V), lambda bh, lg_: (bh, 0, 0))
    sd = jax.ShapeDtypeStruct((b * nh, R_QK, R_V), F32)
    grid_spec = pltpu.PrefetchScalarGridSpec(
        num_scalar_prefetch=1,
        grid=(b * nh,),
        in_specs=[pl.BlockSpec((1, l, R_QK), lambda bh, lg_: (bh // nh, 0, OFF_K // R_QK + bh % nh)),
                  pl.BlockSpec((1, l, R_V), lambda bh, lg_: (bh // nh, 0, OFF_V // R_V + bh % nh))],
        out_specs=[st, st],
    )
    return pl.pallas_call(
        functools.partial(_ret_ctx_kernel, n_heads=nh),
        grid_spec=grid_spec,
        out_shape=[sd, sd],
        compiler_params=_cparams(("arbitrary",)),
        name="ret_ctx",
    )(lg, p, p)


def _final_kernel(x_ref, ga_ref, ma_ref, mb_ref, of_ref, ob_ref, bon_ref, yr_ref,
                  gate_ref, alw_ref, alb_ref, fnw_ref, awo_ref, rwo_ref, wo_ref, o_ref):
    li = lax.broadcasted_iota(jnp.int32, (QUAD, QUAD), 0) // A_HEAD
    lj = lax.broadcasted_iota(jnp.int32, (QUAD, QUAD), 1) // A_HEAD
    msum = (li == lj).astype(BF16)
    inv = 1.0 / A_HEAD

    def head_sum(t):
        return jnp.concatenate([_dot(t[:, q * QUAD:(q + 1) * QUAD].astype(BF16), msum) for q in range(N_QUADS)], axis=1)

    oa = of_ref[0].astype(F32) + ob_ref[0].astype(F32)
    mu = head_sum(oa) * inv
    dv = oa - mu
    var = head_sum(dv * dv) * inv
    ya = dv * lax.rsqrt(var + A_GN_EPS) * alw_ref[...] + alb_ref[...] + bon_ref[0].astype(F32)
    ya = _dot((ya * _silu_half(ga_ref[0].astype(F32))).astype(BF16), awo_ref[...])

    yr = _dot(yr_ref[0], rwo_ref[...])

    merged = (1.0 + jnp.tanh(ma_ref[0].astype(F32))) * ya + (1.0 + jnp.tanh(mb_ref[0].astype(F32))) * yr
    y = _dot(merged.astype(BF16), wo_ref[...])
    xo = x_ref[0] + gate_ref[0] * y
    o_ref[0] = xo * lax.rsqrt(jnp.mean(xo * xo, axis=-1, keepdims=True) + NORM_EPS) * fnw_ref[...]


def _final(x, p, of, ob, bonus, y_ret, gate, a_ln_w, a_ln_b, final_w, awo, rwo, wo, tm):
    b, l, d = x.shape

    def tok(width, off=0):
        return pl.BlockSpec((1, tm, width), lambda bi, i: (bi, i, off // width))

    def vec(width):
        return pl.BlockSpec((1, width), lambda bi, i: (0, 0))

    def mat(r, c):
        return pl.BlockSpec((r, c), lambda bi, i: (0, 0))

    rv = R_HEADS * R_V
    return pl.pallas_call(
        _final_kernel,
        grid=(b, l // tm),
        in_specs=[tok(d), tok(d, OFF_GA), tok(d, OFF_MA), tok(d, OFF_MB),
                  tok(d), tok(d), tok(d), tok(rv),
                  pl.BlockSpec((1, 1, d), lambda bi, i: (bi, 0, 0)),
                  vec(d), vec(d), vec(d), mat(d, d), mat(rv, d), mat(d, d)],
        out_specs=tok(d),
        out_shape=jax.ShapeDtypeStruct((b, l, d), F32),
        compiler_params=_cparams(("arbitrary", "arbitrary")),
        name="final",
    )(x, p, p, p, of, ob, bonus, y_ret, gate,
      a_ln_w.reshape(1, d), a_ln_b.reshape(1, d), final_w.reshape(1, d), awo, rwo, wo)


def _rope_tables(l):
    half = R_QK // 4
    freqs = ROPE_BASE ** (-jnp.arange(half, dtype=F32) / half)

    def tab(n):
        ang = jnp.arange(n, dtype=F32)[:, None] * freqs[None, :]
        return (jnp.concatenate([jnp.cos(ang), jnp.cos(ang)], axis=1),
                jnp.concatenate([-jnp.sin(ang), jnp.sin(ang)], axis=1))

    return tab(l // GRID_W) + tab(GRID_W)


def _reorder_w_in(w):
    return jnp.concatenate([w[:, 0:3072], 0.5 * w[:, 3072:4096], w[:, 4352:8448], 0.5 * w[:, 8448:], w[:, 4096:4352]],
                           axis=1)


def _pick_tile(l, pref):
    t = min(l, pref)
    while l % t:
        t //= 2
    return t


def kernel(x, c, ctx, c_ctx, norm_w, ada_w, ada_b, w_in, a_conv, a_w_up, a_w0, a_a_up, a_a0, a_k_k, a_k_a,
           a_r_k, a_ln_w, a_ln_b, a_w_out, r_decay, r_ln_w, r_ln_b, r_w_out, w_out, final_norm_w):
    b, l, d = x.shape
    lc = ctx.shape[1]
    assert d == D_MODEL and l % R_BLOCK == 0 and lc % R_BLOCK == 0
    lyr = 0

    cc = jnp.zeros((16, d), F32).at[:b].set(c).at[b].set(c_ctx)
    mod = _adaln(cc, ada_w[lyr], ada_b[lyr])
    shift, scale, gate = mod[:b, :d], mod[:b, d:2 * d], mod[:b, 2 * d:]
    shift_c = jnp.broadcast_to(mod[b, :d], (b, d))
    scale_c = jnp.broadcast_to(mod[b, d:2 * d], (b, d))

    w_bf = _reorder_w_in(w_in[lyr]).astype(BF16)
    p_x = _inproj(x, (1.0 + scale)[:, None, :], shift[:, None, :], norm_w[lyr], w_bf, _pick_tile(l, 2048))
    p_c = _inproj(ctx.reshape(1, b * lc, d), (1.0 + scale_c)[:1, None, :], shift_c[:1, None, :], norm_w[lyr], w_bf,
                  _pick_tile(b * lc, 2048)).reshape(b, lc, -1)

    zpad = jnp.zeros((A_LORA, d), F32)
    wup_pad = jnp.stack([jnp.concatenate([a_w_up[lyr, 0], zpad], 0), jnp.concatenate([zpad, a_w_up[lyr, 1]], 0)])
    aup_pad = jnp.stack([jnp.concatenate([a_a_up[lyr, 0], zpad], 0), jnp.concatenate([zpad, a_a_up[lyr, 1]], 0)])
    a_args = (a_conv[lyr], a_k_k[lyr], a_k_a[lyr], a_r_k[lyr].reshape(-1), a_w0[lyr], a_a0[lyr], wup_pad, aup_pad)
    h_zero = jnp.zeros((b, A_HEAD, d), F32)
    par_c = _rwkv_par(p_c, *a_args, tt=_pick_tile(lc, 256))
    _, _, hcf, hcb = _rwkv_seq(par_c[:8], h_zero, h_zero, _pick_tile(lc, 512))
    par_x = _rwkv_par(p_x, *a_args, tt=_pick_tile(l, 1024))
    o_af, o_ab, _, _ = _rwkv_seq(par_x[:8], hcf, hcb, _pick_tile(l, 512))
    bonus = par_x[8]

    lg = -jnp.exp(r_decay[lyr].astype(F32))
    tabs_x = _rope_tables(l)
    sc_f, sc_b = _ret_ctx_states(p_c, lg)
    y_r = _retention(p_x, lg, tabs_x, sc_f, sc_b, r_ln_w[lyr], r_ln_b[lyr], _pick_tile(l, 2048))

    return _final(x, p_x, o_af, o_ab, bonus, y_r, gate[:, None, :], a_ln_w[lyr], a_ln_b[lyr],
                  final_norm_w, (0.5 * a_w_out[lyr]).astype(BF16),
                  (0.5 * r_w_out[lyr]).astype(BF16), w_out[lyr].astype(BF16), _pick_tile(l, 512))
```

```python
import functools
import math

import jax
import jax.numpy as jnp
from jax import lax
from jax.experimental import pallas as pl
from jax.experimental.pallas import tpu as pltpu

F32 = jnp.float32
BF16 = jnp.bfloat16

D_MODEL = 1024
A_HEAD = 64
A_HEADS = D_MODEL // A_HEAD
A_LORA = 64
QUAD = 4 * A_HEAD
PAIR = 2 * A_HEAD
N_QUADS = D_MODEL // QUAD
A_CHUNK = 64
SUB = 8
CUM_BLOCK = 256
UNIT_GROUP = 16
R_HEADS = 4
R_QK = 256
R_V = 512
R_BLOCK = 256
GRID_W = 64
ROPE_BASE = 10000.0
NORM_EPS = 1e-6
A_GN_EPS = 64e-5
R_GN_EPS = 1e-5
L2_EPS = 1e-12
DECAY_SCALE = math.exp(-0.5)

OFF_RKV, OFF_GA, OFF_Q, OFF_K, OFF_V, OFF_GR, OFF_MA, OFF_MB, OFF_LO = (
    0, 3072, 4096, 5120, 6144, 8192, 10240, 11264, 12288)
V7X_VMEM_LIMIT = 56 * 1024 * 1024


def _cparams(sem):
    return pltpu.CompilerParams(dimension_semantics=sem, vmem_limit_bytes=V7X_VMEM_LIMIT)


def _dot(a, b):
    return jnp.dot(a, b, preferred_element_type=F32)


def _dot_nt(a, b):
    return lax.dot_general(a, b, (((1,), (1,)), ((), ())), preferred_element_type=F32)


def _dot_tn(a, b):
    return lax.dot_general(a, b, (((0,), (0,)), ((), ())), preferred_element_type=F32)


def _split(x):
    hi = x.astype(BF16)
    lo = (x - hi.astype(F32)).astype(BF16)
    return hi, lo


def _dot_exact_lhs(m, x):
    hi, lo = _split(x)
    return _dot(m, hi) + _dot(m, lo)


def _dot3(a, b):
    ah, al = _split(a)
    bh, bl = _split(b)
    return _dot(ah, bh) + _dot(ah, bl) + _dot(al, bh)


def _sigmoid(x):
    return 0.5 * jnp.tanh(0.5 * x) + 0.5


def _silu(x):
    return x * _sigmoid(x)


def _silu_half(h):
    return h + h * jnp.tanh(h)


def _adaln_kernel(c_ref, w_ref, b_ref, o_ref):
    cs = _silu(c_ref[...])
    o_ref[...] = _dot3(cs, w_ref[...]) + b_ref[...]


def _adaln(cc, ada_w, ada_b):
    n = ada_w.shape[1]
    tn = 512
    return pl.pallas_call(
        _adaln_kernel,
        grid=(n // tn,),
        in_specs=[pl.BlockSpec(cc.shape, lambda j: (0, 0)),
                  pl.BlockSpec((ada_w.shape[0], tn), lambda j: (0, j)),
                  pl.BlockSpec((1, tn), lambda j: (0, j))],
        out_specs=pl.BlockSpec((cc.shape[0], tn), lambda j: (0, j)),
        out_shape=jax.ShapeDtypeStruct((cc.shape[0], n), F32),
        compiler_params=_cparams(("arbitrary",)),
        name="adaln",
    )(cc, ada_w, ada_b.reshape(1, n))


def _inproj_kernel(x_ref, sc_ref, sh_ref, nw_ref, w_ref, o_ref, u_ref):
    @pl.when(pl.program_id(2) == 0)
    def _():
        xf = x_ref[0]
        y = xf * lax.rsqrt(jnp.mean(xf * xf, axis=-1, keepdims=True) + NORM_EPS) * nw_ref[...]
        u_ref[...] = (y * sc_ref[0] + sh_ref[0]).astype(BF16)

    o_ref[0] = _dot(u_ref[...], w_ref[...]).astype(o_ref.dtype)


def _inproj(x, scale1p, shift, norm_w, w_bf16, tm):
    b, l, d = x.shape
    n = w_bf16.shape[1]
    tn = 1792
    return pl.pallas_call(
        _inproj_kernel,
        grid=(b, l // tm, n // tn),
        in_specs=[pl.BlockSpec((1, tm, d), lambda bi, i, j: (bi, i, 0)),
                  pl.BlockSpec((1, 1, d), lambda bi, i, j: (bi, 0, 0)),
                  pl.BlockSpec((1, 1, d), lambda bi, i, j: (bi, 0, 0)),
                  pl.BlockSpec((1, d), lambda bi, i, j: (0, 0)),
                  pl.BlockSpec((d, tn), lambda bi, i, j: (0, j))],
        out_specs=pl.BlockSpec((1, tm, tn), lambda bi, i, j: (bi, i, j)),
        out_shape=jax.ShapeDtypeStruct((b, l, n), BF16),
        scratch_shapes=[pltpu.VMEM((tm, d), BF16)],
        compiler_params=_cparams(("arbitrary", "arbitrary", "arbitrary")),
        name="inproj",
    )(x, scale1p, shift, norm_w.reshape(1, d), w_bf16)


def _head_masks(rows, dtype):
    lane_head = lax.broadcasted_iota(jnp.int32, (rows, PAIR), 1) // A_HEAD
    return [(lane_head == h).astype(dtype) for h in range(2)]


def _bd2(x_bf16, masks):
    return jnp.concatenate([x_bf16 * m for m in masks], axis=0)


def _rwkv_par_kernel(r_ref, k_ref, v_ref, rp_ref, kp_ref, vp_ref, rn_ref, kn_ref, vn_ref, lo_ref,
                     cr_ref, ck_ref, cv_ref, kkw_ref, kaw_ref, rkw_ref, w0_ref, a0_ref, wup_ref, aup_ref,
                     rtf_ref, oif_ref, phf_ref, dlf_ref, rtb_ref, oib_ref, phb_ref, dlb_ref, bon_ref,
                     *, tt, n_tiles):
    i = pl.program_id(1)
    row = lax.broadcasted_iota(jnp.int32, (tt, QUAD), 0)
    has_prev = (i > 0).astype(F32)
    has_next = (i < n_tiles - 1).astype(F32)

    def conv(x_ref, p_ref, n_ref, c_ref):
        x = x_ref[0].astype(F32)
        prev_row = p_ref[0][15:16, :].astype(F32) * has_prev
        next_row = n_ref[0][0:1, :].astype(F32) * has_next
        x_prev = jnp.where(row == 0, prev_row, pltpu.roll(x, 1, axis=0))
        x_next = jnp.where(row == tt - 1, next_row, pltpu.roll(x, tt - 1, axis=0))
        cw = c_ref[...]
        return x_prev * cw[0:1] + x * cw[1:2] + x_next * cw[2:3]

    r = conv(r_ref, rp_ref, rn_ref, cr_ref)
    k = conv(k_ref, kp_ref, kn_ref, ck_ref)
    v = conv(v_ref, vp_ref, vn_ref, cv_ref)

    li = lax.broadcasted_iota(jnp.int32, (QUAD, QUAD), 0) // A_HEAD
    lj = lax.broadcasted_iota(jnp.int32, (QUAD, QUAD), 1) // A_HEAD
    msum = (li == lj).astype(BF16)

    kk = k * kkw_ref[...]
    kk = kk * lax.rsqrt(_dot((kk * kk).astype(BF16), msum) + L2_EPS)

    cb = min(tt, CUM_BLOCK)
    ti = lax.broadcasted_iota(jnp.int32, (cb, cb), 0)
    tj = lax.broadcasted_iota(jnp.int32, (cb, cb), 1)
    same_chunk = (ti // A_CHUNK) == (tj // A_CHUNK)

    lo = lo_ref[0].astype(F32)
    lact = jnp.tanh(lo[:, 0:2 * A_LORA])
    la = lo[:, 2 * A_LORA:4 * A_LORA]

    masks_b = _head_masks(A_CHUNK, BF16)
    ct = lax.broadcasted_iota(jnp.int32, (A_CHUNK, PAIR), 0)
    cs = lax.broadcasted_iota(jnp.int32, (A_CHUNK, PAIR), 1) % A_HEAD
    eye_q = (ct == cs).astype(F32)
    lane_c = lax.broadcasted_iota(jnp.int32, (SUB, PAIR), 1)
    blk_of_lane = (lane_c % A_HEAD) // SUB
    blk_f = [(blk_of_lane == b_).astype(F32) for b_ in range(A_CHUNK // SUB)]
    head_blk_b = [((lane_c // A_HEAD == h_) & (blk_of_lane == b_)).astype(BF16)
                  for h_ in range(2) for b_ in range(A_CHUNK // SUB)]
    eye_c = (lax.broadcasted_iota(jnp.int32, (SUB, PAIR), 0) == lane_c % SUB).astype(F32)
    merges_b = [(((ct // (2 * w)) == (cs // (2 * w))) & ((ct // w) != (cs // w))).astype(BF16)
                for w in (SUB, 2 * SUB, 4 * SUB)]

    second_head = (lax.broadcasted_iota(jnp.int32, (tt, QUAD), 1) // A_HEAD) % 2 == 1

    def transposed(x):
        xb = x.astype(BF16)
        ahead = jnp.concatenate([xb[tt - A_CHUNK:], xb[:tt - A_CHUNK]], axis=0)
        return jnp.where(second_head, ahead, xb).T

    bonus = jnp.zeros((tt, QUAD), F32)
    outs = ((rtf_ref, oif_ref, phf_ref, dlf_ref), (rtb_ref, oib_ref, phb_ref, dlb_ref))
    prep = []
    for d in range(2):
        if d == 0:
            tri = (same_chunk & (tj <= ti)).astype(BF16)
            strict = (cs < ct).astype(F32)
            incl = (cs <= ct).astype(F32)
        else:
            tri = (same_chunk & (tj >= ti)).astype(BF16)
            strict = (cs > ct).astype(F32)
            incl = (cs >= ct).astype(F32)
        z = w0_ref[d:d + 1, :] + _dot3(lact, wup_ref[d])
        logw = -DECAY_SCALE * _sigmoid(z)
        av = _sigmoid(a0_ref[d:d + 1, :] + _dot3(la, aup_ref[d]))
        kd = k * (1.0 + (av - 1.0) * kaw_ref[...])
        bonus = bonus + _dot((r * kd * rkw_ref[...]).astype(BF16), msum) * v

        cum = jnp.concatenate([_dot_exact_lhs(tri, logw[j * cb:(j + 1) * cb]) for j in range(tt // cb)], axis=0)
        last = 0 if d == 1 else A_CHUNK - 1
        ctot = jnp.concatenate([jnp.broadcast_to(cum[c * A_CHUNK + last:c * A_CHUNK + last + 1, :], (A_CHUNK, QUAD))
                                for c in range(tt // A_CHUNK)], axis=0)
        e_inv = jnp.exp(-cum)
        e_g = jnp.exp(ctot - cum)
        kka = kk * av
        prep.append(dict(kh=kk * jnp.exp(cum - logw), rh=r * jnp.exp(cum), gam=jnp.exp(ctot),
                         bt_t=transposed(kka * e_inv), kt_t=transposed(kd * e_inv),
                         bg_t=transposed(kka * e_g), kg_t=transposed(kd * e_g),
                         strict_b=strict.astype(BF16), incl_b=incl.astype(BF16),
                         diag=[strict[b_ * SUB:(b_ + 1) * SUB] * blk_f[b_] for b_ in range(A_CHUNK // SUB)]))

    all_units = [(d, c, pr) for d in range(2) for c in range(tt // A_CHUNK) for pr in range(QUAD // PAIR)]

    def cut(x, c, pr):
        return x[c * A_CHUNK:(c + 1) * A_CHUNK, pr * PAIR:(pr + 1) * PAIR]

    n_cols = tt // PAIR

    def head_masks(c):
        return masks_b if c % 2 == 0 else masks_b[::-1]

    def bd(x, c):
        return _bd2(x.astype(BF16), head_masks(c))

    def cut_t(xt, c, pr):
        m_top, m_bot = head_masks(c)
        col_top = c // 2
        col_bot = c // 2 if c % 2 == 0 else ((c + 1) // 2) % n_cols
        top = xt[pr * PAIR:pr * PAIR + A_HEAD, col_top * PAIR:(col_top + 1) * PAIR] * m_top
        bot = xt[pr * PAIR + A_HEAD:(pr + 1) * PAIR, col_bot * PAIR:(col_bot + 1) * PAIR] * m_bot
        return jnp.concatenate([top, bot], axis=0), top + bot

    def stage_a(d, c, pr):
        pd = prep[d]
        kh, rh = cut(pd["kh"], c, pr), cut(pd["rh"], c, pr)
        lhs1 = jnp.concatenate([kh, rh], axis=0).astype(BF16)
        w_bt, _ = cut_t(pd["bt_t"], c, pr)
        w_kt, _ = cut_t(pd["kt_t"], c, pr)
        a_all = _dot(lhs1, jnp.concatenate([w_bt, w_kt], axis=1))
        a_b = a_all.astype(BF16)
        diag = a_all[0:SUB, 0:PAIR] * pd["diag"][0]
        for b_ in range(1, A_CHUNK // SUB):
            diag = diag + a_all[b_ * SUB:(b_ + 1) * SUB, 0:PAIR] * pd["diag"][b_]
        return dict(kh=kh, rh=rh, diag=diag,
                    a_ab=a_b[0:A_CHUNK, 0:PAIR] * pd["strict_b"],
                    a_krk=jnp.concatenate([a_b[0:A_CHUNK, PAIR:] * pd["strict_b"],
                                           a_b[A_CHUNK:, PAIR:] * pd["incl_b"],
                                           cut_t(pd["kg_t"], c, pr)[1]], axis=0),
                    a_rbb=jnp.concatenate([a_b[A_CHUNK:, 0:PAIR] * pd["incl_b"],
                                           cut_t(pd["bg_t"], c, pr)[1]], axis=0))

    def run_group(units):
        st = [stage_a(*u_) for u_ in units]

        def expand_w(xc):
            xb = xc.astype(BF16)
            return jnp.concatenate([xb * m for m in head_blk_b], axis=0)

        ps = [-s_["diag"] for s_ in st]
        ts = [eye_c + p for p in ps]
        ps = [_dot(p.astype(BF16), expand_w(p)) for p in ps]
        both = [_dot(jnp.concatenate([p, t], axis=0).astype(BF16), expand_w(p)) for p, t in zip(ps, ts)]
        ps = [b_[0:SUB] for b_ in both]
        ts = [t + b_[SUB:] for t, b_ in zip(ts, both)]
        ts = [t + _dot(t.astype(BF16), expand_w(p)) for t, p in zip(ts, ps)]
        ts = [jnp.concatenate([t * m for m in blk_f], axis=0).astype(BF16) for t in ts]
        for blk in merges_b:
            gs = [_dot(t, _bd2(s_["a_ab"] * blk, masks_b)).astype(BF16) for t, s_ in zip(ts, st)]
            ts = [t - _dot(g, _bd2(t, masks_b)).astype(BF16) for t, g in zip(ts, gs)]

        xas = [_dot(s_["a_krk"], bd(cut(v, c, pr), c)) for s_, (d, c, pr) in zip(st, units)]
        wus = [_dot(t, jnp.concatenate([bd(s_["kh"], c), bd(xa[0:A_CHUNK], c)], axis=1))
               for t, s_, xa, (d, c, pr) in zip(ts, st, xas, units)]
        wubs = [wu.astype(BF16) for wu in wus]
        arbs = [_dot(s_["a_rbb"], jnp.concatenate([_bd2(wb[:, 0:PAIR], head_masks(c)),
                                                   _bd2(wb[:, PAIR:], head_masks(c))], axis=1))
                for s_, wb, (d, c, pr) in zip(st, wubs, units)]

        for (d, c, pr), s_, xa, arb in zip(units, st, xas, arbs):
            rt_ref, oi_ref, ph_ref, dl_ref = outs[d]
            gam = cut(prep[d]["gam"], c, pr)
            rw = slice(c * A_CHUNK, (c + 1) * A_CHUNK)
            ln = slice(pr * PAIR, (pr + 1) * PAIR)
            rt_ref[0, rw, ln] = (s_["rh"] - arb[0:A_CHUNK, 0:PAIR]).astype(rt_ref.dtype)
            oi_ref[0, rw, ln] = (xa[A_CHUNK:2 * A_CHUNK] - arb[0:A_CHUNK, PAIR:]).astype(oi_ref.dtype)
            ph_ref[0, rw, ln] = (eye_q * gam - arb[A_CHUNK:, 0:PAIR]).astype(ph_ref.dtype)
            dl_ref[0, rw, ln] = (xa[2 * A_CHUNK:] - arb[A_CHUNK:, PAIR:]).astype(dl_ref.dtype)

    for g_ in range(0, len(all_units), UNIT_GROUP):
        run_group(all_units[g_:g_ + UNIT_GROUP])
    bon_ref[0] = bonus.astype(bon_ref.dtype)


def _rwkv_par(p, a_conv, k_k, k_a, r_k, w0, a0, wup_pad, aup_pad, tt):
    b, l, _ = p.shape
    n_tiles = l // tt
    h16 = tt // 16
    last16 = l // 16 - 1

    def tok(col0):
        return pl.BlockSpec((1, tt, QUAD), lambda bi, i, q: (bi, i, col0 // QUAD + q))

    def prev(col0):
        return pl.BlockSpec((1, 16, QUAD), lambda bi, i, q: (bi, jnp.maximum(i * h16 - 1, 0), col0 // QUAD + q))

    def nxt(col0):
        return pl.BlockSpec((1, 16, QUAD),
                            lambda bi, i, q: (bi, jnp.minimum((i + 1) * h16, last16), col0 // QUAD + q))

    def vec(rows, col0=0):
        return pl.BlockSpec((rows, QUAD), lambda bi, i, q: (0, col0 // QUAD + q))

    out_tok = pl.BlockSpec((1, tt, QUAD), lambda bi, i, q: (bi, i, q))
    out_sd = jax.ShapeDtypeStruct((b, l, D_MODEL), BF16)
    kern = functools.partial(_rwkv_par_kernel, tt=tt, n_tiles=n_tiles)
    return pl.pallas_call(
        kern,
        grid=(b, n_tiles, N_QUADS),
        in_specs=[tok(0), tok(1024), tok(2048), prev(0), prev(1024), prev(2048),
                  nxt(0), nxt(1024), nxt(2048),
                  pl.BlockSpec((1, tt, 4 * A_LORA), lambda bi, i, q: (bi, i, OFF_LO // (4 * A_LORA))),
                  vec(3, 0), vec(3, 1024), vec(3, 2048),
                  vec(1), vec(1), vec(1), vec(2), vec(2),
                  pl.BlockSpec((2, 2 * A_LORA, QUAD), lambda bi, i, q: (0, 0, q)),
                  pl.BlockSpec((2, 2 * A_LORA, QUAD), lambda bi, i, q: (0, 0, q))],
        out_specs=[out_tok] * 9,
        out_shape=[out_sd] * 9,
        compiler_params=_cparams(("arbitrary", "arbitrary", "arbitrary")),
        name="rwkv_par",
    )(p, p, p, p, p, p, p, p, p, p, a_conv, a_conv, a_conv,
      k_k.reshape(1, D_MODEL), k_a.reshape(1, D_MODEL), r_k.reshape(1, D_MODEL), w0, a0, wup_pad, aup_pad)


def _rwkv_seq_kernel(rtf_ref, oif_ref, phf_ref, dlf_ref, rtb_ref, oib_ref, phb_ref, dlb_ref,
                     h0f_ref, h0b_ref, of_ref, ob_ref, hff_ref, hfb_ref, hf_s, hb_s, *, tt):
    i = pl.program_id(1)

    @pl.when(i == 0)
    def _():
        hf_s[...] = h0f_ref[0]
        hb_s[...] = h0b_ref[0]

    masks_b = _head_masks(A_CHUNK, BF16)
    nc = tt // A_CHUNK
    fwd_refs = (rtf_ref, phf_ref, dlf_ref, oif_ref, of_ref)
    bwd_refs = (rtb_ref, phb_ref, dlb_ref, oib_ref, ob_ref)

    n_pairs = D_MODEL // PAIR

    def pair(j):
        return slice(j * PAIR, (j + 1) * PAIR)

    hs = [hf_s[:, pair(j)] for j in range(n_pairs)] + [hb_s[:, pair(j)] for j in range(n_pairs)]
    for c in range(nc):
        chains = [(fwd_refs, c, j) for j in range(n_pairs)] + [(bwd_refs, nc - 1 - c, j) for j in range(n_pairs)]
        boths = []
        for h, (refs, cc, j) in zip(hs, chains):
            sl = slice(cc * A_CHUNK, (cc + 1) * A_CHUNK)
            lhs = jnp.concatenate([refs[0][0, sl, pair(j)], refs[1][0, sl, pair(j)]], axis=0)
            boths.append(_dot(lhs, _bd2(h.astype(BF16), masks_b)))
        new_hs = []
        for both, (refs, cc, j) in zip(boths, chains):
            sl = slice(cc * A_CHUNK, (cc + 1) * A_CHUNK)
            refs[4][0, sl, pair(j)] = (both[0:A_CHUNK] + refs[3][0, sl, pair(j)].astype(F32)).astype(refs[4].dtype)
            new_hs.append(both[A_CHUNK:] + refs[2][0, sl, pair(j)].astype(F32))
        hs = new_hs
    for j in range(n_pairs):
        hf_s[:, pair(j)] = hs[j]
        hb_s[:, pair(j)] = hs[n_pairs + j]
        hff_ref[0, :, pair(j)] = hs[j]
        hfb_ref[0, :, pair(j)] = hs[n_pairs + j]


def _rwkv_seq(par, h0f, h0b, tt):
    rtf, oif, phf, dlf, rtb, oib, phb, dlb = par
    b, l, _ = rtf.shape
    n_tiles = l // tt
    fwd = pl.BlockSpec((1, tt, D_MODEL), lambda bi, i: (bi, i, 0))
    bwd = pl.BlockSpec((1, tt, D_MODEL), lambda bi, i: (bi, n_tiles - 1 - i, 0))
    st = pl.BlockSpec((1, A_HEAD, D_MODEL), lambda bi, i: (bi, 0, 0))
    o_sd = jax.ShapeDtypeStruct((b, l, D_MODEL), BF16)
    h_sd = jax.ShapeDtypeStruct((b, A_HEAD, D_MODEL), F32)
    return pl.pallas_call(
        functools.partial(_rwkv_seq_kernel, tt=tt),
        grid=(b, n_tiles),
        in_specs=[fwd] * 4 + [bwd] * 4 + [st, st],
        out_specs=[fwd, bwd, st, st],
        out_shape=[o_sd, o_sd, h_sd, h_sd],
        scratch_shapes=[pltpu.VMEM((A_HEAD, D_MODEL), F32), pltpu.VMEM((A_HEAD, D_MODEL), F32)],
        compiler_params=_cparams(("arbitrary", "arbitrary")),
        name="rwkv_seq",
    )(rtf, oif, phf, dlf, rtb, oib, phb, dlb, h0f, h0b)


def _ret_kernel(lg_ref, q_ref, k_ref, v_ref, rc_ref, rs_ref, cc_ref, cs_ref, s0_ref, *rest,
                tt, reverse, with_intra, rotate, n_heads, n_tiles):
    rest = list(rest)
    prev_ref = rest.pop(0) if with_intra else None
    o_ref, sf_ref = rest[0:2]
    qr_ref, kr_ref = rest[2:4] if rotate else (None, None)
    s_s = rest[-1]
    bh = pl.program_id(0)
    i = pl.program_id(1)
    h = bh % n_heads

    @pl.when(i == 0)
    def _():
        s_s[...] = s0_ref[0]

    lg_f = lg_ref[0, h]
    lg_b = lg_ref[1, h]
    lg = lg_b if reverse else lg_f

    n_rows = tt // GRID_W
    r0 = (n_tiles - 1 - i if reverse else i) * n_rows

    def by_row(ref):
        return jnp.concatenate([jnp.broadcast_to(ref[pl.ds(r0 + j, 1), :], (GRID_W, 128)) for j in range(n_rows)], axis=0)

    def by_col(ref):
        return jnp.concatenate([ref[...]] * n_rows, axis=0)

    cos_r, sin_r, cos_c, sin_c = by_row(rc_ref), by_row(rs_ref), by_col(cc_ref), by_col(cs_ref)

    def rope(x_ref):
        x = x_ref[0].astype(F32)
        xa, xb = x[:, 0:128], x[:, 128:256]
        return jnp.concatenate([xa * cos_r + pltpu.roll(xa, 64, axis=1) * sin_r,
                                xb * cos_c + pltpu.roll(xb, 64, axis=1) * sin_c], axis=1)

    if not rotate:
        q = q_ref[0].astype(F32)
        k = k_ref[0].astype(F32)
    else:
        q = rope(q_ref)
        k = rope(k_ref) * (R_QK ** -0.5)
        qr_ref[0] = q.astype(qr_ref.dtype)
        kr_ref[0] = k.astype(kr_ref.dtype)

    idx = lax.broadcasted_iota(jnp.int32, (R_BLOCK, R_QK), 0).astype(F32)
    if reverse:
        q_dec = jnp.exp(lg * (R_BLOCK - idx))
        k_dec = jnp.exp(lg * idx)
    else:
        q_dec = jnp.exp(lg * (idx + 1.0))
        k_dec = jnp.exp(lg * (R_BLOCK - 1.0 - idx))
    c_dec = jnp.exp(lg * R_BLOCK)

    nc = tt // R_BLOCK
    order = [nc - 1 - cc if reverse else cc for cc in range(nc)]

    def blk(c):
        return slice(c * R_BLOCK, (c + 1) * R_BLOCK)

    if with_intra:
        di = lax.broadcasted_iota(jnp.int32, (R_BLOCK, R_BLOCK), 0)
        dj = lax.broadcasted_iota(jnp.int32, (R_BLOCK, R_BLOCK), 1)
        diff = (di - dj).astype(F32)
        dmask = (jnp.where(diff >= 0, jnp.exp(lg_f * jnp.maximum(diff, 0.0)), 0.0)
                 + jnp.where(diff <= 0, jnp.exp(lg_b * jnp.maximum(-diff, 0.0)), 0.0))
        scores = {c: _dot_nt(q[blk(c)].astype(BF16), k[blk(c)].astype(BF16)) for c in order}
    kvs = {c: _dot_tn((k[blk(c)] * k_dec).astype(BF16), v_ref[0, blk(c), :]) for c in order}

    s = s_s[...]
    inter = {}
    for c in order:
        inter[c] = _dot((q[blk(c)] * q_dec).astype(BF16), s.astype(BF16))
        s = s * c_dec + kvs[c]
    s_s[...] = s
    sf_ref[0] = s
    for c in order:
        o = inter[c]
        if with_intra:
            o = o + _dot((scores[c] * dmask).astype(BF16), v_ref[0, blk(c), :]) + prev_ref[0, blk(c), :].astype(F32)
        o_ref[0, blk(c), :] = o.astype(o_ref.dtype)


def _ret_pass(p, lg, tabs, s0, prev, tt, reverse, rotated=None):
    with_intra = prev is not None
    rotate = rotated is None
    b, l, _ = p.shape
    n_tiles = l // tt
    nh = R_HEADS

    def tile(i):
        return n_tiles - 1 - i if reverse else i

    def col(width, off):
        return pl.BlockSpec((1, tt, width), lambda bh, i, lg_: (bh // nh, tile(i), off // width + bh % nh))

    def whole(a):
        return pl.BlockSpec(a.shape, lambda bh, i, lg_: (0, 0))

    st = pl.BlockSpec((1, R_QK, R_V), lambda bh, i, lg_: (bh, 0, 0))
    o_spec = pl.BlockSpec((1, tt, R_V), lambda bh, i, lg_: (bh // nh, tile(i), bh % nh))
    rot_spec = pl.BlockSpec((1, tt, R_QK), lambda bh, i, lg_: (bh // nh, tile(i), bh % nh))
    rot_sd = jax.ShapeDtypeStruct((b, l, nh * R_QK), BF16)
    qk_specs = [col(R_QK, OFF_Q), col(R_QK, OFF_K)] if rotate else [rot_spec, rot_spec]
    qk_args = [p, p] if rotate else list(rotated)
    kern = functools.partial(_ret_kernel, tt=tt, reverse=reverse, with_intra=with_intra, rotate=rotate, n_heads=nh,
                             n_tiles=n_tiles)
    grid_spec = pltpu.PrefetchScalarGridSpec(
        num_scalar_prefetch=1,
        grid=(b * nh, n_tiles),
        in_specs=qk_specs + [col(R_V, OFF_V)] + [whole(t) for t in tabs] + [st]
        + ([o_spec] if with_intra else []),
        out_specs=[o_spec, st] + ([rot_spec, rot_spec] if rotate else []),
        scratch_shapes=[pltpu.VMEM((R_QK, R_V), F32)],
    )
    return pl.pallas_call(
        kern,
        grid_spec=grid_spec,
        out_shape=[jax.ShapeDtypeStruct((b, l, nh * R_V), BF16),
                   jax.ShapeDtypeStruct((b * nh, R_QK, R_V), F32)] + ([rot_sd, rot_sd] if rotate else []),
        compiler_params=_cparams(("arbitrary", "arbitrary")),
        name="ret_bwd" if reverse else "ret_fwd",
    )(lg, *qk_args, p, *tabs, s0, *([prev] if with_intra else []))


def _ret_ctx_kernel(lg_ref, k_ref, v_ref, sf_ref, sb_ref, *, n_heads):
    h = pl.program_id(0) % n_heads
    k = k_ref[0].astype(F32) * (R_QK ** -0.5)
    v = v_ref[0]
    n = k.shape[0]
    idx = lax.broadcasted_iota(jnp.int32, k.shape, 0).astype(F32)
    sf_ref[0] = _dot_tn((k * jnp.exp(lg_ref[0, h] * (n - 1.0 - idx))).astype(BF16), v)
    sb_ref[0] = _dot_tn((k * jnp.exp(lg_ref[1, h] * idx)).astype(BF16), v)


def _ret_ctx_states(p, lg):
    b, l, _ = p.shape
    nh = R_HEADS
    st = pl.BlockSpec((1, R_QK, R_V), lambda bh, lg_: (bh, 0, 0))
    sd = jax.ShapeDtypeStruct((b * nh, R_QK, R_V), F32)
    grid_spec = pltpu.PrefetchScalarGridSpec(
        num_scalar_prefetch=1,
        grid=(b * nh,),
        in_specs=[pl.BlockSpec((1, l, R_QK), lambda bh, lg_: (bh // nh, 0, OFF_K // R_QK + bh % nh)),
                  pl.BlockSpec((1, l, R_V), lambda bh, lg_: (bh // nh, 0, OFF_V // R_V + bh % nh))],
        out_specs=[st, st],
    )
    return pl.pallas_call(
        functools.partial(_ret_ctx_kernel, n_heads=nh),
        grid_spec=grid_spec,
        out_shape=[sd, sd],
        compiler_params=_cparams(("arbitrary",)),
        name="ret_ctx",
    )(lg, p, p)


def _final_kernel(x_ref, ga_ref, gr_ref, ma_ref, mb_ref, of_ref, ob_ref, bon_ref, or_ref,
                  gate_ref, alw_ref, alb_ref, rlw_ref, rlb_ref, fnw_ref, awo_ref, rwo_ref, wo_ref, o_ref):
    li = lax.broadcasted_iota(jnp.int32, (QUAD, QUAD), 0) // A_HEAD
    lj = lax.broadcasted_iota(jnp.int32, (QUAD, QUAD), 1) // A_HEAD
    msum = (li == lj).astype(BF16)
    inv = 1.0 / A_HEAD

    def head_sum(t):
        return jnp.concatenate([_dot(t[:, q * QUAD:(q + 1) * QUAD].astype(BF16), msum) for q in range(N_QUADS)], axis=1)

    oa = of_ref[0].astype(F32) + ob_ref[0].astype(F32)
    mu = head_sum(oa) * inv
    dv = oa - mu
    var = head_sum(dv * dv) * inv
    ya = dv * lax.rsqrt(var + A_GN_EPS) * alw_ref[...] + alb_ref[...] + bon_ref[0].astype(F32)
    ya = _dot(ya.astype(BF16) * _silu_half(ga_ref[0]), awo_ref[...])

    orr = or_ref[0].astype(F32)
    parts = []
    for h in range(R_HEADS):
        oh = orr[:, h * R_V:(h + 1) * R_V]
        m = jnp.mean(oh, axis=-1, keepdims=True)
        dh = oh - m
        vh = jnp.mean(dh * dh, axis=-1, keepdims=True)
        parts.append(dh * lax.rsqrt(vh + R_GN_EPS))
    yr = jnp.concatenate(parts, axis=1) * rlw_ref[...] + rlb_ref[...]
    yr = _dot(yr.astype(BF16) * _silu_half(gr_ref[0]), rwo_ref[...])

    merged = (1.0 + jnp.tanh(ma_ref[0])) * ya.astype(BF16) + (1.0 + jnp.tanh(mb_ref[0])) * yr.astype(BF16)
    y = _dot(merged, wo_ref[...])
    xo = x_ref[0] + gate_ref[0] * y
    o_ref[0] = xo * lax.rsqrt(jnp.mean(xo * xo, axis=-1, keepdims=True) + NORM_EPS) * fnw_ref[...]


def _final(x, p, of, ob, bonus, o_ret, gate, a_ln_w, a_ln_b, r_ln_w, r_ln_b, final_w, awo, rwo, wo, tm):
    b, l, d = x.shape

    def tok(width, off=0):
        return pl.BlockSpec((1, tm, width), lambda bi, i: (bi, i, off // width))

    def vec(width):
        return pl.BlockSpec((1, width), lambda bi, i: (0, 0))

    def mat(r, c):
        return pl.BlockSpec((r, c), lambda bi, i: (0, 0))

    rv = R_HEADS * R_V
    return pl.pallas_call(
        _final_kernel,
        grid=(b, l // tm),
        in_specs=[tok(d), tok(d, OFF_GA), tok(rv, OFF_GR), tok(d, OFF_MA), tok(d, OFF_MB),
                  tok(d), tok(d), tok(d), tok(rv),
                  pl.BlockSpec((1, 1, d), lambda bi, i: (bi, 0, 0)),
                  vec(d), vec(d), vec(rv), vec(rv), vec(d), mat(d, d), mat(rv, d), mat(d, d)],
        out_specs=tok(d),
        out_shape=jax.ShapeDtypeStruct((b, l, d), F32),
        compiler_params=_cparams(("arbitrary", "arbitrary")),
        name="final",
    )(x, p, p, p, p, of, ob, bonus, o_ret, gate,
      a_ln_w.reshape(1, d), a_ln_b.reshape(1, d), r_ln_w.reshape(1, rv), r_ln_b.reshape(1, rv),
      final_w.reshape(1, d), awo, rwo, wo)


def _rope_tables(l):
    half = R_QK // 4
    freqs = ROPE_BASE ** (-jnp.arange(half, dtype=F32) / half)

    def tab(n):
        ang = jnp.arange(n, dtype=F32)[:, None] * freqs[None, :]
        return (jnp.concatenate([jnp.cos(ang), jnp.cos(ang)], axis=1),
                jnp.concatenate([-jnp.sin(ang), jnp.sin(ang)], axis=1))

    return tab(l // GRID_W) + tab(GRID_W)


def _reorder_w_in(w):
    return jnp.concatenate([w[:, 0:3072], 0.5 * w[:, 3072:4096], w[:, 4352:8448], 0.5 * w[:, 8448:], w[:, 4096:4352]],
                           axis=1)


def _pick_tile(l, pref):
    t = min(l, pref)
    while l % t:
        t //= 2
    return t


def kernel(x, c, ctx, c_ctx, norm_w, ada_w, ada_b, w_in, a_conv, a_w_up, a_w0, a_a_up, a_a0, a_k_k, a_k_a,
           a_r_k, a_ln_w, a_ln_b, a_w_out, r_decay, r_ln_w, r_ln_b, r_w_out, w_out, final_norm_w):
    b, l, d = x.shape
    lc = ctx.shape[1]
    assert d == D_MODEL and l % R_BLOCK == 0 and lc % R_BLOCK == 0
    lyr = 0

    cc = jnp.zeros((16, d), F32).at[:b].set(c).at[b].set(c_ctx)
    mod = _adaln(cc, ada_w[lyr], ada_b[lyr])
    shift, scale, gate = mod[:b, :d], mod[:b, d:2 * d], mod[:b, 2 * d:]
    shift_c = jnp.broadcast_to(mod[b, :d], (b, d))
    scale_c = jnp.broadcast_to(mod[b, d:2 * d], (b, d))

    w_bf = _reorder_w_in(w_in[lyr]).astype(BF16)
    p_x = _inproj(x, (1.0 + scale)[:, None, :], shift[:, None, :], norm_w[lyr], w_bf, _pick_tile(l, 2048))
    p_c = _inproj(ctx.reshape(1, b * lc, d), (1.0 + scale_c)[:1, None, :], shift_c[:1, None, :], norm_w[lyr], w_bf,
                  _pick_tile(b * lc, 2048)).reshape(b, lc, -1)

    zpad = jnp.zeros((A_LORA, d), F32)
    wup_pad = jnp.stack([jnp.concatenate([a_w_up[lyr, 0], zpad], 0), jnp.concatenate([zpad, a_w_up[lyr, 1]], 0)])
    aup_pad = jnp.stack([jnp.concatenate([a_a_up[lyr, 0], zpad], 0), jnp.concatenate([zpad, a_a_up[lyr, 1]], 0)])
    a_args = (a_conv[lyr], a_k_k[lyr], a_k_a[lyr], a_r_k[lyr].reshape(-1), a_w0[lyr], a_a0[lyr], wup_pad, aup_pad)
    h_zero = jnp.zeros((b, A_HEAD, d), F32)
    par_c = _rwkv_par(p_c, *a_args, tt=_pick_tile(lc, 256))
    _, _, hcf, hcb = _rwkv_seq(par_c[:8], h_zero, h_zero, _pick_tile(lc, 512))
    par_x = _rwkv_par(p_x, *a_args, tt=_pick_tile(l, 1024))
    o_af, o_ab, _, _ = _rwkv_seq(par_x[:8], hcf, hcb, _pick_tile(l, 512))
    bonus = par_x[8]

    lg = -jnp.exp(r_decay[lyr].astype(F32))
    tabs_x = _rope_tables(l)
    sc_f, sc_b = _ret_ctx_states(p_c, lg)
    tx = _pick_tile(l, 2048)
    o_rb, _, q_rot, k_rot = _ret_pass(p_x, lg, tabs_x, sc_b, None, tx, True)
    o_r, _ = _ret_pass(p_x, lg, tabs_x, sc_f, o_rb, tx, False, (q_rot, k_rot))

    return _final(x, p_x, o_af, o_ab, bonus, o_r, gate[:, None, :], a_ln_w[lyr], a_ln_b[lyr],
                  r_ln_w[lyr], r_ln_b[lyr], final_norm_w, (0.5 * a_w_out[lyr]).astype(BF16),
                  (0.5 * r_w_out[lyr]).astype(BF16), w_out[lyr].astype(BF16), _pick_tile(l, 512))
```

```python
import functools
import math

import jax
import jax.numpy as jnp
from jax import lax
from jax.experimental import pallas as pl
from jax.experimental.pallas import tpu as pltpu

F32 = jnp.float32
BF16 = jnp.bfloat16

D_MODEL = 1024
A_HEAD = 64
A_HEADS = D_MODEL // A_HEAD
A_LORA = 64
QUAD = 4 * A_HEAD
PAIR = 2 * A_HEAD
N_QUADS = D_MODEL // QUAD
A_CHUNK = 64
SUB = 8
CUM_BLOCK = 256
UNIT_GROUP = 16
R_HEADS = 4
R_QK = 256
R_V = 512
R_BLOCK = 256
GRID_W = 64
ROPE_BASE = 10000.0
NORM_EPS = 1e-6
A_GN_EPS = 64e-5
R_GN_EPS = 1e-5
L2_EPS = 1e-12
DECAY_SCALE = math.exp(-0.5)

OFF_RKV, OFF_GA, OFF_Q, OFF_K, OFF_V, OFF_GR, OFF_MA, OFF_MB, OFF_LO = (
    0, 3072, 4096, 5120, 6144, 8192, 10240, 11264, 12288)
V7X_VMEM_LIMIT = 56 * 1024 * 1024


def _cparams(sem):
    return pltpu.CompilerParams(dimension_semantics=sem, vmem_limit_bytes=V7X_VMEM_LIMIT)


def _dot(a, b):
    return jnp.dot(a, b, preferred_element_type=F32)


def _dot_nt(a, b):
    return lax.dot_general(a, b, (((1,), (1,)), ((), ())), preferred_element_type=F32)


def _dot_tn(a, b):
    return lax.dot_general(a, b, (((0,), (0,)), ((), ())), preferred_element_type=F32)


def _split(x):
    hi = x.astype(BF16)
    lo = (x - hi.astype(F32)).astype(BF16)
    return hi, lo


def _dot_exact_lhs(m, x):
    hi, lo = _split(x)
    return _dot(m, hi) + _dot(m, lo)


def _dot3(a, b):
    ah, al = _split(a)
    bh, bl = _split(b)
    return _dot(ah, bh) + _dot(ah, bl) + _dot(al, bh)


def _sigmoid(x):
    return 0.5 * jnp.tanh(0.5 * x) + 0.5


def _silu(x):
    return x * _sigmoid(x)


def _silu_half(h):
    return h + h * jnp.tanh(h)


def _adaln_kernel(c_ref, w_ref, b_ref, o_ref):
    cs = _silu(c_ref[...])
    o_ref[...] = _dot3(cs, w_ref[...]) + b_ref[...]


def _adaln(cc, ada_w, ada_b):
    n = ada_w.shape[1]
    tn = 512
    return pl.pallas_call(
        _adaln_kernel,
        grid=(n // tn,),
        in_specs=[pl.BlockSpec(cc.shape, lambda j: (0, 0)),
                  pl.BlockSpec((ada_w.shape[0], tn), lambda j: (0, j)),
                  pl.BlockSpec((1, tn), lambda j: (0, j))],
        out_specs=pl.BlockSpec((cc.shape[0], tn), lambda j: (0, j)),
        out_shape=jax.ShapeDtypeStruct((cc.shape[0], n), F32),
        compiler_params=_cparams(("arbitrary",)),
        name="adaln",
    )(cc, ada_w, ada_b.reshape(1, n))


def _inproj_kernel(x_ref, sc_ref, sh_ref, nw_ref, w_ref, o_ref, u_ref):
    @pl.when(pl.program_id(2) == 0)
    def _():
        xf = x_ref[0]
        y = xf * lax.rsqrt(jnp.mean(xf * xf, axis=-1, keepdims=True) + NORM_EPS) * nw_ref[...]
        u_ref[...] = (y * sc_ref[0] + sh_ref[0]).astype(BF16)

    o_ref[0] = _dot(u_ref[...], w_ref[...]).astype(o_ref.dtype)


def _inproj(x, scale1p, shift, norm_w, w_bf16, tm):
    b, l, d = x.shape
    n = w_bf16.shape[1]
    tn = 1792
    return pl.pallas_call(
        _inproj_kernel,
        grid=(b, l // tm, n // tn),
        in_specs=[pl.BlockSpec((1, tm, d), lambda bi, i, j: (bi, i, 0)),
                  pl.BlockSpec((1, 1, d), lambda bi, i, j: (bi, 0, 0)),
                  pl.BlockSpec((1, 1, d), lambda bi, i, j: (bi, 0, 0)),
                  pl.BlockSpec((1, d), lambda bi, i, j: (0, 0)),
                  pl.BlockSpec((d, tn), lambda bi, i, j: (0, j))],
        out_specs=pl.BlockSpec((1, tm, tn), lambda bi, i, j: (bi, i, j)),
        out_shape=jax.ShapeDtypeStruct((b, l, n), BF16),
        scratch_shapes=[pltpu.VMEM((tm, d), BF16)],
        compiler_params=_cparams(("arbitrary", "arbitrary", "arbitrary")),
        name="inproj",
    )(x, scale1p, shift, norm_w.reshape(1, d), w_bf16)


def _head_masks(rows, dtype):
    lane_head = lax.broadcasted_iota(jnp.int32, (rows, PAIR), 1) // A_HEAD
    return [(lane_head == h).astype(dtype) for h in range(2)]


def _bd2(x_bf16, masks):
    return jnp.concatenate([x_bf16 * m for m in masks], axis=0)


def _rwkv_par_kernel(r_ref, k_ref, v_ref, rp_ref, kp_ref, vp_ref, rn_ref, kn_ref, vn_ref, lo_ref,
                     cr_ref, ck_ref, cv_ref, kkw_ref, kaw_ref, rkw_ref, w0_ref, a0_ref, wup_ref, aup_ref,
                     rtf_ref, oif_ref, phf_ref, dlf_ref, rtb_ref, oib_ref, phb_ref, dlb_ref, bon_ref,
                     *, tt, n_tiles):
    i = pl.program_id(1)
    row = lax.broadcasted_iota(jnp.int32, (tt, QUAD), 0)
    has_prev = (i > 0).astype(F32)
    has_next = (i < n_tiles - 1).astype(F32)

    def conv(x_ref, p_ref, n_ref, c_ref):
        x = x_ref[0].astype(F32)
        prev_row = p_ref[0][15:16, :].astype(F32) * has_prev
        next_row = n_ref[0][0:1, :].astype(F32) * has_next
        x_prev = jnp.where(row == 0, prev_row, pltpu.roll(x, 1, axis=0))
        x_next = jnp.where(row == tt - 1, next_row, pltpu.roll(x, tt - 1, axis=0))
        cw = c_ref[...]
        return x_prev * cw[0:1] + x * cw[1:2] + x_next * cw[2:3]

    r = conv(r_ref, rp_ref, rn_ref, cr_ref)
    k = conv(k_ref, kp_ref, kn_ref, ck_ref)
    v = conv(v_ref, vp_ref, vn_ref, cv_ref)

    li = lax.broadcasted_iota(jnp.int32, (QUAD, QUAD), 0) // A_HEAD
    lj = lax.broadcasted_iota(jnp.int32, (QUAD, QUAD), 1) // A_HEAD
    msum = (li == lj).astype(BF16)

    kk = k * kkw_ref[...]
    kk = kk * lax.rsqrt(_dot((kk * kk).astype(BF16), msum) + L2_EPS)

    cb = min(tt, CUM_BLOCK)
    ti = lax.broadcasted_iota(jnp.int32, (cb, cb), 0)
    tj = lax.broadcasted_iota(jnp.int32, (cb, cb), 1)
    same_chunk = (ti // A_CHUNK) == (tj // A_CHUNK)

    lo = lo_ref[0]
    lact = jnp.tanh(lo[:, 0:2 * A_LORA].astype(F32)).astype(BF16)
    la = lo[:, 2 * A_LORA:4 * A_LORA]

    masks_b = _head_masks(A_CHUNK, BF16)
    ct = lax.broadcasted_iota(jnp.int32, (A_CHUNK, PAIR), 0)
    cs = lax.broadcasted_iota(jnp.int32, (A_CHUNK, PAIR), 1) % A_HEAD
    eye_q = (ct == cs).astype(F32)
    lane_c = lax.broadcasted_iota(jnp.int32, (SUB, PAIR), 1)
    blk_of_lane = (lane_c % A_HEAD) // SUB
    blk_f = [(blk_of_lane == b_).astype(F32) for b_ in range(A_CHUNK // SUB)]
    head_blk_b = [((lane_c // A_HEAD == h_) & (blk_of_lane == b_)).astype(BF16)
                  for h_ in range(2) for b_ in range(A_CHUNK // SUB)]
    eye_c = (lax.broadcasted_iota(jnp.int32, (SUB, PAIR), 0) == lane_c % SUB).astype(F32)
    merges_b = [(((ct // (2 * w)) == (cs // (2 * w))) & ((ct // w) != (cs // w))).astype(BF16)
                for w in (SUB, 2 * SUB, 4 * SUB)]

    second_head = (lax.broadcasted_iota(jnp.int32, (tt, QUAD), 1) // A_HEAD) % 2 == 1

    def transposed(x):
        xb = x.astype(BF16)
        ahead = jnp.concatenate([xb[tt - A_CHUNK:], xb[:tt - A_CHUNK]], axis=0)
        return jnp.where(second_head, ahead, xb).T

    bonus = jnp.zeros((tt, QUAD), F32)
    outs = ((rtf_ref, oif_ref, phf_ref, dlf_ref), (rtb_ref, oib_ref, phb_ref, dlb_ref))
    prep = []
    for d in range(2):
        if d == 0:
            tri = (same_chunk & (tj <= ti)).astype(BF16)
            strict = (cs < ct).astype(F32)
            incl = (cs <= ct).astype(F32)
        else:
            tri = (same_chunk & (tj >= ti)).astype(BF16)
            strict = (cs > ct).astype(F32)
            incl = (cs >= ct).astype(F32)
        z = w0_ref[d:d + 1, :] + _dot(lact, wup_ref[d])
        logw = -DECAY_SCALE * _sigmoid(z)
        av = _sigmoid(a0_ref[d:d + 1, :] + _dot(la, aup_ref[d]))
        kd = k * (1.0 + (av - 1.0) * kaw_ref[...])
        bonus = bonus + _dot((r * kd * rkw_ref[...]).astype(BF16), msum) * v

        cum = jnp.concatenate([_dot_exact_lhs(tri, logw[j * cb:(j + 1) * cb]) for j in range(tt // cb)], axis=0)
        last = 0 if d == 1 else A_CHUNK - 1
        ctot = jnp.concatenate([jnp.broadcast_to(cum[c * A_CHUNK + last:c * A_CHUNK + last + 1, :], (A_CHUNK, QUAD))
                                for c in range(tt // A_CHUNK)], axis=0)
        e_inv = jnp.exp(-cum)
        e_g = jnp.exp(ctot - cum)
        kka = kk * av
        prep.append(dict(kh=kk * jnp.exp(cum - logw), rh=r * jnp.exp(cum), gam=jnp.exp(ctot),
                         bt_t=transposed(kka * e_inv), kt_t=transposed(kd * e_inv),
                         bg_t=transposed(kka * e_g), kg_t=transposed(kd * e_g),
                         strict_b=strict.astype(BF16), incl_b=incl.astype(BF16),
                         diag=[strict[b_ * SUB:(b_ + 1) * SUB] * blk_f[b_] for b_ in range(A_CHUNK // SUB)]))

    all_units = [(d, c, pr) for d in range(2) for c in range(tt // A_CHUNK) for pr in range(QUAD // PAIR)]

    def cut(x, c, pr):
        return x[c * A_CHUNK:(c + 1) * A_CHUNK, pr * PAIR:(pr + 1) * PAIR]

    n_cols = tt // PAIR

    def head_masks(c):
        return masks_b if c % 2 == 0 else masks_b[::-1]

    def bd(x, c):
        return _bd2(x.astype(BF16), head_masks(c))

    def cut_t(xt, c, pr):
        m_top, m_bot = head_masks(c)
        col_top = c // 2
        col_bot = c // 2 if c % 2 == 0 else ((c + 1) // 2) % n_cols
        top = xt[pr * PAIR:pr * PAIR + A_HEAD, col_top * PAIR:(col_top + 1) * PAIR] * m_top
        bot = xt[pr * PAIR + A_HEAD:(pr + 1) * PAIR, col_bot * PAIR:(col_bot + 1) * PAIR] * m_bot
        return jnp.concatenate([top, bot], axis=0), top + bot

    def stage_a(d, c, pr):
        pd = prep[d]
        kh, rh = cut(pd["kh"], c, pr), cut(pd["rh"], c, pr)
        lhs1 = jnp.concatenate([kh, rh], axis=0).astype(BF16)
        w_bt, _ = cut_t(pd["bt_t"], c, pr)
        w_kt, _ = cut_t(pd["kt_t"], c, pr)
        a_all = _dot(lhs1, jnp.concatenate([w_bt, w_kt], axis=1))
        a_b = a_all.astype(BF16)
        diag = a_all[0:SUB, 0:PAIR] * pd["diag"][0]
        for b_ in range(1, A_CHUNK // SUB):
            diag = diag + a_all[b_ * SUB:(b_ + 1) * SUB, 0:PAIR] * pd["diag"][b_]
        return dict(kh=kh, rh=rh, diag=diag,
                    a_ab=a_b[0:A_CHUNK, 0:PAIR] * pd["strict_b"],
                    a_krk=jnp.concatenate([a_b[0:A_CHUNK, PAIR:] * pd["strict_b"],
                                           a_b[A_CHUNK:, PAIR:] * pd["incl_b"],
                                           cut_t(pd["kg_t"], c, pr)[1]], axis=0),
                    a_rbb=jnp.concatenate([a_b[A_CHUNK:, 0:PAIR] * pd["incl_b"],
                                           cut_t(pd["bg_t"], c, pr)[1]], axis=0))

    def run_group(units):
        st = [stage_a(*u_) for u_ in units]

        def expand_w(xc):
            xb = xc.astype(BF16)
            return jnp.concatenate([xb * m for m in head_blk_b], axis=0)

        ps = [-s_["diag"] for s_ in st]
        ts = [eye_c + p for p in ps]
        ps = [_dot(p.astype(BF16), expand_w(p)) for p in ps]
        both = [_dot(jnp.concatenate([p, t], axis=0).astype(BF16), expand_w(p)) for p, t in zip(ps, ts)]
        ps = [b_[0:SUB] for b_ in both]
        ts = [t + b_[SUB:] for t, b_ in zip(ts, both)]
        ts = [t + _dot(t.astype(BF16), expand_w(p)) for t, p in zip(ts, ps)]
        ts = [jnp.concatenate([t * m for m in blk_f], axis=0).astype(BF16) for t in ts]
        for blk in merges_b:
            gs = [_dot(t, _bd2(s_["a_ab"] * blk, masks_b)).astype(BF16) for t, s_ in zip(ts, st)]
            ts = [t - _dot(g, _bd2(t, masks_b)).astype(BF16) for t, g in zip(ts, gs)]

        xas = [_dot(s_["a_krk"], bd(cut(v, c, pr), c)) for s_, (d, c, pr) in zip(st, units)]
        wus = [_dot(t, jnp.concatenate([bd(s_["kh"], c), bd(xa[0:A_CHUNK], c)], axis=1))
               for t, s_, xa, (d, c, pr) in zip(ts, st, xas, units)]
        wubs = [wu.astype(BF16) for wu in wus]
        arbs = [_dot(s_["a_rbb"], jnp.concatenate([_bd2(wb[:, 0:PAIR], head_masks(c)),
                                                   _bd2(wb[:, PAIR:], head_masks(c))], axis=1))
                for s_, wb, (d, c, pr) in zip(st, wubs, units)]

        for (d, c, pr), s_, xa, arb in zip(units, st, xas, arbs):
            rt_ref, oi_ref, ph_ref, dl_ref = outs[d]
            gam = cut(prep[d]["gam"], c, pr)
            rw = slice(c * A_CHUNK, (c + 1) * A_CHUNK)
            ln = slice(pr * PAIR, (pr + 1) * PAIR)
            rt_ref[0, rw, ln] = (s_["rh"] - arb[0:A_CHUNK, 0:PAIR]).astype(rt_ref.dtype)
            oi_ref[0, rw, ln] = (xa[A_CHUNK:2 * A_CHUNK] - arb[0:A_CHUNK, PAIR:]).astype(oi_ref.dtype)
            ph_ref[0, rw, ln] = (eye_q * gam - arb[A_CHUNK:, 0:PAIR]).astype(ph_ref.dtype)
            dl_ref[0, rw, ln] = (xa[2 * A_CHUNK:] - arb[A_CHUNK:, PAIR:]).astype(dl_ref.dtype)

    for g_ in range(0, len(all_units), UNIT_GROUP):
        run_group(all_units[g_:g_ + UNIT_GROUP])
    bon_ref[0] = bonus.astype(bon_ref.dtype)


def _rwkv_par(p, a_conv, k_k, k_a, r_k, w0, a0, wup_pad, aup_pad, tt):
    b, l, _ = p.shape
    n_tiles = l // tt
    h16 = tt // 16
    last16 = l // 16 - 1

    def tok(col0):
        return pl.BlockSpec((1, tt, QUAD), lambda bi, i, q: (bi, i, col0 // QUAD + q))

    def prev(col0):
        return pl.BlockSpec((1, 16, QUAD), lambda bi, i, q: (bi, jnp.maximum(i * h16 - 1, 0), col0 // QUAD + q))

    def nxt(col0):
        return pl.BlockSpec((1, 16, QUAD),
                            lambda bi, i, q: (bi, jnp.minimum((i + 1) * h16, last16), col0 // QUAD + q))

    def vec(rows, col0=0):
        return pl.BlockSpec((rows, QUAD), lambda bi, i, q: (0, col0 // QUAD + q))

    out_tok = pl.BlockSpec((1, tt, QUAD), lambda bi, i, q: (bi, i, q))
    out_sd = jax.ShapeDtypeStruct((b, l, D_MODEL), BF16)
    kern = functools.partial(_rwkv_par_kernel, tt=tt, n_tiles=n_tiles)
    return pl.pallas_call(
        kern,
        grid=(b, n_tiles, N_QUADS),
        in_specs=[tok(0), tok(1024), tok(2048), prev(0), prev(1024), prev(2048),
                  nxt(0), nxt(1024), nxt(2048),
                  pl.BlockSpec((1, tt, 4 * A_LORA), lambda bi, i, q: (bi, i, OFF_LO // (4 * A_LORA))),
                  vec(3, 0), vec(3, 1024), vec(3, 2048),
                  vec(1), vec(1), vec(1), vec(2), vec(2),
                  pl.BlockSpec((2, 2 * A_LORA, QUAD), lambda bi, i, q: (0, 0, q)),
                  pl.BlockSpec((2, 2 * A_LORA, QUAD), lambda bi, i, q: (0, 0, q))],
        out_specs=[out_tok] * 9,
        out_shape=[out_sd] * 9,
        compiler_params=_cparams(("arbitrary", "arbitrary", "arbitrary")),
        name="rwkv_par",
    )(p, p, p, p, p, p, p, p, p, p, a_conv, a_conv, a_conv,
      k_k.reshape(1, D_MODEL), k_a.reshape(1, D_MODEL), r_k.reshape(1, D_MODEL), w0, a0, wup_pad, aup_pad)


def _rwkv_seq_kernel(rtf_ref, oif_ref, phf_ref, dlf_ref, rtb_ref, oib_ref, phb_ref, dlb_ref,
                     h0f_ref, h0b_ref, of_ref, ob_ref, hff_ref, hfb_ref, hf_s, hb_s, *, tt):
    i = pl.program_id(1)

    @pl.when(i == 0)
    def _():
        hf_s[...] = h0f_ref[0]
        hb_s[...] = h0b_ref[0]

    masks_b = _head_masks(A_CHUNK, BF16)
    nc = tt // A_CHUNK
    fwd_refs = (rtf_ref, phf_ref, dlf_ref, oif_ref, of_ref)
    bwd_refs = (rtb_ref, phb_ref, dlb_ref, oib_ref, ob_ref)

    n_pairs = D_MODEL // PAIR

    def pair(j):
        return slice(j * PAIR, (j + 1) * PAIR)

    hs = [hf_s[:, pair(j)] for j in range(n_pairs)] + [hb_s[:, pair(j)] for j in range(n_pairs)]
    for c in range(nc):
        chains = [(fwd_refs, c, j) for j in range(n_pairs)] + [(bwd_refs, nc - 1 - c, j) for j in range(n_pairs)]
        boths = []
        for h, (refs, cc, j) in zip(hs, chains):
            sl = slice(cc * A_CHUNK, (cc + 1) * A_CHUNK)
            lhs = jnp.concatenate([refs[0][0, sl, pair(j)], refs[1][0, sl, pair(j)]], axis=0)
            boths.append(_dot(lhs, _bd2(h.astype(BF16), masks_b)))
        new_hs = []
        for both, (refs, cc, j) in zip(boths, chains):
            sl = slice(cc * A_CHUNK, (cc + 1) * A_CHUNK)
            refs[4][0, sl, pair(j)] = (both[0:A_CHUNK] + refs[3][0, sl, pair(j)].astype(F32)).astype(refs[4].dtype)
            new_hs.append(both[A_CHUNK:] + refs[2][0, sl, pair(j)].astype(F32))
        hs = new_hs
    for j in range(n_pairs):
        hf_s[:, pair(j)] = hs[j]
        hb_s[:, pair(j)] = hs[n_pairs + j]
        hff_ref[0, :, pair(j)] = hs[j]
        hfb_ref[0, :, pair(j)] = hs[n_pairs + j]


def _rwkv_seq(par, h0f, h0b, tt):
    rtf, oif, phf, dlf, rtb, oib, phb, dlb = par
    b, l, _ = rtf.shape
    n_tiles = l // tt
    fwd = pl.BlockSpec((1, tt, D_MODEL), lambda bi, i: (bi, i, 0))
    bwd = pl.BlockSpec((1, tt, D_MODEL), lambda bi, i: (bi, n_tiles - 1 - i, 0))
    st = pl.BlockSpec((1, A_HEAD, D_MODEL), lambda bi, i: (bi, 0, 0))
    o_sd = jax.ShapeDtypeStruct((b, l, D_MODEL), BF16)
    h_sd = jax.ShapeDtypeStruct((b, A_HEAD, D_MODEL), F32)
    return pl.pallas_call(
        functools.partial(_rwkv_seq_kernel, tt=tt),
        grid=(b, n_tiles),
        in_specs=[fwd] * 4 + [bwd] * 4 + [st, st],
        out_specs=[fwd, bwd, st, st],
        out_shape=[o_sd, o_sd, h_sd, h_sd],
        scratch_shapes=[pltpu.VMEM((A_HEAD, D_MODEL), F32), pltpu.VMEM((A_HEAD, D_MODEL), F32)],
        compiler_params=_cparams(("arbitrary", "arbitrary")),
        name="rwkv_seq",
    )(rtf, oif, phf, dlf, rtb, oib, phb, dlb, h0f, h0b)


def _ret_kernel(lg_ref, q_ref, k_ref, v_ref, rc_ref, rs_ref, cc_ref, cs_ref, s0_ref, *rest,
                tt, reverse, with_intra, rotate, n_heads, n_tiles):
    rest = list(rest)
    prev_ref = rest.pop(0) if with_intra else None
    o_ref, sf_ref = rest[0:2]
    qr_ref, kr_ref = rest[2:4] if rotate else (None, None)
    s_s = rest[-1]
    bh = pl.program_id(0)
    i = pl.program_id(1)
    h = bh % n_heads

    @pl.when(i == 0)
    def _():
        s_s[...] = s0_ref[0]

    lg_f = lg_ref[0, h]
    lg_b = lg_ref[1, h]
    lg = lg_b if reverse else lg_f

    n_rows = tt // GRID_W
    r0 = (n_tiles - 1 - i if reverse else i) * n_rows

    def by_row(ref):
        return jnp.concatenate([jnp.broadcast_to(ref[pl.ds(r0 + j, 1), :], (GRID_W, 128)) for j in range(n_rows)], axis=0)

    def by_col(ref):
        return jnp.concatenate([ref[...]] * n_rows, axis=0)

    cos_r, sin_r, cos_c, sin_c = by_row(rc_ref), by_row(rs_ref), by_col(cc_ref), by_col(cs_ref)

    def rope(x_ref):
        x = x_ref[0].astype(F32)
        xa, xb = x[:, 0:128], x[:, 128:256]
        return jnp.concatenate([xa * cos_r + pltpu.roll(xa, 64, axis=1) * sin_r,
                                xb * cos_c + pltpu.roll(xb, 64, axis=1) * sin_c], axis=1)

    if not rotate:
        q = q_ref[0].astype(F32)
        k = k_ref[0].astype(F32)
    else:
        q = rope(q_ref)
        k = rope(k_ref) * (R_QK ** -0.5)
        qr_ref[0] = q.astype(qr_ref.dtype)
        kr_ref[0] = k.astype(kr_ref.dtype)

    idx = lax.broadcasted_iota(jnp.int32, (R_BLOCK, R_QK), 0).astype(F32)
    if reverse:
        q_dec = jnp.exp(lg * (R_BLOCK - idx))
        k_dec = jnp.exp(lg * idx)
    else:
        q_dec = jnp.exp(lg * (idx + 1.0))
        k_dec = jnp.exp(lg * (R_BLOCK - 1.0 - idx))
    c_dec = jnp.exp(lg * R_BLOCK)

    nc = tt // R_BLOCK
    order = [nc - 1 - cc if reverse else cc for cc in range(nc)]

    def blk(c):
        return slice(c * R_BLOCK, (c + 1) * R_BLOCK)

    if with_intra:
        di = lax.broadcasted_iota(jnp.int32, (R_BLOCK, R_BLOCK), 0)
        dj = lax.broadcasted_iota(jnp.int32, (R_BLOCK, R_BLOCK), 1)
        diff = (di - dj).astype(F32)
        dmask = (jnp.where(diff >= 0, jnp.exp(lg_f * jnp.maximum(diff, 0.0)), 0.0)
                 + jnp.where(diff <= 0, jnp.exp(lg_b * jnp.maximum(-diff, 0.0)), 0.0))
        scores = {c: _dot_nt(q[blk(c)].astype(BF16), k[blk(c)].astype(BF16)) for c in order}
    kvs = {c: _dot_tn((k[blk(c)] * k_dec).astype(BF16), v_ref[0, blk(c), :]) for c in order}

    s = s_s[...]
    inter = {}
    for c in order:
        inter[c] = _dot((q[blk(c)] * q_dec).astype(BF16), s.astype(BF16))
        s = s * c_dec + kvs[c]
    s_s[...] = s
    sf_ref[0] = s
    for c in order:
        o = inter[c]
        if with_intra:
            o = o + _dot((scores[c] * dmask).astype(BF16), v_ref[0, blk(c), :]) + prev_ref[0, blk(c), :].astype(F32)
        o_ref[0, blk(c), :] = o.astype(o_ref.dtype)


def _ret_pass(p, lg, tabs, s0, prev, tt, reverse, rotated=None):
    with_intra = prev is not None
    rotate = rotated is None
    b, l, _ = p.shape
    n_tiles = l // tt
    nh = R_HEADS

    def tile(i):
        return n_tiles - 1 - i if reverse else i

    def col(width, off):
        return pl.BlockSpec((1, tt, width), lambda bh, i, lg_: (bh // nh, tile(i), off // width + bh % nh))

    def whole(a):
        return pl.BlockSpec(a.shape, lambda bh, i, lg_: (0, 0))

    st = pl.BlockSpec((1, R_QK, R_V), lambda bh, i, lg_: (bh, 0, 0))
    o_spec = pl.BlockSpec((1, tt, R_V), lambda bh, i, lg_: (bh // nh, tile(i), bh % nh))
    rot_spec = pl.BlockSpec((1, tt, R_QK), lambda bh, i, lg_: (bh // nh, tile(i), bh % nh))
    rot_sd = jax.ShapeDtypeStruct((b, l, nh * R_QK), BF16)
    qk_specs = [col(R_QK, OFF_Q), col(R_QK, OFF_K)] if rotate else [rot_spec, rot_spec]
    qk_args = [p, p] if rotate else list(rotated)
    kern = functools.partial(_ret_kernel, tt=tt, reverse=reverse, with_intra=with_intra, rotate=rotate, n_heads=nh,
                             n_tiles=n_tiles)
    grid_spec = pltpu.PrefetchScalarGridSpec(
        num_scalar_prefetch=1,
        grid=(b * nh, n_tiles),
        in_specs=qk_specs + [col(R_V, OFF_V)] + [whole(t) for t in tabs] + [st]
        + ([o_spec] if with_intra else []),
        out_specs=[o_spec, st] + ([rot_spec, rot_spec] if rotate else []),
        scratch_shapes=[pltpu.VMEM((R_QK, R_V), F32)],
    )
    return pl.pallas_call(
        kern,
        grid_spec=grid_spec,
        out_shape=[jax.ShapeDtypeStruct((b, l, nh * R_V), BF16),
                   jax.ShapeDtypeStruct((b * nh, R_QK, R_V), F32)] + ([rot_sd, rot_sd] if rotate else []),
        compiler_params=_cparams(("arbitrary", "arbitrary")),
        name="ret_bwd" if reverse else "ret_fwd",
    )(lg, *qk_args, p, *tabs, s0, *([prev] if with_intra else []))


def _ret_ctx_kernel(lg_ref, k_ref, v_ref, sf_ref, sb_ref, *, n_heads):
    h = pl.program_id(0) % n_heads
    k = k_ref[0].astype(F32) * (R_QK ** -0.5)
    v = v_ref[0]
    n = k.shape[0]
    idx = lax.broadcasted_iota(jnp.int32, k.shape, 0).astype(F32)
    sf_ref[0] = _dot_tn((k * jnp.exp(lg_ref[0, h] * (n - 1.0 - idx))).astype(BF16), v)
    sb_ref[0] = _dot_tn((k * jnp.exp(lg_ref[1, h] * idx)).astype(BF16), v)


def _ret_ctx_states(p, lg):
    b, l, _ = p.shape
    nh = R_HEADS
    st = pl.BlockSpec((1, R_QK, R_V), lambda bh, lg_: (bh, 0, 0))
    sd = jax.ShapeDtypeStruct((b * nh, R_QK, R_V), F32)
    grid_spec = pltpu.PrefetchScalarGridSpec(
        num_scalar_prefetch=1,
        grid=(b * nh,),
        in_specs=[pl.BlockSpec((1, l, R_QK), lambda bh, lg_: (bh // nh, 0, OFF_K // R_QK + bh % nh)),
                  pl.BlockSpec((1, l, R_V), lambda bh, lg_: (bh // nh, 0, OFF_V // R_V + bh % nh))],
        out_specs=[st, st],
    )
    return pl.pallas_call(
        functools.partial(_ret_ctx_kernel, n_heads=nh),
        grid_spec=grid_spec,
        out_shape=[sd, sd],
        compiler_params=_cparams(("arbitrary",)),
        name="ret_ctx",
    )(lg, p, p)


def _final_kernel(x_ref, ga_ref, gr_ref, ma_ref, mb_ref, of_ref, ob_ref, bon_ref, or_ref,
                  gate_ref, alw_ref, alb_ref, rlw_ref, rlb_ref, fnw_ref, awo_ref, rwo_ref, wo_ref, o_ref):
    li = lax.broadcasted_iota(jnp.int32, (QUAD, QUAD), 0) // A_HEAD
    lj = lax.broadcasted_iota(jnp.int32, (QUAD, QUAD), 1) // A_HEAD
    msum = (li == lj).astype(BF16)
    inv = 1.0 / A_HEAD

    def head_sum(t):
        return jnp.concatenate([_dot(t[:, q * QUAD:(q + 1) * QUAD].astype(BF16), msum) for q in range(N_QUADS)], axis=1)

    oa = of_ref[0].astype(F32) + ob_ref[0].astype(F32)
    mu = head_sum(oa) * inv
    dv = oa - mu
    var = head_sum(dv * dv) * inv
    ya = dv * lax.rsqrt(var + A_GN_EPS) * alw_ref[...] + alb_ref[...] + bon_ref[0].astype(F32)
    ya = _dot(ya.astype(BF16) * _silu_half(ga_ref[0]), awo_ref[...])

    orr = or_ref[0].astype(F32)
    parts = []
    for h in range(R_HEADS):
        oh = orr[:, h * R_V:(h + 1) * R_V]
        m = jnp.mean(oh, axis=-1, keepdims=True)
        dh = oh - m
        vh = jnp.mean(dh * dh, axis=-1, keepdims=True)
        parts.append(dh * lax.rsqrt(vh + R_GN_EPS))
    yr = jnp.concatenate(parts, axis=1) * rlw_ref[...] + rlb_ref[...]
    yr = _dot(yr.astype(BF16) * _silu_half(gr_ref[0]), rwo_ref[...])

    merged = (1.0 + jnp.tanh(ma_ref[0])) * ya.astype(BF16) + (1.0 + jnp.tanh(mb_ref[0])) * yr.astype(BF16)
    y = _dot(merged, wo_ref[...])
    xo = x_ref[0] + gate_ref[0] * y
    o_ref[0] = xo * lax.rsqrt(jnp.mean(xo * xo, axis=-1, keepdims=True) + NORM_EPS) * fnw_ref[...]


def _final(x, p, of, ob, bonus, o_ret, gate, a_ln_w, a_ln_b, r_ln_w, r_ln_b, final_w, awo, rwo, wo, tm):
    b, l, d = x.shape

    def tok(width, off=0):
        return pl.BlockSpec((1, tm, width), lambda bi, i: (bi, i, off // width))

    def vec(width):
        return pl.BlockSpec((1, width), lambda bi, i: (0, 0))

    def mat(r, c):
        return pl.BlockSpec((r, c), lambda bi, i: (0, 0))

    rv = R_HEADS * R_V
    return pl.pallas_call(
        _final_kernel,
        grid=(b, l // tm),
        in_specs=[tok(d), tok(d, OFF_GA), tok(rv, OFF_GR), tok(d, OFF_MA), tok(d, OFF_MB),
                  tok(d), tok(d), tok(d), tok(rv),
                  pl.BlockSpec((1, 1, d), lambda bi, i: (bi, 0, 0)),
                  vec(d), vec(d), vec(rv), vec(rv), vec(d), mat(d, d), mat(rv, d), mat(d, d)],
        out_specs=tok(d),
        out_shape=jax.ShapeDtypeStruct((b, l, d), F32),
        compiler_params=_cparams(("arbitrary", "arbitrary")),
        name="final",
    )(x, p, p, p, p, of, ob, bonus, o_ret, gate,
      a_ln_w.reshape(1, d), a_ln_b.reshape(1, d), r_ln_w.reshape(1, rv), r_ln_b.reshape(1, rv),
      final_w.reshape(1, d), awo, rwo, wo)


def _rope_tables(l):
    half = R_QK // 4
    freqs = ROPE_BASE ** (-jnp.arange(half, dtype=F32) / half)

    def tab(n):
        ang = jnp.arange(n, dtype=F32)[:, None] * freqs[None, :]
        return (jnp.concatenate([jnp.cos(ang), jnp.cos(ang)], axis=1),
                jnp.concatenate([-jnp.sin(ang), jnp.sin(ang)], axis=1))

    return tab(l // GRID_W) + tab(GRID_W)


def _reorder_w_in(w):
    return jnp.concatenate([w[:, 0:3072], 0.5 * w[:, 3072:4096], w[:, 4352:8448], 0.5 * w[:, 8448:], w[:, 4096:4352]],
                           axis=1)


def _pick_tile(l, pref):
    t = min(l, pref)
    while l % t:
        t //= 2
    return t


def kernel(x, c, ctx, c_ctx, norm_w, ada_w, ada_b, w_in, a_conv, a_w_up, a_w0, a_a_up, a_a0, a_k_k, a_k_a,
           a_r_k, a_ln_w, a_ln_b, a_w_out, r_decay, r_ln_w, r_ln_b, r_w_out, w_out, final_norm_w):
    b, l, d = x.shape
    lc = ctx.shape[1]
    assert d == D_MODEL and l % R_BLOCK == 0 and lc % R_BLOCK == 0
    lyr = 0

    cc = jnp.zeros((16, d), F32).at[:b].set(c).at[b].set(c_ctx)
    mod = _adaln(cc, ada_w[lyr], ada_b[lyr])
    shift, scale, gate = mod[:b, :d], mod[:b, d:2 * d], mod[:b, 2 * d:]
    shift_c = jnp.broadcast_to(mod[b, :d], (b, d))
    scale_c = jnp.broadcast_to(mod[b, d:2 * d], (b, d))

    w_bf = _reorder_w_in(w_in[lyr]).astype(BF16)
    p_x = _inproj(x, (1.0 + scale)[:, None, :], shift[:, None, :], norm_w[lyr], w_bf, _pick_tile(l, 2048))
    p_c = _inproj(ctx.reshape(1, b * lc, d), (1.0 + scale_c)[:1, None, :], shift_c[:1, None, :], norm_w[lyr], w_bf,
                  _pick_tile(b * lc, 2048)).reshape(b, lc, -1)

    zpad = jnp.zeros((A_LORA, d), F32)
    wup_pad = jnp.stack([jnp.concatenate([a_w_up[lyr, 0], zpad], 0), jnp.concatenate([zpad, a_w_up[lyr, 1]], 0)])
    aup_pad = jnp.stack([jnp.concatenate([a_a_up[lyr, 0], zpad], 0), jnp.concatenate([zpad, a_a_up[lyr, 1]], 0)])
    a_args = (a_conv[lyr], a_k_k[lyr], a_k_a[lyr], a_r_k[lyr].reshape(-1), a_w0[lyr], a_a0[lyr],
              wup_pad.astype(BF16), aup_pad.astype(BF16))
    h_zero = jnp.zeros((b, A_HEAD, d), F32)
    par_c = _rwkv_par(p_c, *a_args, tt=_pick_tile(lc, 256))
    _, _, hcf, hcb = _rwkv_seq(par_c[:8], h_zero, h_zero, _pick_tile(lc, 512))
    par_x = _rwkv_par(p_x, *a_args, tt=_pick_tile(l, 1024))
    o_af, o_ab, _, _ = _rwkv_seq(par_x[:8], hcf, hcb, _pick_tile(l, 512))
    bonus = par_x[8]

    lg = -jnp.exp(r_decay[lyr].astype(F32))
    tabs_x = _rope_tables(l)
    sc_f, sc_b = _ret_ctx_states(p_c, lg)
    tx = _pick_tile(l, 2048)
    o_rb, _, q_rot, k_rot = _ret_pass(p_x, lg, tabs_x, sc_b, None, tx, True)
    o_r, _ = _ret_pass(p_x, lg, tabs_x, sc_f, o_rb, tx, False, (q_rot, k_rot))

    return _final(x, p_x, o_af, o_ab, bonus, o_r, gate[:, None, :], a_ln_w[lyr], a_ln_b[lyr],
                  r_ln_w[lyr], r_ln_b[lyr], final_norm_w, (0.5 * a_w_out[lyr]).astype(BF16),
                  (0.5 * r_w_out[lyr]).astype(BF16), w_out[lyr].astype(BF16), _pick_tile(l, 512))
```

```python
import functools
import math

import jax
import jax.numpy as jnp
from jax import lax
from jax.experimental import pallas as pl
from jax.experimental.pallas import tpu as pltpu

F32 = jnp.float32
BF16 = jnp.bfloat16

D_MODEL = 1024
A_HEAD = 64
A_HEADS = D_MODEL // A_HEAD
A_LORA = 64
QUAD = 4 * A_HEAD
PAIR = 2 * A_HEAD
N_QUADS = D_MODEL // QUAD
A_CHUNK = 64
SUB = 8
CUM_BLOCK = 256
UNIT_GROUP = 16
R_HEADS = 4
R_QK = 256
R_V = 512
R_BLOCK = 256
GRID_W = 64
ROPE_BASE = 10000.0
NORM_EPS = 1e-6
A_GN_EPS = 64e-5
R_GN_EPS = 1e-5
L2_EPS = 1e-12
DECAY_SCALE = math.exp(-0.5)

OFF_RKV, OFF_GA, OFF_Q, OFF_K, OFF_V, OFF_GR, OFF_MA, OFF_MB, OFF_LO = (
    0, 3072, 4096, 5120, 6144, 8192, 10240, 11264, 12288)
V7X_VMEM_LIMIT = 56 * 1024 * 1024


def _cparams(sem):
    return pltpu.CompilerParams(dimension_semantics=sem, vmem_limit_bytes=V7X_VMEM_LIMIT)


def _dot(a, b):
    return jnp.dot(a, b, preferred_element_type=F32)


def _dot_nt(a, b):
    return lax.dot_general(a, b, (((1,), (1,)), ((), ())), preferred_element_type=F32)


def _dot_tn(a, b):
    return lax.dot_general(a, b, (((0,), (0,)), ((), ())), preferred_element_type=F32)


def _split(x):
    hi = x.astype(BF16)
    lo = (x - hi.astype(F32)).astype(BF16)
    return hi, lo


def _dot_exact_lhs(m, x):
    hi, lo = _split(x)
    return _dot(m, hi) + _dot(m, lo)


def _dot3(a, b):
    ah, al = _split(a)
    bh, bl = _split(b)
    return _dot(ah, bh) + _dot(ah, bl) + _dot(al, bh)


def _sigmoid(x):
    return 0.5 * jnp.tanh(0.5 * x) + 0.5


def _silu(x):
    return x * _sigmoid(x)


def _silu_half(h):
    return h + h * jnp.tanh(h)


def _adaln_kernel(c_ref, w_ref, b_ref, o_ref):
    cs = _silu(c_ref[...])
    o_ref[...] = _dot3(cs, w_ref[...]) + b_ref[...]


def _adaln(cc, ada_w, ada_b):
    n = ada_w.shape[1]
    tn = 512
    return pl.pallas_call(
        _adaln_kernel,
        grid=(n // tn,),
        in_specs=[pl.BlockSpec(cc.shape, lambda j: (0, 0)),
                  pl.BlockSpec((ada_w.shape[0], tn), lambda j: (0, j)),
                  pl.BlockSpec((1, tn), lambda j: (0, j))],
        out_specs=pl.BlockSpec((cc.shape[0], tn), lambda j: (0, j)),
        out_shape=jax.ShapeDtypeStruct((cc.shape[0], n), F32),
        compiler_params=_cparams(("arbitrary",)),
        name="adaln",
    )(cc, ada_w, ada_b.reshape(1, n))


def _inproj_kernel(x_ref, sc_ref, sh_ref, nw_ref, w_ref, o_ref, u_ref):
    @pl.when(pl.program_id(2) == 0)
    def _():
        xf = x_ref[0]
        y = xf * lax.rsqrt(jnp.mean(xf * xf, axis=-1, keepdims=True) + NORM_EPS) * nw_ref[...]
        u_ref[...] = (y * sc_ref[0] + sh_ref[0]).astype(BF16)

    o_ref[0] = _dot(u_ref[...], w_ref[...]).astype(o_ref.dtype)


def _inproj(x, scale1p, shift, norm_w, w_bf16, tm):
    b, l, d = x.shape
    n = w_bf16.shape[1]
    tn = 1792
    return pl.pallas_call(
        _inproj_kernel,
        grid=(b, l // tm, n // tn),
        in_specs=[pl.BlockSpec((1, tm, d), lambda bi, i, j: (bi, i, 0)),
                  pl.BlockSpec((1, 1, d), lambda bi, i, j: (bi, 0, 0)),
                  pl.BlockSpec((1, 1, d), lambda bi, i, j: (bi, 0, 0)),
                  pl.BlockSpec((1, d), lambda bi, i, j: (0, 0)),
                  pl.BlockSpec((d, tn), lambda bi, i, j: (0, j))],
        out_specs=pl.BlockSpec((1, tm, tn), lambda bi, i, j: (bi, i, j)),
        out_shape=jax.ShapeDtypeStruct((b, l, n), BF16),
        scratch_shapes=[pltpu.VMEM((tm, d), BF16)],
        compiler_params=_cparams(("arbitrary", "arbitrary", "arbitrary")),
        name="inproj",
    )(x, scale1p, shift, norm_w.reshape(1, d), w_bf16)


def _head_masks(rows, dtype):
    lane_head = lax.broadcasted_iota(jnp.int32, (rows, PAIR), 1) // A_HEAD
    return [(lane_head == h).astype(dtype) for h in range(2)]


def _bd2(x_bf16, masks):
    return jnp.concatenate([x_bf16 * m for m in masks], axis=0)


def _rwkv_par_kernel(r_ref, k_ref, v_ref, rp_ref, kp_ref, vp_ref, rn_ref, kn_ref, vn_ref, lo_ref,
                     cr_ref, ck_ref, cv_ref, kkw_ref, kaw_ref, rkw_ref, w0_ref, a0_ref, wup_ref, aup_ref,
                     rtf_ref, oif_ref, phf_ref, dlf_ref, rtb_ref, oib_ref, phb_ref, dlb_ref, bon_ref,
                     *, tt, n_tiles):
    i = pl.program_id(1)
    row8 = lax.broadcasted_iota(jnp.int32, (8, QUAD), 0)
    has_prev = (i > 0).astype(F32)
    has_next = (i < n_tiles - 1).astype(F32)

    def conv(x_ref, p_ref, n_ref, c_ref):
        x = x_ref[0].astype(F32)
        prev_row = p_ref[0][15:16, :].astype(F32) * has_prev
        next_row = n_ref[0][0:1, :].astype(F32) * has_next
        dn, up = pltpu.roll(x, 1, axis=0), pltpu.roll(x, tt - 1, axis=0)
        x_prev = jnp.concatenate([jnp.where(row8 == 0, prev_row, dn[0:8]), dn[8:]], axis=0)
        x_next = jnp.concatenate([up[:tt - 8], jnp.where(row8 == 7, next_row, up[tt - 8:])], axis=0)
        cw = c_ref[...]
        return x_prev * cw[0:1] + x * cw[1:2] + x_next * cw[2:3]

    r = conv(r_ref, rp_ref, rn_ref, cr_ref)
    k = conv(k_ref, kp_ref, kn_ref, ck_ref)
    v = conv(v_ref, vp_ref, vn_ref, cv_ref)

    li = lax.broadcasted_iota(jnp.int32, (QUAD, QUAD), 0) // A_HEAD
    lj = lax.broadcasted_iota(jnp.int32, (QUAD, QUAD), 1) // A_HEAD
    msum = (li == lj).astype(BF16)

    k_ka = k * kaw_ref[...]
    kk = k * kkw_ref[...]
    kk = kk * lax.rsqrt(_dot((kk * kk).astype(BF16), msum) + L2_EPS)

    cb = min(tt, CUM_BLOCK)
    ti = lax.broadcasted_iota(jnp.int32, (cb, cb), 0)
    tj = lax.broadcasted_iota(jnp.int32, (cb, cb), 1)
    same_chunk = (ti // A_CHUNK) == (tj // A_CHUNK)

    lo = lo_ref[0]
    lact = jnp.tanh(lo[:, 0:2 * A_LORA].astype(F32)).astype(BF16)
    la = lo[:, 2 * A_LORA:4 * A_LORA]

    masks_b = _head_masks(A_CHUNK, BF16)
    ct = lax.broadcasted_iota(jnp.int32, (A_CHUNK, PAIR), 0)
    cs = lax.broadcasted_iota(jnp.int32, (A_CHUNK, PAIR), 1) % A_HEAD
    eye_q = (ct == cs).astype(F32)
    lane_c = lax.broadcasted_iota(jnp.int32, (SUB, PAIR), 1)
    blk_of_lane = (lane_c % A_HEAD) // SUB
    blk_f = [(blk_of_lane == b_).astype(F32) for b_ in range(A_CHUNK // SUB)]
    head_blk_b = [((lane_c // A_HEAD == h_) & (blk_of_lane == b_)).astype(BF16)
                  for h_ in range(2) for b_ in range(A_CHUNK // SUB)]
    eye_c = (lax.broadcasted_iota(jnp.int32, (SUB, PAIR), 0) == lane_c % SUB).astype(F32)
    merges_b = [(((ct // (2 * w)) == (cs // (2 * w))) & ((ct // w) != (cs // w))).astype(BF16)
                for w in (SUB, 2 * SUB, 4 * SUB)]

    second_head = (lax.broadcasted_iota(jnp.int32, (tt, QUAD), 1) // A_HEAD) % 2 == 1

    def transposed(x):
        xb = x.astype(BF16)
        ahead = jnp.concatenate([xb[tt - A_CHUNK:], xb[:tt - A_CHUNK]], axis=0)
        return jnp.where(second_head, ahead, xb).T

    outs = ((rtf_ref, oif_ref, phf_ref, dlf_ref), (rtb_ref, oib_ref, phb_ref, dlb_ref))
    prep = {}

    def prep_dir(d):
        if d == 0:
            tri = (same_chunk & (tj <= ti)).astype(BF16)
            strict = (cs < ct).astype(F32)
            incl = (cs <= ct).astype(F32)
        else:
            tri = (same_chunk & (tj >= ti)).astype(BF16)
            strict = (cs > ct).astype(F32)
            incl = (cs >= ct).astype(F32)
        logw = (-0.5 * DECAY_SCALE) * jnp.tanh(w0_ref[d:d + 1, :] + _dot(lact, wup_ref[d])) - 0.5 * DECAY_SCALE
        av_m1 = 0.5 * jnp.tanh(a0_ref[d:d + 1, :] + _dot(la, aup_ref[d])) - 0.5
        av = av_m1 + 1.0
        kd = k + k_ka * av_m1

        cum = jnp.concatenate([_dot_exact_lhs(tri, logw[j * cb:(j + 1) * cb]) for j in range(tt // cb)], axis=0)
        last = 0 if d == 1 else A_CHUNK - 1
        e_cum = jnp.exp(cum)
        e_inv = 1.0 / e_cum
        gam = jnp.concatenate([jnp.broadcast_to(e_cum[c * A_CHUNK + last:c * A_CHUNK + last + 1, :], (A_CHUNK, QUAD))
                               for c in range(tt // A_CHUNK)], axis=0)
        e_g = gam * e_inv
        kka = kk * av
        prep[d] = (dict(kh=kk * jnp.exp(cum - logw), rh=r * e_cum, gam=gam,
                         bt_t=transposed(kka * e_inv), kt_t=transposed(kd * e_inv),
                         bg_t=transposed(kka * e_g), kg_t=transposed(kd * e_g),
                         strict_b=strict.astype(BF16), incl_b=incl.astype(BF16),
                         diag=[strict[b_ * SUB:(b_ + 1) * SUB] * blk_f[b_] for b_ in range(A_CHUNK // SUB)]))
        return kd

    def cut(x, c, pr):
        return x[c * A_CHUNK:(c + 1) * A_CHUNK, pr * PAIR:(pr + 1) * PAIR]

    n_cols = tt // PAIR

    def head_masks(c):
        return masks_b if c % 2 == 0 else masks_b[::-1]

    def bd(x, c):
        return _bd2(x.astype(BF16), head_masks(c))

    def cut_t(xt, c, pr):
        m_top, m_bot = head_masks(c)
        col_top = c // 2
        col_bot = c // 2 if c % 2 == 0 else ((c + 1) // 2) % n_cols
        top = xt[pr * PAIR:pr * PAIR + A_HEAD, col_top * PAIR:(col_top + 1) * PAIR] * m_top
        bot = xt[pr * PAIR + A_HEAD:(pr + 1) * PAIR, col_bot * PAIR:(col_bot + 1) * PAIR] * m_bot
        return jnp.concatenate([top, bot], axis=0), top + bot

    def stage_a(d, c, pr):
        pd = prep[d]
        kh, rh = cut(pd["kh"], c, pr), cut(pd["rh"], c, pr)
        lhs1 = jnp.concatenate([kh, rh], axis=0).astype(BF16)
        w_bt, _ = cut_t(pd["bt_t"], c, pr)
        w_kt, _ = cut_t(pd["kt_t"], c, pr)
        a_all = _dot(lhs1, jnp.concatenate([w_bt, w_kt], axis=1))
        a_b = a_all.astype(BF16)
        diag = a_all[0:SUB, 0:PAIR] * pd["diag"][0]
        for b_ in range(1, A_CHUNK // SUB):
            diag = diag + a_all[b_ * SUB:(b_ + 1) * SUB, 0:PAIR] * pd["diag"][b_]
        return dict(kh=kh, rh=rh, diag=diag,
                    a_ab=a_b[0:A_CHUNK, 0:PAIR] * pd["strict_b"],
                    a_krk=jnp.concatenate([a_b[0:A_CHUNK, PAIR:] * pd["strict_b"],
                                           a_b[A_CHUNK:, PAIR:] * pd["incl_b"],
                                           cut_t(pd["kg_t"], c, pr)[1]], axis=0),
                    a_rbb=jnp.concatenate([a_b[A_CHUNK:, 0:PAIR] * pd["incl_b"],
                                           cut_t(pd["bg_t"], c, pr)[1]], axis=0))

    def run_group(units):
        st = [stage_a(*u_) for u_ in units]

        def expand_w(xc):
            xb = xc.astype(BF16)
            return jnp.concatenate([xb * m for m in head_blk_b], axis=0)

        ps = [-s_["diag"] for s_ in st]
        ts = [eye_c + p for p in ps]
        ps = [_dot(p.astype(BF16), expand_w(p)) for p in ps]
        both = [_dot(jnp.concatenate([p, t], axis=0).astype(BF16), expand_w(p)) for p, t in zip(ps, ts)]
        ps = [b_[0:SUB] for b_ in both]
        ts = [t + b_[SUB:] for t, b_ in zip(ts, both)]
        ts = [t + _dot(t.astype(BF16), expand_w(p)) for t, p in zip(ts, ps)]
        ts = [jnp.concatenate([t * m for m in blk_f], axis=0).astype(BF16) for t in ts]
        for blk in merges_b:
            gs = [_dot(t, _bd2(s_["a_ab"] * blk, masks_b)).astype(BF16) for t, s_ in zip(ts, st)]
            ts = [t - _dot(g, _bd2(t, masks_b)).astype(BF16) for t, g in zip(ts, gs)]

        xas = [_dot(s_["a_krk"], bd(cut(v, c, pr), c)) for s_, (d, c, pr) in zip(st, units)]
        wus = [_dot(t, jnp.concatenate([bd(s_["kh"], c), bd(xa[0:A_CHUNK], c)], axis=1))
               for t, s_, xa, (d, c, pr) in zip(ts, st, xas, units)]
        wubs = [wu.astype(BF16) for wu in wus]
        arbs = [_dot(s_["a_rbb"], jnp.concatenate([_bd2(wb[:, 0:PAIR], head_masks(c)),
                                                   _bd2(wb[:, PAIR:], head_masks(c))], axis=1))
                for s_, wb, (d, c, pr) in zip(st, wubs, units)]

        for (d, c, pr), s_, xa, arb in zip(units, st, xas, arbs):
            rt_ref, oi_ref, ph_ref, dl_ref = outs[d]
            gam = cut(prep[d]["gam"], c, pr)
            rw = slice(c * A_CHUNK, (c + 1) * A_CHUNK)
            ln = slice(pr * PAIR, (pr + 1) * PAIR)
            rt_ref[0, rw, ln] = (s_["rh"] - arb[0:A_CHUNK, 0:PAIR]).astype(rt_ref.dtype)
            oi_ref[0, rw, ln] = (xa[A_CHUNK:2 * A_CHUNK] - arb[0:A_CHUNK, PAIR:]).astype(oi_ref.dtype)
            ph_ref[0, rw, ln] = (eye_q * gam - arb[A_CHUNK:, 0:PAIR]).astype(ph_ref.dtype)
            dl_ref[0, rw, ln] = (xa[2 * A_CHUNK:] - arb[A_CHUNK:, PAIR:]).astype(dl_ref.dtype)

    n_chunks = tt // A_CHUNK
    pairs = QUAD // PAIR
    kd_sum = prep_dir(0)
    if 2 * n_chunks * pairs <= UNIT_GROUP:
        kd_sum = kd_sum + prep_dir(1)
        run_group([(d, c, pr) for d in range(2) for c in range(n_chunks) for pr in range(pairs)])
    else:
        per = UNIT_GROUP // pairs
        groups = [[(d, c, pr) for c in range(c0, min(c0 + per, n_chunks)) for pr in range(pairs)]
                  for d in range(2) for c0 in range(0, n_chunks, per)]
        for n_, units in enumerate(groups):
            if n_ == 1:
                kd_sum = kd_sum + prep_dir(1)
            run_group(units)
    bon_ref[0] = (_dot((r * kd_sum * rkw_ref[...]).astype(BF16), msum) * v).astype(bon_ref.dtype)


def _rwkv_par(p, a_conv, k_k, k_a, r_k, w0, a0, wup_pad, aup_pad, tt):
    b, l, _ = p.shape
    n_tiles = l // tt
    h16 = tt // 16
    last16 = l // 16 - 1

    def tok(col0):
        return pl.BlockSpec((1, tt, QUAD), lambda bi, i, q: (bi, i, col0 // QUAD + q))

    def prev(col0):
        return pl.BlockSpec((1, 16, QUAD), lambda bi, i, q: (bi, jnp.maximum(i * h16 - 1, 0), col0 // QUAD + q))

    def nxt(col0):
        return pl.BlockSpec((1, 16, QUAD),
                            lambda bi, i, q: (bi, jnp.minimum((i + 1) * h16, last16), col0 // QUAD + q))

    def vec(rows, col0=0):
        return pl.BlockSpec((rows, QUAD), lambda bi, i, q: (0, col0 // QUAD + q))

    out_tok = pl.BlockSpec((1, tt, QUAD), lambda bi, i, q: (bi, i, q))
    out_sd = jax.ShapeDtypeStruct((b, l, D_MODEL), BF16)
    kern = functools.partial(_rwkv_par_kernel, tt=tt, n_tiles=n_tiles)
    return pl.pallas_call(
        kern,
        grid=(b, n_tiles, N_QUADS),
        in_specs=[tok(0), tok(1024), tok(2048), prev(0), prev(1024), prev(2048),
                  nxt(0), nxt(1024), nxt(2048),
                  pl.BlockSpec((1, tt, 4 * A_LORA), lambda bi, i, q: (bi, i, OFF_LO // (4 * A_LORA))),
                  vec(3, 0), vec(3, 1024), vec(3, 2048),
                  vec(1), vec(1), vec(1), vec(2), vec(2),
                  pl.BlockSpec((2, 2 * A_LORA, QUAD), lambda bi, i, q: (0, 0, q)),
                  pl.BlockSpec((2, 2 * A_LORA, QUAD), lambda bi, i, q: (0, 0, q))],
        out_specs=[out_tok] * 9,
        out_shape=[out_sd] * 9,
        compiler_params=_cparams(("arbitrary", "arbitrary", "arbitrary")),
        name="rwkv_par",
    )(p, p, p, p, p, p, p, p, p, p, a_conv, a_conv, a_conv,
      k_k.reshape(1, D_MODEL), k_a.reshape(1, D_MODEL), r_k.reshape(1, D_MODEL), w0, a0, wup_pad, aup_pad)


def _rwkv_seq_kernel(rtf_ref, oif_ref, phf_ref, dlf_ref, rtb_ref, oib_ref, phb_ref, dlb_ref,
                     h0f_ref, h0b_ref, of_ref, ob_ref, hff_ref, hfb_ref, hf_s, hb_s, *, tt):
    i = pl.program_id(1)

    @pl.when(i == 0)
    def _():
        hf_s[...] = h0f_ref[0]
        hb_s[...] = h0b_ref[0]

    masks_b = _head_masks(A_CHUNK, BF16)
    nc = tt // A_CHUNK
    fwd_refs = (rtf_ref, phf_ref, dlf_ref, oif_ref, of_ref)
    bwd_refs = (rtb_ref, phb_ref, dlb_ref, oib_ref, ob_ref)

    n_pairs = D_MODEL // PAIR

    def pair(j):
        return slice(j * PAIR, (j + 1) * PAIR)

    hs = [hf_s[:, pair(j)] for j in range(n_pairs)] + [hb_s[:, pair(j)] for j in range(n_pairs)]
    for c in range(nc):
        chains = [(fwd_refs, c, j) for j in range(n_pairs)] + [(bwd_refs, nc - 1 - c, j) for j in range(n_pairs)]
        boths = []
        for h, (refs, cc, j) in zip(hs, chains):
            sl = slice(cc * A_CHUNK, (cc + 1) * A_CHUNK)
            lhs = jnp.concatenate([refs[0][0, sl, pair(j)], refs[1][0, sl, pair(j)]], axis=0)
            boths.append(_dot(lhs, _bd2(h.astype(BF16), masks_b)))
        new_hs = []
        for both, (refs, cc, j) in zip(boths, chains):
            sl = slice(cc * A_CHUNK, (cc + 1) * A_CHUNK)
            refs[4][0, sl, pair(j)] = (both[0:A_CHUNK] + refs[3][0, sl, pair(j)].astype(F32)).astype(refs[4].dtype)
            new_hs.append(both[A_CHUNK:] + refs[2][0, sl, pair(j)].astype(F32))
        hs = new_hs
    for j in range(n_pairs):
        hf_s[:, pair(j)] = hs[j]
        hb_s[:, pair(j)] = hs[n_pairs + j]
        hff_ref[0, :, pair(j)] = hs[j]
        hfb_ref[0, :, pair(j)] = hs[n_pairs + j]


def _rwkv_seq(par, h0f, h0b, tt):
    rtf, oif, phf, dlf, rtb, oib, phb, dlb = par
    b, l, _ = rtf.shape
    n_tiles = l // tt
    fwd = pl.BlockSpec((1, tt, D_MODEL), lambda bi, i: (bi, i, 0))
    bwd = pl.BlockSpec((1, tt, D_MODEL), lambda bi, i: (bi, n_tiles - 1 - i, 0))
    st = pl.BlockSpec((1, A_HEAD, D_MODEL), lambda bi, i: (bi, 0, 0))
    o_sd = jax.ShapeDtypeStruct((b, l, D_MODEL), BF16)
    h_sd = jax.ShapeDtypeStruct((b, A_HEAD, D_MODEL), F32)
    return pl.pallas_call(
        functools.partial(_rwkv_seq_kernel, tt=tt),
        grid=(b, n_tiles),
        in_specs=[fwd] * 4 + [bwd] * 4 + [st, st],
        out_specs=[fwd, bwd, st, st],
        out_shape=[o_sd, o_sd, h_sd, h_sd],
        scratch_shapes=[pltpu.VMEM((A_HEAD, D_MODEL), F32), pltpu.VMEM((A_HEAD, D_MODEL), F32)],
        compiler_params=_cparams(("arbitrary", "arbitrary")),
        name="rwkv_seq",
    )(rtf, oif, phf, dlf, rtb, oib, phb, dlb, h0f, h0b)


def _ret_kernel(lg_ref, q_ref, k_ref, v_ref, rc_ref, rs_ref, cc_ref, cs_ref, s0_ref, *rest,
                tt, reverse, with_intra, rotate, n_heads, n_tiles):
    rest = list(rest)
    prev_ref = rest.pop(0) if with_intra else None
    o_ref, sf_ref = rest[0:2]
    qr_ref, kr_ref = rest[2:4] if rotate else (None, None)
    s_s = rest[-1]
    bh = pl.program_id(0)
    i = pl.program_id(1)
    h = bh % n_heads

    @pl.when(i == 0)
    def _():
        s_s[...] = s0_ref[0]

    lg_f = lg_ref[0, h]
    lg_b = lg_ref[1, h]
    lg = lg_b if reverse else lg_f

    n_rows = tt // GRID_W
    r0 = (n_tiles - 1 - i if reverse else i) * n_rows

    def by_row(ref):
        return jnp.concatenate([jnp.broadcast_to(ref[pl.ds(r0 + j, 1), :], (GRID_W, 128)) for j in range(n_rows)], axis=0)

    def by_col(ref):
        return jnp.concatenate([ref[...]] * n_rows, axis=0)

    cos_r, sin_r, cos_c, sin_c = by_row(rc_ref), by_row(rs_ref), by_col(cc_ref), by_col(cs_ref)

    def rope(x_ref):
        x = x_ref[0].astype(F32)
        xa, xb = x[:, 0:128], x[:, 128:256]
        return jnp.concatenate([xa * cos_r + pltpu.roll(xa, 64, axis=1) * sin_r,
                                xb * cos_c + pltpu.roll(xb, 64, axis=1) * sin_c], axis=1)

    if not rotate:
        q = q_ref[0].astype(F32)
        k = k_ref[0].astype(F32)
    else:
        q = rope(q_ref)
        k = rope(k_ref) * (R_QK ** -0.5)
        qr_ref[0] = q.astype(qr_ref.dtype)
        kr_ref[0] = k.astype(kr_ref.dtype)

    idx = lax.broadcasted_iota(jnp.int32, (R_BLOCK, R_QK), 0).astype(F32)
    if reverse:
        q_dec = jnp.exp(lg * (R_BLOCK - idx))
        k_dec = jnp.exp(lg * idx)
    else:
        q_dec = jnp.exp(lg * (idx + 1.0))
        k_dec = jnp.exp(lg * (R_BLOCK - 1.0 - idx))
    c_dec = jnp.exp(lg * R_BLOCK)

    nc = tt // R_BLOCK
    order = [nc - 1 - cc if reverse else cc for cc in range(nc)]

    def blk(c):
        return slice(c * R_BLOCK, (c + 1) * R_BLOCK)

    if with_intra:
        di = lax.broadcasted_iota(jnp.int32, (R_BLOCK, R_BLOCK), 0)
        dj = lax.broadcasted_iota(jnp.int32, (R_BLOCK, R_BLOCK), 1)
        diff = (di - dj).astype(F32)
        dmask = (jnp.where(diff >= 0, jnp.exp(lg_f * jnp.maximum(diff, 0.0)), 0.0)
                 + jnp.where(diff <= 0, jnp.exp(lg_b * jnp.maximum(-diff, 0.0)), 0.0))
        scores = {c: _dot_nt(q[blk(c)].astype(BF16), k[blk(c)].astype(BF16)) for c in order}
    kvs = {c: _dot_tn((k[blk(c)] * k_dec).astype(BF16), v_ref[0, blk(c), :]) for c in order}

    s = s_s[...]
    inter = {}
    for c in order:
        inter[c] = _dot((q[blk(c)] * q_dec).astype(BF16), s.astype(BF16))
        s = s * c_dec + kvs[c]
    s_s[...] = s
    sf_ref[0] = s
    for c in order:
        o = inter[c]
        if with_intra:
            o = o + _dot((scores[c] * dmask).astype(BF16), v_ref[0, blk(c), :]) + prev_ref[0, blk(c), :].astype(F32)
        o_ref[0, blk(c), :] = o.astype(o_ref.dtype)


def _ret_pass(p, lg, tabs, s0, prev, tt, reverse, rotated=None):
    with_intra = prev is not None
    rotate = rotated is None
    b, l, _ = p.shape
    n_tiles = l // tt
    nh = R_HEADS

    def tile(i):
        return n_tiles - 1 - i if reverse else i

    def col(width, off):
        return pl.BlockSpec((1, tt, width), lambda bh, i, lg_: (bh // nh, tile(i), off // width + bh % nh))

    def whole(a):
        return pl.BlockSpec(a.shape, lambda bh, i, lg_: (0, 0))

    st = pl.BlockSpec((1, R_QK, R_V), lambda bh, i, lg_: (bh, 0, 0))
    o_spec = pl.BlockSpec((1, tt, R_V), lambda bh, i, lg_: (bh // nh, tile(i), bh % nh))
    rot_spec = pl.BlockSpec((1, tt, R_QK), lambda bh, i, lg_: (bh // nh, tile(i), bh % nh))
    rot_sd = jax.ShapeDtypeStruct((b, l, nh * R_QK), BF16)
    qk_specs = [col(R_QK, OFF_Q), col(R_QK, OFF_K)] if rotate else [rot_spec, rot_spec]
    qk_args = [p, p] if rotate else list(rotated)
    kern = functools.partial(_ret_kernel, tt=tt, reverse=reverse, with_intra=with_intra, rotate=rotate, n_heads=nh,
                             n_tiles=n_tiles)
    grid_spec = pltpu.PrefetchScalarGridSpec(
        num_scalar_prefetch=1,
        grid=(b * nh, n_tiles),
        in_specs=qk_specs + [col(R_V, OFF_V)] + [whole(t) for t in tabs] + [st]
        + ([o_spec] if with_intra else []),
        out_specs=[o_spec, st] + ([rot_spec, rot_spec] if rotate else []),
        scratch_shapes=[pltpu.VMEM((R_QK, R_V), F32)],
    )
    return pl.pallas_call(
        kern,
        grid_spec=grid_spec,
        out_shape=[jax.ShapeDtypeStruct((b, l, nh * R_V), BF16),
                   jax.ShapeDtypeStruct((b * nh, R_QK, R_V), F32)] + ([rot_sd, rot_sd] if rotate else []),
        compiler_params=_cparams(("arbitrary", "arbitrary")),
        name="ret_bwd" if reverse else "ret_fwd",
    )(lg, *qk_args, p, *tabs, s0, *([prev] if with_intra else []))


def _ret_ctx_kernel(lg_ref, k_ref, v_ref, sf_ref, sb_ref, *, n_heads):
    h = pl.program_id(0) % n_heads
    k = k_ref[0].astype(F32) * (R_QK ** -0.5)
    v = v_ref[0]
    n = k.shape[0]
    idx = lax.broadcasted_iota(jnp.int32, k.shape, 0).astype(F32)
    sf_ref[0] = _dot_tn((k * jnp.exp(lg_ref[0, h] * (n - 1.0 - idx))).astype(BF16), v)
    sb_ref[0] = _dot_tn((k * jnp.exp(lg_ref[1, h] * idx)).astype(BF16), v)


def _ret_ctx_states(p, lg):
    b, l, _ = p.shape
    nh = R_HEADS
    st = pl.BlockSpec((1, R_QK, R_V), lambda bh, lg_: (bh, 0, 0))
    sd = jax.ShapeDtypeStruct((b * nh, R_QK, R_V), F32)
    grid_spec = pltpu.PrefetchScalarGridSpec(
        num_scalar_prefetch=1,
        grid=(b * nh,),
        in_specs=[pl.BlockSpec((1, l, R_QK), lambda bh, lg_: (bh // nh, 0, OFF_K // R_QK + bh % nh)),
                  pl.BlockSpec((1, l, R_V), lambda bh, lg_: (bh // nh, 0, OFF_V // R_V + bh % nh))],
        out_specs=[st, st],
    )
    return pl.pallas_call(
        functools.partial(_ret_ctx_kernel, n_heads=nh),
        grid_spec=grid_spec,
        out_shape=[sd, sd],
        compiler_params=_cparams(("arbitrary",)),
        name="ret_ctx",
    )(lg, p, p)


def _final_kernel(x_ref, ga_ref, gr_ref, ma_ref, mb_ref, of_ref, ob_ref, bon_ref, or_ref,
                  gate_ref, alw_ref, alb_ref, rlw_ref, rlb_ref, fnw_ref, awo_ref, rwo_ref, wo_ref, o_ref):
    li = lax.broadcasted_iota(jnp.int32, (QUAD, QUAD), 0) // A_HEAD
    lj = lax.broadcasted_iota(jnp.int32, (QUAD, QUAD), 1) // A_HEAD
    msum = (li == lj).astype(BF16)
    inv = 1.0 / A_HEAD

    def head_sum(t):
        return jnp.concatenate([_dot(t[:, q * QUAD:(q + 1) * QUAD].astype(BF16), msum) for q in range(N_QUADS)], axis=1)

    oa = of_ref[0].astype(F32) + ob_ref[0].astype(F32)
    mu = head_sum(oa) * inv
    dv = oa - mu
    var = head_sum(dv * dv) * inv
    ya = dv * lax.rsqrt(var + A_GN_EPS) * alw_ref[...] + alb_ref[...] + bon_ref[0].astype(F32)
    ya = _dot(ya.astype(BF16) * _silu_half(ga_ref[0]), awo_ref[...])

    orr = or_ref[0].astype(F32)
    parts = []
    for h in range(R_HEADS):
        oh = orr[:, h * R_V:(h + 1) * R_V]
        m = jnp.mean(oh, axis=-1, keepdims=True)
        dh = oh - m
        vh = jnp.mean(dh * dh, axis=-1, keepdims=True)
        parts.append(dh * lax.rsqrt(vh + R_GN_EPS))
    yr = jnp.concatenate(parts, axis=1) * rlw_ref[...] + rlb_ref[...]
    yr = _dot(yr.astype(BF16) * _silu_half(gr_ref[0]), rwo_ref[...])

    merged = (1.0 + jnp.tanh(ma_ref[0])) * ya.astype(BF16) + (1.0 + jnp.tanh(mb_ref[0])) * yr.astype(BF16)
    y = _dot(merged, wo_ref[...])
    xo = x_ref[0] + gate_ref[0] * y
    o_ref[0] = xo * lax.rsqrt(jnp.mean(xo * xo, axis=-1, keepdims=True) + NORM_EPS) * fnw_ref[...]


def _final(x, p, of, ob, bonus, o_ret, gate, a_ln_w, a_ln_b, r_ln_w, r_ln_b, final_w, awo, rwo, wo, tm):
    b, l, d = x.shape

    def tok(width, off=0):
        return pl.BlockSpec((1, tm, width), lambda bi, i: (bi, i, off // width))

    def vec(width):
        return pl.BlockSpec((1, width), lambda bi, i: (0, 0))

    def mat(r, c):
        return pl.BlockSpec((r, c), lambda bi, i: (0, 0))

    rv = R_HEADS * R_V
    return pl.pallas_call(
        _final_kernel,
        grid=(b, l // tm),
        in_specs=[tok(d), tok(d, OFF_GA), tok(rv, OFF_GR), tok(d, OFF_MA), tok(d, OFF_MB),
                  tok(d), tok(d), tok(d), tok(rv),
                  pl.BlockSpec((1, 1, d), lambda bi, i: (bi, 0, 0)),
                  vec(d), vec(d), vec(rv), vec(rv), vec(d), mat(d, d), mat(rv, d), mat(d, d)],
        out_specs=tok(d),
        out_shape=jax.ShapeDtypeStruct((b, l, d), F32),
        compiler_params=_cparams(("arbitrary", "arbitrary")),
        name="final",
    )(x, p, p, p, p, of, ob, bonus, o_ret, gate,
      a_ln_w.reshape(1, d), a_ln_b.reshape(1, d), r_ln_w.reshape(1, rv), r_ln_b.reshape(1, rv),
      final_w.reshape(1, d), awo, rwo, wo)


def _rope_tables(l):
    half = R_QK // 4
    freqs = ROPE_BASE ** (-jnp.arange(half, dtype=F32) / half)

    def tab(n):
        ang = jnp.arange(n, dtype=F32)[:, None] * freqs[None, :]
        return (jnp.concatenate([jnp.cos(ang), jnp.cos(ang)], axis=1),
                jnp.concatenate([-jnp.sin(ang), jnp.sin(ang)], axis=1))

    return tab(l // GRID_W) + tab(GRID_W)


def _reorder_w_in(w):
    return jnp.concatenate([w[:, 0:3072], 0.5 * w[:, 3072:4096], w[:, 4352:8448], 0.5 * w[:, 8448:], w[:, 4096:4352]],
                           axis=1)


def _pick_tile(l, pref):
    t = min(l, pref)
    while l % t:
        t //= 2
    return t


def kernel(x, c, ctx, c_ctx, norm_w, ada_w, ada_b, w_in, a_conv, a_w_up, a_w0, a_a_up, a_a0, a_k_k, a_k_a,
           a_r_k, a_ln_w, a_ln_b, a_w_out, r_decay, r_ln_w, r_ln_b, r_w_out, w_out, final_norm_w):
    b, l, d = x.shape
    lc = ctx.shape[1]
    assert d == D_MODEL and l % R_BLOCK == 0 and lc % R_BLOCK == 0
    lyr = 0

    cc = jnp.zeros((16, d), F32).at[:b].set(c).at[b].set(c_ctx)
    mod = _adaln(cc, ada_w[lyr], ada_b[lyr])
    shift, scale, gate = mod[:b, :d], mod[:b, d:2 * d], mod[:b, 2 * d:]
    shift_c = jnp.broadcast_to(mod[b, :d], (b, d))
    scale_c = jnp.broadcast_to(mod[b, d:2 * d], (b, d))

    w_bf = _reorder_w_in(w_in[lyr]).astype(BF16)
    p_x = _inproj(x, (1.0 + scale)[:, None, :], shift[:, None, :], norm_w[lyr], w_bf, _pick_tile(l, 2048))
    p_c = _inproj(ctx.reshape(1, b * lc, d), (1.0 + scale_c)[:1, None, :], shift_c[:1, None, :], norm_w[lyr], w_bf,
                  _pick_tile(b * lc, 2048)).reshape(b, lc, -1)

    zpad = jnp.zeros((A_LORA, d), F32)
    wup_pad = jnp.stack([jnp.concatenate([a_w_up[lyr, 0], zpad], 0), jnp.concatenate([zpad, a_w_up[lyr, 1]], 0)])
    aup_pad = jnp.stack([jnp.concatenate([a_a_up[lyr, 0], zpad], 0), jnp.concatenate([zpad, a_a_up[lyr, 1]], 0)])
    a_args = (a_conv[lyr], a_k_k[lyr], a_k_a[lyr], a_r_k[lyr].reshape(-1), 0.5 * a_w0[lyr], 0.5 * a_a0[lyr],
              (0.5 * wup_pad).astype(BF16), (0.5 * aup_pad).astype(BF16))
    h_zero = jnp.zeros((b, A_HEAD, d), F32)
    par_c = _rwkv_par(p_c, *a_args, tt=_pick_tile(lc, 256))
    _, _, hcf, hcb = _rwkv_seq(par_c[:8], h_zero, h_zero, _pick_tile(lc, 512))
    par_x = _rwkv_par(p_x, *a_args, tt=_pick_tile(l, 1024))
    o_af, o_ab, _, _ = _rwkv_seq(par_x[:8], hcf, hcb, _pick_tile(l, 512))
    bonus = par_x[8]

    lg = -jnp.exp(r_decay[lyr].astype(F32))
    tabs_x = _rope_tables(l)
    sc_f, sc_b = _ret_ctx_states(p_c, lg)
    tx = _pick_tile(l, 2048)
    o_rb, _, q_rot, k_rot = _ret_pass(p_x, lg, tabs_x, sc_b, None, tx, True)
    o_r, _ = _ret_pass(p_x, lg, tabs_x, sc_f, o_rb, tx, False, (q_rot, k_rot))

    return _final(x, p_x, o_af, o_ab, bonus, o_r, gate[:, None, :], a_ln_w[lyr], a_ln_b[lyr],
                  r_ln_w[lyr], r_ln_b[lyr], final_norm_w, (0.5 * a_w_out[lyr]).astype(BF16),
                  (0.5 * r_w_out[lyr]).astype(BF16), w_out[lyr].astype(BF16), _pick_tile(l, 512))
```
